```python
import jax
import jax.numpy as jnp
from jax import lax
import numpy as np


D_MODEL = 2048
BATCH = 8
SEQ = 4096
DEPTH = 1

HEAD_DIM = 128
D_RNN = D_MODEL
LRU_BLOCK_WIDTH = 128
LRU_BLOCKS = D_RNN // LRU_BLOCK_WIDTH
CONV_WIDTH = 4
LRU_C = 8.0
ATTN_GROUPS = ((128, 1), (512, 4), (2048, 16))
HEADS_PER_GROUP = 4
N_ATTN_HEADS = HEADS_PER_GROUP * len(ATTN_GROUPS)
D_ATTN = N_ATTN_HEADS * HEAD_DIM
D_ATTN_OUT = HEADS_PER_GROUP * HEAD_DIM
IN_SPLITS = (D_RNN, D_RNN, D_ATTN, D_ATTN, D_ATTN, D_MODEL, D_MODEL)
D_IN = sum(IN_SPLITS)
N_EXPERTS = 256
TOP_K = 8
N_GROUP = 8
TOPK_GROUP = 4
D_EXPERT = D_MODEL // 4
ROUTED_SCALE = 2.5
EXPERT_BLOCK = 128
DN_ALPHA = (2 * DEPTH) ** 0.25
DN_BETA = (8 * DEPTH) ** -0.25
LN_EPS = 1e-5

kernel_name = 'hybrid_rglru_dilated_alibi_attn_moe_deepnorm'


def _layer_norm(x, g, b):
    xf = x.astype(jnp.float32)
    mu = jnp.mean(xf, axis=-1, keepdims=True)
    var = jnp.mean(jnp.square(xf - mu), axis=-1, keepdims=True)
    y = (xf - mu) * lax.rsqrt(var + LN_EPS) * g.astype(jnp.float32) + b.astype(jnp.float32)
    return y.astype(x.dtype)


def _in_proj(x, w_in, b_in, layer, part):
    lo = sum(IN_SPLITS[:part])
    hi = lo + IN_SPLITS[part]
    return jnp.einsum('bsd,de->bse', x, w_in[layer, :, lo:hi]) + b_in[layer, lo:hi]


def _rglru_branch(u, gate_in, conv_w, conv_b, wa, ba, wx, bx, lam):
    B, S, C = u.shape
    uc = lax.conv_general_dilated(
        u, conv_w[:, None, :], window_strides=(1,), padding=((CONV_WIDTH - 1, 0),),
        dimension_numbers=('NWC', 'WIO', 'NWC'), feature_group_count=C) + conv_b
    ub = uc.reshape(B, S, LRU_BLOCKS, LRU_BLOCK_WIDTH)
    r = jax.nn.sigmoid((jnp.einsum('bsni,nij->bsnj', ub, wa) + ba).astype(jnp.float32)).reshape(B, S, C)
    i = jax.nn.sigmoid((jnp.einsum('bsni,nij->bsnj', ub, wx) + bx).astype(jnp.float32)).reshape(B, S, C)
    log_a = -LRU_C * r * jax.nn.softplus(-lam.astype(jnp.float32))
    a = jnp.exp(log_a)
    mult = jnp.sqrt(-jnp.expm1(2.0 * log_a))
    mult = mult.at[:, 0].set(1.0)
    bt = uc.astype(jnp.float32) * i * mult

    def step(h, ab):
        a_t, b_t = ab
        h = a_t * h + b_t
        return h, h

    _, hs = lax.scan(step, jnp.zeros((B, C), jnp.float32),
                     (jnp.swapaxes(a, 0, 1), jnp.swapaxes(bt, 0, 1)))
    h = jnp.swapaxes(hs, 0, 1)
    return (h * jax.nn.gelu(gate_in.astype(jnp.float32))).astype(u.dtype)


def _dilated_window_group(q, k, v, slopes, window, dilation):
    B, S, H, Dh = q.shape
    band = window // dilation
    chunk = dilation * band
    Sp = -(-S // chunk) * chunk
    L = Sp // dilation
    nb = L // band

    def to_blocks(a):
        a = jnp.pad(a, ((0, 0), (0, Sp - S), (0, 0), (0, 0)))
        a = a.reshape(B, L, dilation, H, Dh).transpose(0, 2, 1, 3, 4)
        return a.reshape(B, dilation, nb, band, H, Dh)

    def with_prev(a):
        prev = jnp.pad(a, ((0, 0), (0, 0), (1, 0), (0, 0), (0, 0), (0, 0)))[:, :, :-1]
        return jnp.concatenate([prev, a], axis=3)

    qb = to_blocks(q).astype(jnp.float32)
    kk = with_prev(to_blocks(k)).astype(jnp.float32)
    vv = with_prev(to_blocks(v)).astype(jnp.float32)
    s = jnp.einsum('brnqhd,brnkhd->brnhqk', qb, kk) * (Dh ** -0.5)
    qi = jnp.arange(band)[:, None]
    kj = jnp.arange(2 * band)[None, :]
    dist = qi + band - kj
    ok = (dist >= 0) & (dist <= band)
    not_first = (jnp.arange(nb)[:, None, None] > 0) | (kj[None] >= band)
    mask = ok[None] & not_first
    alibi = -slopes.astype(jnp.float32)[:, None, None] * (dist * dilation).astype(jnp.float32)
    s = jnp.where(mask[:, None], s + alibi, -jnp.inf)
    lse = jax.nn.logsumexp(s, axis=-1)
    p = jnp.exp(s - lse[..., None])
    o = jnp.einsum('brnhqk,brnkhd->brnqhd', p, vv)
    o = o.reshape(B, dilation, L, H, Dh).transpose(0, 2, 1, 3, 4).reshape(B, Sp, H, Dh)[:, :S]
    lse = lse.transpose(0, 1, 2, 4, 3).reshape(B, dilation, L, H).transpose(0, 2, 1, 3).reshape(B, Sp, H)[:, :S]
    return o, lse


def _dilated_attention_branch(q, k, v):
    B, S, _ = q.shape
    q = q.reshape(B, S, N_ATTN_HEADS, HEAD_DIM)
    k = k.reshape(B, S, N_ATTN_HEADS, HEAD_DIM)
    v = v.reshape(B, S, N_ATTN_HEADS, HEAD_DIM)
    slopes = 2.0 ** (-8.0 * jnp.arange(1, N_ATTN_HEADS + 1, dtype=jnp.float32) / N_ATTN_HEADS)
    outs, lses = [], []
    for g, (window, dilation) in enumerate(ATTN_GROUPS):
        hs = slice(g * HEADS_PER_GROUP, (g + 1) * HEADS_PER_GROUP)
        o, l = _dilated_window_group(q[:, :, hs], k[:, :, hs], v[:, :, hs], slopes[hs], window, dilation)
        outs.append(o)
        lses.append(l)
    w = jax.nn.softmax(jnp.stack(lses, axis=0), axis=0)
    o = jnp.sum(w[..., None] * jnp.stack(outs, axis=0), axis=0)
    return o.reshape(B, S, D_ATTN_OUT).astype(q.dtype)


def _token_mixer(x, layer, w_in, b_in, conv_w, conv_b, lru_wa, lru_ba, lru_wx, lru_bx, lru_lambda,
                 w_rnn_br, w_attn_br, w_out):
    u_rnn = _in_proj(x, w_in, b_in, layer, 0)
    g_rnn = _in_proj(x, w_in, b_in, layer, 1)
    q = _in_proj(x, w_in, b_in, layer, 2)
    k = _in_proj(x, w_in, b_in, layer, 3)
    v = _in_proj(x, w_in, b_in, layer, 4)
    gate_r = _in_proj(x, w_in, b_in, layer, 5)
    gate_a = _in_proj(x, w_in, b_in, layer, 6)
    y_r = _rglru_branch(u_rnn, g_rnn, conv_w[layer], conv_b[layer], lru_wa[layer], lru_ba[layer],
                        lru_wx[layer], lru_bx[layer], lru_lambda[layer])
    y_a = _dilated_attention_branch(q, k, v)
    merged = (jax.nn.sigmoid(gate_r) * jnp.einsum('bsc,cd->bsd', y_r, w_rnn_br[layer])
              + jax.nn.sigmoid(gate_a) * jnp.einsum('bsc,cd->bsd', y_a, w_attn_br[layer]))
    return jnp.einsum('bsd,de->bse', merged, w_out[layer])


def _route(xt, w_router, router_bias):
    T = xt.shape[0]
    scores = jax.nn.sigmoid((xt @ w_router).astype(jnp.float32))
    biased = scores + router_bias.astype(jnp.float32)
    grp = biased.reshape(T, N_GROUP, N_EXPERTS // N_GROUP)
    grp_score = jnp.sum(lax.top_k(grp, 2)[0], axis=-1)
    _, top_g = lax.top_k(grp_score, TOPK_GROUP)
    keep = jnp.any(top_g[..., None] == jnp.arange(N_GROUP), axis=1)
    keep = jnp.repeat(keep, N_EXPERTS // N_GROUP, axis=1)
    _, idx = lax.top_k(jnp.where(keep, biased, -jnp.inf), TOP_K)
    w = jnp.take_along_axis(scores, idx, axis=1)
    w = w / jnp.sum(w, axis=-1, keepdims=True) * ROUTED_SCALE
    return idx.astype(jnp.int32), w


def _routed_experts(xt, idx, wts, layer, w_gate, w_up, w_down):
    T, D = xt.shape
    M = T * TOP_K
    flat_e = idx.reshape(M)
    flat_tok = jnp.arange(M, dtype=jnp.int32) // TOP_K
    flat_w = wts.reshape(M).astype(xt.dtype)
    order = jnp.argsort(flat_e)
    se, stok, sw = flat_e[order], flat_tok[order], flat_w[order]
    counts = jnp.bincount(flat_e, length=N_EXPERTS)
    start = jnp.cumsum(counts) - counts
    pcounts = (counts + EXPERT_BLOCK - 1) // EXPERT_BLOCK * EXPERT_BLOCK
    pend = jnp.cumsum(pcounts)
    pstart = pend - pcounts
    dest = pstart[se] + (jnp.arange(M) - start[se])
    n_blocks = -(-(M + N_EXPERTS * (EXPERT_BLOCK - 1)) // EXPERT_BLOCK)
    rows = n_blocks * EXPERT_BLOCK
    slot_tok = jnp.zeros((rows,), jnp.int32).at[dest].set(stok)
    slot_w = jnp.zeros((rows,), xt.dtype).at[dest].set(sw)
    block_e = jnp.minimum(jnp.searchsorted(pend, jnp.arange(n_blocks) * EXPERT_BLOCK, side='right'),
                          N_EXPERTS - 1).astype(jnp.int32)

    def step(y, blk):
        tok, w, e = blk
        xb = xt[tok]
        h = jax.nn.silu(xb @ w_gate[layer, e]) * (xb @ w_up[layer, e])
        return y.at[tok].add((h @ w_down[layer, e]) * w[:, None]), None

    y, _ = lax.scan(step, jnp.zeros_like(xt),
                    (slot_tok.reshape(n_blocks, EXPERT_BLOCK), slot_w.reshape(n_blocks, EXPERT_BLOCK), block_e))
    return y


def _moe(x, layer, w_router, router_bias, w_gate, w_up, w_down, ws_gate, ws_up, ws_down):
    B, S, D = x.shape
    xt = x.reshape(B * S, D)
    idx, wts = _route(xt, w_router[layer], router_bias[layer])
    routed = _routed_experts(xt, idx, wts, layer, w_gate, w_up, w_down)
    shared = (jax.nn.silu(xt @ ws_gate[layer]) * (xt @ ws_up[layer])) @ ws_down[layer]
    return (routed + shared).reshape(B, S, D)


def setup_inputs(seed: int = 0) -> dict:
    key = jax.random.key(seed)
    ks = jax.random.split(key, 26)
    f32 = jnp.float32

    def nrm(k, shape, fan_in, scale=1.0):
        return jax.random.normal(k, shape, f32) * (scale * fan_in ** -0.5)

    def small(k, shape, scale=0.02):
        return jax.random.normal(k, shape, f32) * scale

    L = DEPTH
    u = jax.random.uniform(ks[10], (L, D_RNN), f32, minval=0.9, maxval=0.999)
    lru_lambda = -jnp.log(jnp.expm1(-jnp.log(u) / LRU_C))
    return {
        'x': jax.random.normal(ks[0], (BATCH, SEQ, D_MODEL), f32),
        'w_in': nrm(ks[1], (L, D_MODEL, D_IN), D_MODEL),
        'b_in': small(ks[2], (L, D_IN)),
        'conv_w': nrm(ks[3], (L, CONV_WIDTH, D_RNN), CONV_WIDTH),
        'conv_b': small(ks[4], (L, D_RNN)),
        'lru_wa': nrm(ks[5], (L, LRU_BLOCKS, LRU_BLOCK_WIDTH, LRU_BLOCK_WIDTH), LRU_BLOCK_WIDTH),
        'lru_ba': small(ks[6], (L, LRU_BLOCKS, LRU_BLOCK_WIDTH)),
        'lru_wx': nrm(ks[7], (L, LRU_BLOCKS, LRU_BLOCK_WIDTH, LRU_BLOCK_WIDTH), LRU_BLOCK_WIDTH),
        'lru_bx': small(ks[8], (L, LRU_BLOCKS, LRU_BLOCK_WIDTH)),
        'lru_lambda': lru_lambda,
        'w_rnn_br': nrm(ks[11], (L, D_RNN, D_MODEL), D_RNN),
        'w_attn_br': nrm(ks[12], (L, D_ATTN_OUT, D_MODEL), D_ATTN_OUT),
        'w_out': nrm(ks[13], (L, D_MODEL, D_MODEL), D_MODEL, DN_BETA),
        'ln1_g': 1.0 + small(ks[14], (L, D_MODEL)),
        'ln1_b': small(ks[15], (L, D_MODEL)),
        'w_router': nrm(ks[16], (L, D_MODEL, N_EXPERTS), D_MODEL),
        'router_bias': small(ks[17], (L, N_EXPERTS), 0.01),
        'w_gate': nrm(ks[18], (L, N_EXPERTS, D_MODEL, D_EXPERT), D_MODEL),
        'w_up': nrm(ks[19], (L, N_EXPERTS, D_MODEL, D_EXPERT), D_MODEL),
        'w_down': nrm(ks[20], (L, N_EXPERTS, D_EXPERT, D_MODEL), D_EXPERT, DN_BETA),
        'ws_gate': nrm(ks[21], (L, D_MODEL, D_EXPERT), D_MODEL),
        'ws_up': nrm(ks[22], (L, D_MODEL, D_EXPERT), D_MODEL),
        'ws_down': nrm(ks[23], (L, D_EXPERT, D_MODEL), D_EXPERT, DN_BETA),
        'ln2_g': 1.0 + small(ks[24], (L, D_MODEL)),
        'ln2_b': small(ks[25], (L, D_MODEL)),
    }


def reference(x, w_in, b_in, conv_w, conv_b, lru_wa, lru_ba, lru_wx, lru_bx, lru_lambda,
              w_rnn_br, w_attn_br, w_out, ln1_g, ln1_b, w_router, router_bias,
              w_gate, w_up, w_down, ws_gate, ws_up, ws_down, ln2_g, ln2_b):
    for layer in range(DEPTH):
        mix = _token_mixer(x, layer, w_in, b_in, conv_w, conv_b, lru_wa, lru_ba, lru_wx, lru_bx,
                           lru_lambda, w_rnn_br, w_attn_br, w_out)
        x = _layer_norm(DN_ALPHA * x + mix, ln1_g[layer], ln1_b[layer])
        ffn = _moe(x, layer, w_router, router_bias, w_gate, w_up, w_down, ws_gate, ws_up, ws_down)
        x = _layer_norm(DN_ALPHA * x + ffn, ln2_g[layer], ln2_b[layer])
    return x
```

```python
import functools

import jax
import jax.numpy as jnp
from jax import lax
from jax.experimental import pallas as pl
from jax.experimental.pallas import tpu as pltpu

HEAD_DIM = 128
ATTN_GROUPS = ((128, 1), (512, 4), (2048, 16))
HEADS_PER_GROUP = 4
CONV_WIDTH = 4
LRU_C = 8.0
N_GROUP = 8
TOPK_GROUP = 4
TOP_K = 8
ROUTED_SCALE = 2.5
DEPTH = 1
DN_ALPHA = (2 * DEPTH) ** 0.25
LN_EPS = 1e-5

SUBLANES = 8
LANES = 128
V7X_VMEM_LIMIT_BYTES = 56 * 1024 * 1024

F32 = jnp.float32
BF16 = jnp.bfloat16


def _pick(n, cands):
    for c in cands:
        if n % c == 0:
            return c
    raise ValueError(f"no tile in {cands} divides {n}")


def _sigmoid(x):
    return 1.0 / (1.0 + jnp.exp(-x))


def _params(sem, vmem=V7X_VMEM_LIMIT_BYTES):
    return pltpu.CompilerParams(dimension_semantics=sem, vmem_limit_bytes=vmem)


def _time_major_rows(x_ref, tt, d):
    return jnp.concatenate([x_ref[:, t * d:(t + 1) * d] for t in range(tt)], axis=0)


def _in_proj_kernel(x_ref, w_ref, b_ref, z_ref, xb_ref, *, tt, d):
    @pl.when(pl.program_id(1) == 0)
    def _():
        for t in range(0, tt, 2):
            rows = jnp.concatenate([x_ref[:, t * d:(t + 1) * d], x_ref[:, (t + 1) * d:(t + 2) * d]], axis=0)
            xb_ref[t * SUBLANES:(t + 2) * SUBLANES, :] = rows.astype(BF16)

    z = jnp.dot(xb_ref[...], w_ref[...], preferred_element_type=F32) + b_ref[...]
    z_ref[...] = z.astype(z_ref.dtype)


def _in_proj(xv, w, b, n, d):
    d_in = w.shape[1]
    tm = _pick(n, (1024, 512, 256, 128, 64, 32, 16))
    tn = _pick(d_in, (1280, 1024, 512, 256, 128))
    tt = tm // SUBLANES
    return pl.pallas_call(
        functools.partial(_in_proj_kernel, tt=tt, d=d),
        out_shape=jax.ShapeDtypeStruct((n, d_in), BF16),
        grid=(n // tm, d_in // tn),
        in_specs=[
            pl.BlockSpec((SUBLANES, tt * d), lambda i, j: (0, i)),
            pl.BlockSpec((d, tn), lambda i, j: (0, j)),
            pl.BlockSpec((1, tn), lambda i, j: (0, j)),
        ],
        out_specs=pl.BlockSpec((tm, tn), lambda i, j: (i, j)),
        scratch_shapes=[pltpu.VMEM((tm, d), BF16)],
        compiler_params=_params(("arbitrary", "arbitrary")),
        name="in_proj",
    )(xv, w, b)


def _lru_kernel(u_ref, g_ref, cw_ref, cb_ref, wa_ref, ba_ref, wx_ref, bx_ref, lam_ref, y_ref,
                carry_ref, h_ref, a_ref, b_ref, *, tt, nblk, bw):
    ti = pl.program_id(1)
    rows = tt * SUBLANES
    halo = SUBLANES * (CONV_WIDTH - 1)

    @pl.when(ti == 0)
    def _():
        carry_ref[...] = jnp.zeros_like(carry_ref)
        h_ref[...] = jnp.zeros_like(h_ref)

    u = u_ref[...].astype(F32)
    ext = jnp.concatenate([carry_ref[...], u], axis=0)
    uc = cb_ref[...] + cw_ref[0:1, :] * ext[0:rows, :]
    for j in range(1, CONV_WIDTH):
        uc = uc + cw_ref[j:j + 1, :] * ext[SUBLANES * j:SUBLANES * j + rows, :]
    carry_ref[...] = u[rows - halo:, :]

    ucb = uc.astype(BF16)
    r_parts, i_parts = [], []
    for kb in range(nblk):
        blk = ucb[:, kb * bw:(kb + 1) * bw]
        r_parts.append(jnp.dot(blk, wa_ref[kb], preferred_element_type=F32))
        i_parts.append(jnp.dot(blk, wx_ref[kb], preferred_element_type=F32))
    r = _sigmoid(jnp.concatenate(r_parts, axis=1) + ba_ref[...])
    ig = _sigmoid(jnp.concatenate(i_parts, axis=1) + bx_ref[...])

    nl = -lam_ref[...]
    softplus = jnp.maximum(nl, 0.0) + jnp.log(1.0 + jnp.exp(-jnp.abs(nl)))
    a = jnp.exp((-LRU_C) * r * softplus)
    mult = jnp.sqrt(1.0 - a * a)
    row = lax.broadcasted_iota(jnp.int32, a.shape, 0)
    mult = jnp.where((row < SUBLANES) & (ti == 0), 1.0, mult)
    a_ref[...] = a
    b_ref[...] = uc * ig * mult

    def step(t, h):
        r0 = pl.multiple_of(t * SUBLANES, SUBLANES)
        h = a_ref[pl.ds(r0, SUBLANES), :] * h + b_ref[pl.ds(r0, SUBLANES), :]
        b_ref[pl.ds(r0, SUBLANES), :] = h
        return h

    h_ref[...] = lax.fori_loop(0, tt, step, h_ref[...], unroll=8)

    g = g_ref[...].astype(F32)
    gelu = 0.5 * g * (1.0 + jnp.tanh(0.7978845608028654 * (g + 0.044715 * (g * g * g))))
    y_ref[...] = (b_ref[...] * gelu).astype(y_ref.dtype)


def _lru(z, conv_w, conv_b, wa, ba, wx, bx, lam, n, c, col_u, col_g):
    nblk_total, bw, _ = wa.shape
    ct = _pick(c, (512, 256, 128))
    tt = _pick(n // SUBLANES, (128, 64, 32, 16, 8))
    rows = tt * SUBLANES
    nblk = ct // bw
    vec = lambda: pl.BlockSpec((1, ct), lambda ci, ti: (0, ci))
    return pl.pallas_call(
        functools.partial(_lru_kernel, tt=tt, nblk=nblk, bw=bw),
        out_shape=jax.ShapeDtypeStruct((n, c), BF16),
        grid=(c // ct, n // rows),
        in_specs=[
            pl.BlockSpec((rows, ct), lambda ci, ti: (ti, col_u // ct + ci)),
            pl.BlockSpec((rows, ct), lambda ci, ti: (ti, col_g // ct + ci)),
            pl.BlockSpec((CONV_WIDTH, ct), lambda ci, ti: (0, ci)),
            vec(),
            pl.BlockSpec((nblk, bw, bw), lambda ci, ti: (ci, 0, 0)),
            vec(),
            pl.BlockSpec((nblk, bw, bw), lambda ci, ti: (ci, 0, 0)),
            vec(),
            vec(),
        ],
        out_specs=pl.BlockSpec((rows, ct), lambda ci, ti: (ti, ci)),
        scratch_shapes=[
            pltpu.VMEM((SUBLANES * (CONV_WIDTH - 1), ct), F32),
            pltpu.VMEM((SUBLANES, ct), F32),
            pltpu.VMEM((rows, ct), F32),
            pltpu.VMEM((rows, ct), F32),
        ],
        compiler_params=_params(("arbitrary", "arbitrary")),
        name="rg_lru",
    )(z, z, conv_w, conv_b, wa, ba, wx, bx, lam)


def _attn_kernel(q_ref, k_ref, v_ref, o_ref, st_ref, *, band, dil, slopes):
    n = pl.program_id(1)
    prev = jnp.maximum(n - 1, 0)
    cur0 = pl.multiple_of(n * band, band)
    prev0 = pl.multiple_of(prev * band, band)
    q = q_ref[...]
    kc = k_ref[pl.ds(cur0, band), :]
    kp = k_ref[pl.ds(prev0, band), :]
    vc = v_ref[pl.ds(cur0, band), :]
    vp = v_ref[pl.ds(prev0, band), :]

    qi = lax.broadcasted_iota(jnp.int32, (band, band), 0)
    kj = lax.broadcasted_iota(jnp.int32, (band, band), 1)
    dist_c = (qi - kj).astype(F32)
    dist_p = dist_c + float(band)
    valid_c = kj <= qi
    valid_p = (kj >= qi) & (n > 0)
    scale = HEAD_DIM ** -0.5
    nt = (((1,), (1,)), ((), ()))
    lane = lax.broadcasted_iota(jnp.int32, (band, LANES), 1)

    outs = []
    stat = jnp.zeros((band, LANES), F32)
    for h in range(HEADS_PER_GROUP):
        sl = slice(h * HEAD_DIM, (h + 1) * HEAD_DIM)
        bias = slopes[h] * dil
        sc = lax.dot_general(q[:, sl], kc[:, sl], nt, preferred_element_type=F32)
        sp = lax.dot_general(q[:, sl], kp[:, sl], nt, preferred_element_type=F32)
        sc = jnp.where(valid_c, sc * scale - bias * dist_c, -jnp.inf)
        sp = jnp.where(valid_p, sp * scale - bias * dist_p, -jnp.inf)
        m = jnp.maximum(jnp.max(sc, axis=1, keepdims=True), jnp.max(sp, axis=1, keepdims=True))
        pc = jnp.exp(sc - m)
        pp = jnp.exp(sp - m)
        l = jnp.sum(pc, axis=1, keepdims=True) + jnp.sum(pp, axis=1, keepdims=True)
        acc = (jnp.dot(pc.astype(BF16), vc[:, sl], preferred_element_type=F32)
               + jnp.dot(pp.astype(BF16), vp[:, sl], preferred_element_type=F32))
        outs.append(acc / l)
        stat = jnp.where(lane == h, m + jnp.log(l), stat)
    o_ref[...] = jnp.concatenate(outs, axis=1).astype(o_ref.dtype)
    st_ref[...] = stat


def _attn_group(z, g, n, d_in, col_q, col_k, col_v):
    window, dil = ATTN_GROUPS[g]
    band = window // dil
    stride = SUBLANES * dil
    lp = n // stride
    gw = HEADS_PER_GROUP * HEAD_DIM
    wblk = d_in // gw
    zc = z.reshape(lp, stride * d_in)
    n_heads = HEADS_PER_GROUP * len(ATTN_GROUPS)
    slopes = tuple(2.0 ** (-8.0 * (g * HEADS_PER_GROUP + h + 1) / n_heads) for h in range(HEADS_PER_GROUP))
    qb, kb, vb = (col_q // gw + g, col_k // gw + g, col_v // gw + g)
    o, st = pl.pallas_call(
        functools.partial(_attn_kernel, band=band, dil=float(dil), slopes=slopes),
        out_shape=(jax.ShapeDtypeStruct((lp, stride * gw), BF16),
                   jax.ShapeDtypeStruct((lp, stride * LANES), F32)),
        grid=(stride, lp // band),
        in_specs=[
            pl.BlockSpec((band, gw), lambda c, i: (i, c * wblk + qb)),
            pl.BlockSpec((lp, gw), lambda c, i: (0, c * wblk + kb)),
            pl.BlockSpec((lp, gw), lambda c, i: (0, c * wblk + vb)),
        ],
        out_specs=(pl.BlockSpec((band, gw), lambda c, i: (i, c)),
                   pl.BlockSpec((band, LANES), lambda c, i: (i, c))),
        compiler_params=_params(("arbitrary", "arbitrary")),
        name=f"dilated_attn_g{g}",
    )(zc, zc, zc)
    return o.reshape(n, gw), st.reshape(n, LANES)


def _merge_kernel(x_ref, yr_ref, gr_ref, ga_ref, o0_ref, o1_ref, o2_ref, s0_ref, s1_ref, s2_ref,
                  wr_ref, wa_ref, wo_ref, g1_ref, b1_ref, x1_ref, *, tt, d):
    t1 = jnp.dot(yr_ref[...], wr_ref[...], preferred_element_type=F32)
    merged = _sigmoid(gr_ref[...].astype(F32)) * t1

    stats = [s0_ref[...], s1_ref[...], s2_ref[...]]
    outs = [o0_ref, o1_ref, o2_ref]
    heads = []
    for h in range(HEADS_PER_GROUP):
        lse = [s[:, h:h + 1] for s in stats]
        mx = jnp.maximum(jnp.maximum(lse[0], lse[1]), lse[2])
        e = [jnp.exp(v - mx) for v in lse]
        tot = e[0] + e[1] + e[2]
        acc = None
        for gi in range(len(ATTN_GROUPS)):
            term = (e[gi] / tot) * outs[gi][:, h * HEAD_DIM:(h + 1) * HEAD_DIM].astype(F32)
            acc = term if acc is None else acc + term
        heads.append(acc)
    oa = jnp.concatenate(heads, axis=1).astype(BF16)
    t2 = jnp.dot(oa, wa_ref[...], preferred_element_type=F32)
    merged = merged + _sigmoid(ga_ref[...].astype(F32)) * t2
    mix = jnp.dot(merged.astype(BF16), wo_ref[...], preferred_element_type=F32)

    v = DN_ALPHA * _time_major_rows(x_ref, tt, d) + mix
    mu = jnp.mean(v, axis=-1, keepdims=True)
    cen = v - mu
    var = jnp.mean(cen * cen, axis=-1, keepdims=True)
    x1_ref[...] = cen * lax.rsqrt(var + LN_EPS) * g1_ref[...] + b1_ref[...]


def _merge(xv, yr, z, os_, sts, wr, wa, wo, g1, b1, n, d, col_gr, col_ga):
    tm = _pick(n, (256, 128, 64))
    tt = tm // SUBLANES
    gw = HEADS_PER_GROUP * HEAD_DIM
    const = lambda shape: pl.BlockSpec(shape, lambda i: (0, 0), pipeline_mode=pl.Buffered(1))
    row = lambda w: pl.BlockSpec((tm, w), lambda i: (i, 0))
    return pl.pallas_call(
        functools.partial(_merge_kernel, tt=tt, d=d),
        out_shape=jax.ShapeDtypeStruct((n, d), F32),
        grid=(n // tm,),
        in_specs=[
            pl.BlockSpec((SUBLANES, tt * d), lambda i: (0, i)),
            row(yr.shape[1]),
            pl.BlockSpec((tm, d), lambda i: (i, col_gr // d)),
            pl.BlockSpec((tm, d), lambda i: (i, col_ga // d)),
            row(gw), row(gw), row(gw),
            row(LANES), row(LANES), row(LANES),
            const(wr.shape), const(wa.shape), const(wo.shape),
            const((1, d)), const((1, d)),
        ],
        out_specs=row(d),
        compiler_params=_params(("arbitrary",)),
        name="merge_ln1",
    )(xv, yr, z, z, *os_, *sts, wr, wa, wo, g1, b1)


def _router_kernel(x_ref, wr_ref, bias_ref, idx_ref, w_ref, rank_ref, cnt_ref, base_ref, tri_ref, *, t, e):
    i = pl.program_id(0)

    @pl.when(i == 0)
    def _():
        base_ref[...] = jnp.zeros_like(base_ref)
        rr = lax.broadcasted_iota(jnp.int32, (t, t), 0)
        cc = lax.broadcasted_iota(jnp.int32, (t, t), 1)
        tri_ref[...] = jnp.where(rr < cc, 1.0, 0.0).astype(BF16)

    nt = (((1,), (1,)), ((), ()))
    logits = lax.dot_general(wr_ref[...], x_ref[...].astype(BF16), nt, preferred_element_type=F32)
    scores = _sigmoid(logits)
    biased = scores + bias_ref[...]

    per = e // N_GROUP
    sub = lax.broadcasted_iota(jnp.int32, (per, t), 0).astype(F32)
    blocks, gscore = [], []
    for g in range(N_GROUP):
        blk = biased[g * per:(g + 1) * per, :]
        m1 = jnp.max(blk, axis=0, keepdims=True)
        first = jnp.min(jnp.where(blk == m1, sub, float(per)), axis=0, keepdims=True)
        m2 = jnp.max(jnp.where(sub == first, -jnp.inf, blk), axis=0, keepdims=True)
        blocks.append(blk)
        gscore.append(m1 + m2)
    masked_blocks = []
    for g in range(N_GROUP):
        beaten = jnp.zeros((1, t), F32)
        for g2 in range(N_GROUP):
            if g2 == g:
                continue
            wins = (gscore[g2] > gscore[g]) | ((gscore[g2] == gscore[g]) & (g2 < g))
            beaten = beaten + jnp.where(wins, 1.0, 0.0)
        keep = beaten < float(TOPK_GROUP)
        masked_blocks.append(jnp.where(keep, blocks[g], -jnp.inf))
    masked = jnp.concatenate(masked_blocks, axis=0)

    eidx = lax.broadcasted_iota(jnp.int32, (e, t), 0).astype(F32)
    member = jnp.zeros((e, t), F32)
    sels, ws = [], []
    for _ in range(TOP_K):
        m = jnp.max(masked, axis=0, keepdims=True)
        sel = jnp.min(jnp.where(masked == m, eidx, float(e)), axis=0, keepdims=True)
        hit = eidx == sel
        ws.append(jnp.sum(jnp.where(hit, scores, 0.0), axis=0, keepdims=True))
        masked = jnp.where(hit, -jnp.inf, masked)
        member = jnp.where(hit, 1.0, member)
        sels.append(sel)
    wsum = ws[0]
    for k in range(1, TOP_K):
        wsum = wsum + ws[k]

    before = jnp.dot(member.astype(BF16), tri_ref[...], preferred_element_type=F32) + base_ref[...]
    for k in range(TOP_K):
        idx_ref[k:k + 1, :] = sels[k].astype(jnp.int32)
        w_ref[k:k + 1, :] = ws[k] / wsum * ROUTED_SCALE
        rk = jnp.sum(jnp.where(eidx == sels[k], before, 0.0), axis=0, keepdims=True)
        rank_ref[k:k + 1, :] = rk.astype(jnp.int32)
    base_ref[...] = base_ref[...] + jnp.sum(member, axis=1, keepdims=True)
    cnt_ref[...] = jnp.broadcast_to(base_ref[...], cnt_ref.shape)


def _router(x1, wrt, bias, n, d):
    e = wrt.shape[0]
    t = _pick(n, (512, 256, 128))
    tok = lambda dt: jax.ShapeDtypeStruct((TOP_K, n), dt)
    return pl.pallas_call(
        functools.partial(_router_kernel, t=t, e=e),
        out_shape=(tok(jnp.int32), tok(F32), tok(jnp.int32), jax.ShapeDtypeStruct((e, LANES), F32)),
        grid=(n // t,),
        in_specs=[
            pl.BlockSpec((t, d), lambda i: (i, 0)),
            pl.BlockSpec((e, d), lambda i: (0, 0)),
            pl.BlockSpec((e, 1), lambda i: (0, 0)),
        ],
        out_specs=(pl.BlockSpec((TOP_K, t), lambda i: (0, i)),
                   pl.BlockSpec((TOP_K, t), lambda i: (0, i)),
                   pl.BlockSpec((TOP_K, t), lambda i: (0, i)),
                   pl.BlockSpec((e, LANES), lambda i: (0, 0))),
        scratch_shapes=[pltpu.VMEM((e, 1), F32), pltpu.VMEM((t, t), BF16)],
        compiler_params=_params(("arbitrary",)),
        name="router_topk",
    )(x1, wrt, bias)


def _dest_kernel(idx_ref, rank_ref, start_ref, dest_ref, *, t, e):
    eidx = lax.broadcasted_iota(jnp.int32, (e, t), 0)
    for k in range(TOP_K):
        hit = eidx == idx_ref[k:k + 1, :]
        st = jnp.sum(jnp.where(hit, start_ref[...], 0.0), axis=0, keepdims=True)
        dest_ref[0, k:k + 1, :] = st.astype(jnp.int32) + rank_ref[k:k + 1, :]


def _dest(idx, rank, start_col, n, t):
    e = start_col.shape[0]
    return pl.pallas_call(
        functools.partial(_dest_kernel, t=t, e=e),
        out_shape=jax.ShapeDtypeStruct((n // t, TOP_K, t), jnp.int32),
        grid=(n // t,),
        in_specs=[
            pl.BlockSpec((TOP_K, t), lambda i: (0, i)),
            pl.BlockSpec((TOP_K, t), lambda i: (0, i)),
            pl.BlockSpec((e, 1), lambda i: (0, 0)),
        ],
        out_specs=pl.BlockSpec((1, TOP_K, t), lambda i: (i, 0, 0)),
        compiler_params=_params(("arbitrary",)),
        name="moe_dest",
    )(idx, rank, start_col)


def _row_copy(src_ref, src_row, dst_ref, dst_row, sem):
    return pltpu.make_async_copy(src_ref.at[pl.ds(src_row, 1), :], dst_ref.at[pl.ds(dst_row, 1), :], sem)


def _dispatch_kernel(dest_hbm, x_ref, xs_hbm, dsm, sem_idx, sem, *, t):
    i = pl.program_id(0)
    cp = pltpu.make_async_copy(dest_hbm.at[i], dsm, sem_idx)
    cp.start()
    cp.wait()

    def issue(tok, c):
        for k in range(TOP_K):
            _row_copy(x_ref, tok, xs_hbm, dsm[k, tok], sem).start()
        return c

    lax.fori_loop(0, t, issue, 0)

    def drain(tok, c):
        for k in range(TOP_K):
            _row_copy(x_ref, tok, xs_hbm, dsm[k, tok], sem).wait()
        return c

    lax.fori_loop(0, t, drain, 0)


def _dispatch(dest3, x1, n, d, t):
    return pl.pallas_call(
        functools.partial(_dispatch_kernel, t=t),
        out_shape=jax.ShapeDtypeStruct((n * TOP_K, d), F32),
        grid=(n // t,),
        in_specs=[
            pl.BlockSpec(memory_space=pl.ANY),
            pl.BlockSpec((t, d), lambda i: (i, 0)),
        ],
        out_specs=pl.BlockSpec(memory_space=pl.ANY),
        scratch_shapes=[pltpu.SMEM((TOP_K, t), jnp.int32), pltpu.SemaphoreType.DMA, pltpu.SemaphoreType.DMA],
        compiler_params=_params(("arbitrary",)),
        name="moe_dispatch",
    )(dest3, x1)


def _expert_kernel(wb_ref, we_ref, nw_ref, st_ref, en_ref, xs_ref, wg_ref, wu_ref, wd_ref, o_ref,
                   wgb_ref, wub_ref, wdb_ref, *, tm):
    w = pl.program_id(0)

    @pl.when(w < nw_ref[0])
    def _():
        e = we_ref[w]
        b = wb_ref[w]
        pw = jnp.maximum(w - 1, 0)

        @pl.when((w == 0) | (e != we_ref[pw]))
        def _():
            wgb_ref[...] = wg_ref[...].astype(BF16)
            wub_ref[...] = wu_ref[...].astype(BF16)
            wdb_ref[...] = wd_ref[...].astype(BF16)

        x = xs_ref[...].astype(BF16)
        g = jnp.dot(x, wgb_ref[...], preferred_element_type=F32)
        u = jnp.dot(x, wub_ref[...], preferred_element_type=F32)
        h = (g * _sigmoid(g)) * u
        o = jnp.dot(h.astype(BF16), wdb_ref[...], preferred_element_type=F32)

        rows = b * tm + lax.broadcasted_iota(jnp.int32, (tm, 1), 0)
        mine = (rows >= st_ref[e]) & (rows < en_ref[e])
        first = (w == 0) | (b != wb_ref[pw])

        @pl.when(first)
        def _():
            o_ref[...] = jnp.where(mine, o, 0.0)

        @pl.when(jnp.logical_not(first))
        def _():
            o_ref[...] = jnp.where(mine, o, o_ref[...])


def _experts(wb, we, nw, start, end, xs, w_gate, w_up, w_down, tm):
    m, d = xs.shape
    de = w_gate.shape[2]
    n_work = wb.shape[0]
    return pl.pallas_call(
        functools.partial(_expert_kernel, tm=tm),
        out_shape=jax.ShapeDtypeStruct((m, d), F32),
        grid_spec=pltpu.PrefetchScalarGridSpec(
            num_scalar_prefetch=5,
            grid=(n_work,),
            in_specs=[
                pl.BlockSpec((tm, d), lambda w, wb, we, nw, st, en: (wb[w], 0)),
                pl.BlockSpec((None, d, de), lambda w, wb, we, nw, st, en: (we[w], 0, 0)),
                pl.BlockSpec((None, d, de), lambda w, wb, we, nw, st, en: (we[w], 0, 0)),
                pl.BlockSpec((None, de, d), lambda w, wb, we, nw, st, en: (we[w], 0, 0)),
            ],
            out_specs=pl.BlockSpec((tm, d), lambda w, wb, we, nw, st, en: (wb[w], 0)),
            scratch_shapes=[pltpu.VMEM((d, de), BF16), pltpu.VMEM((d, de), BF16), pltpu.VMEM((de, d), BF16)],
        ),
        compiler_params=_params(("arbitrary",)),
        name="moe_experts",
    )(wb, we, nw, start, end, xs, w_gate, w_up, w_down)


def _combine_kernel(dest_hbm, x1_ref, wt_ref, o_hbm, wsg_ref, wsu_ref, wsd_ref, g2_ref, b2_ref, y_ref,
                    dsm, gbuf, sem_idx, sem, *, t, tt, d):
    i = pl.program_id(0)
    cp = pltpu.make_async_copy(dest_hbm.at[i], dsm, sem_idx)
    cp.start()
    cp.wait()

    def issue(tok, c):
        for k in range(TOP_K):
            _row_copy(o_hbm, dsm[k, tok], gbuf.at[k], tok, sem).start()
        return c

    lax.fori_loop(0, t, issue, 0)

    x1 = x1_ref[...]
    xb = x1.astype(BF16)
    g = jnp.dot(xb, wsg_ref[...], preferred_element_type=F32)
    u = jnp.dot(xb, wsu_ref[...], preferred_element_type=F32)
    h = (g * _sigmoid(g)) * u
    shared = jnp.dot(h.astype(BF16), wsd_ref[...], preferred_element_type=F32)

    def drain(tok, c):
        for k in range(TOP_K):
            _row_copy(o_hbm, dsm[k, tok], gbuf.at[k], tok, sem).wait()
        return c

    lax.fori_loop(0, t, drain, 0)

    routed = wt_ref[:, 0:1] * gbuf[0]
    for k in range(1, TOP_K):
        routed = routed + wt_ref[:, k:k + 1] * gbuf[k]

    v = DN_ALPHA * x1 + (routed + shared)
    mu = jnp.mean(v, axis=-1, keepdims=True)
    cen = v - mu
    var = jnp.mean(cen * cen, axis=-1, keepdims=True)
    y = cen * lax.rsqrt(var + LN_EPS) * g2_ref[...] + b2_ref[...]
    for s in range(tt):
        y_ref[:, s * d:(s + 1) * d] = y[s * SUBLANES:(s + 1) * SUBLANES, :]


def _combine(dest3, x1, wt, o, wsg, wsu, wsd, g2, b2, n, d, t):
    tt = t // SUBLANES
    const = lambda shape: pl.BlockSpec(shape, lambda i: (0, 0), pipeline_mode=pl.Buffered(1))
    return pl.pallas_call(
        functools.partial(_combine_kernel, t=t, tt=tt, d=d),
        out_shape=jax.ShapeDtypeStruct((SUBLANES, (n // SUBLANES) * d), F32),
        grid=(n // t,),
        in_specs=[
            pl.BlockSpec(memory_space=pl.ANY),
            pl.BlockSpec((t, d), lambda i: (i, 0)),
            pl.BlockSpec((t, TOP_K), lambda i: (i, 0)),
            pl.BlockSpec(memory_space=pl.ANY),
            const(wsg.shape), const(wsu.shape), const(wsd.shape),
            const((1, d)), const((1, d)),
        ],
        out_specs=pl.BlockSpec((SUBLANES, tt * d), lambda i: (0, i)),
        scratch_shapes=[pltpu.SMEM((TOP_K, t), jnp.int32), pltpu.VMEM((TOP_K, t, d), F32),
                        pltpu.SemaphoreType.DMA, pltpu.SemaphoreType.DMA],
        compiler_params=_params(("arbitrary",)),
        name="moe_combine_ln2",
    )(dest3, x1, wt, o, wsg, wsu, wsd, g2, b2)


def _work_items(counts, m, tm):
    e = counts.shape[0]
    end = jnp.cumsum(counts)
    start = end - counts
    fb = start // tm
    lb = (end - 1) // tm
    nblk = jnp.where(counts > 0, lb - fb + 1, 0)
    cum = jnp.cumsum(nblk)
    off = cum - nblk
    n_work = m // tm + e - 1
    w = jnp.arange(n_work, dtype=jnp.int32)
    we = jnp.minimum(jnp.searchsorted(cum, w, side="right"), e - 1).astype(jnp.int32)
    wb = (fb[we] + (w - off[we])).astype(jnp.int32)
    nw = cum[-1].astype(jnp.int32)
    last = jnp.maximum(nw - 1, 0)
    valid = w < nw
    we = jnp.where(valid, we, we[last])
    wb = jnp.where(valid, wb, wb[last])
    return wb, we, nw.reshape(1), start.astype(jnp.int32), end.astype(jnp.int32)


def kernel(x, w_in, b_in, conv_w, conv_b, lru_wa, lru_ba, lru_wx, lru_bx, lru_lambda, w_rnn_br, w_attn_br, w_out, ln1_g, ln1_b, w_router, router_bias, w_gate, w_up, w_down, ws_gate, ws_up, ws_down, ln2_g, ln2_b):
    bsz, s, d = x.shape
    assert bsz == SUBLANES, "time-major rows need batch == 8"
    assert DEPTH == 1 and w_in.shape[0] == 1
    n = bsz * s
    c = lru_lambda.shape[1]
    a = HEADS_PER_GROUP * len(ATTN_GROUPS) * HEAD_DIM
    assert c == d and w_in.shape[2] == 2 * c + 3 * a + 2 * d
    layer = 0

    src = (0, c, 2 * c, 2 * c + a, 2 * c + 2 * a, 2 * c + 3 * a, 2 * c + 3 * a + d, 2 * c + 3 * a + 2 * d)
    order = (0, 1, 5, 6, 2, 3, 4)
    perm = lambda v: jnp.concatenate([v[..., src[p]:src[p + 1]] for p in order], axis=-1)
    w_cat = perm(w_in[layer]).astype(BF16)
    b_cat = perm(b_in[layer]).reshape(1, -1)
    col_u, col_g, col_gr, col_ga = 0, c, 2 * c, 2 * c + d
    col_q = 2 * c + 2 * d
    col_k, col_v = col_q + a, col_q + 2 * a
    d_in = w_cat.shape[1]
    row = lambda v: v.reshape(1, -1)

    xv = x.reshape(bsz, s * d)
    z = _in_proj(xv, w_cat, b_cat, n, d)

    yr = _lru(z, conv_w[layer], row(conv_b[layer]), lru_wa[layer].astype(BF16), row(lru_ba[layer]),
              lru_wx[layer].astype(BF16), row(lru_bx[layer]), row(lru_lambda[layer]), n, c, col_u, col_g)

    os_, sts = [], []
    for g in range(len(ATTN_GROUPS)):
        o, st = _attn_group(z, g, n, d_in, col_q, col_k, col_v)
        os_.append(o)
        sts.append(st)

    x1 = _merge(xv, yr, z, os_, sts, w_rnn_br[layer].astype(BF16), w_attn_br[layer].astype(BF16),
                w_out[layer].astype(BF16), row(ln1_g[layer]), row(ln1_b[layer]), n, d, col_gr, col_ga)

    n_exp = w_router.shape[2]
    idx, wts, rank, cnt = _router(x1, w_router[layer].T.astype(BF16), router_bias[layer].reshape(n_exp, 1), n, d)

    tm = 256
    t_tok = _pick(n, (256, 128))
    m = n * TOP_K
    counts = cnt[:, 0].astype(jnp.int32)
    wb, we, nw, start, end = _work_items(counts, m, tm)
    dest3 = _dest(idx, rank, start.astype(F32).reshape(n_exp, 1), n, t_tok)

    xs = _dispatch(dest3, x1, n, d, t_tok)
    o = _experts(wb, we, nw, start, end, xs, w_gate[layer], w_up[layer], w_down[layer], tm)
    y = _combine(dest3, x1, wts.T, o, ws_gate[layer].astype(BF16), ws_up[layer].astype(BF16),
                 ws_down[layer].astype(BF16), row(ln2_g[layer]), row(ln2_b[layer]), n, d, t_tok)
    return y.reshape(bsz, s, d)
```

```python
import functools

import jax
import jax.numpy as jnp
from jax import lax
from jax.experimental import pallas as pl
from jax.experimental.pallas import tpu as pltpu

HEAD_DIM = 128
ATTN_GROUPS = ((128, 1), (512, 4), (2048, 16))
HEADS_PER_GROUP = 4
CONV_WIDTH = 4
LRU_C = 8.0
N_GROUP = 8
TOPK_GROUP = 4
TOP_K = 8
ROUTED_SCALE = 2.5
DEPTH = 1
DN_ALPHA = (2 * DEPTH) ** 0.25
LN_EPS = 1e-5

SUBLANES = 8
LANES = 128
V7X_VMEM_LIMIT_BYTES = 56 * 1024 * 1024

F32 = jnp.float32
BF16 = jnp.bfloat16


def _pick(n, cands):
    for c in cands:
        if n % c == 0:
            return c
    raise ValueError(f"no tile in {cands} divides {n}")


def _sigmoid(x):
    return 1.0 / (1.0 + jnp.exp(-x))


def _params(sem, vmem=V7X_VMEM_LIMIT_BYTES):
    return pltpu.CompilerParams(dimension_semantics=sem, vmem_limit_bytes=vmem)


def _time_major_rows(x_ref, tt, d):
    return jnp.concatenate([x_ref[:, t * d:(t + 1) * d] for t in range(tt)], axis=0)


def _in_proj_kernel(x_ref, w_ref, b_ref, z_ref, xb_ref, *, tt, d):
    @pl.when(pl.program_id(1) == 0)
    def _():
        for t in range(0, tt, 2):
            rows = jnp.concatenate([x_ref[:, t * d:(t + 1) * d], x_ref[:, (t + 1) * d:(t + 2) * d]], axis=0)
            xb_ref[t * SUBLANES:(t + 2) * SUBLANES, :] = rows.astype(BF16)

    z = jnp.dot(xb_ref[...], w_ref[...], preferred_element_type=F32) + b_ref[...]
    z_ref[...] = z.astype(z_ref.dtype)


def _in_proj(xv, w, b, n, d):
    d_out = w.shape[1]
    tm = _pick(n, (1024, 512, 256, 128, 64, 32, 16))
    tn = _pick(d_out, (1024, 512, 256, 128))
    tt = tm // SUBLANES
    return pl.pallas_call(
        functools.partial(_in_proj_kernel, tt=tt, d=d),
        out_shape=(jax.ShapeDtypeStruct((n, d_out), BF16), jax.ShapeDtypeStruct((n, d), BF16)),
        grid=(n // tm, d_out // tn),
        in_specs=[
            pl.BlockSpec((SUBLANES, tt * d), lambda i, j: (0, i)),
            pl.BlockSpec((d, tn), lambda i, j: (0, j)),
            pl.BlockSpec((1, tn), lambda i, j: (0, j)),
        ],
        out_specs=(pl.BlockSpec((tm, tn), lambda i, j: (i, j)),
                   pl.BlockSpec((tm, d), lambda i, j: (i, 0))),
        compiler_params=_params(("arbitrary", "arbitrary")),
        name="in_proj",
    )(xv, w, b)


def _qkv_kernel(xb_ref, w_ref, b_ref, p_ref, o_ref):
    r = jnp.dot(xb_ref[...], w_ref[...], preferred_element_type=F32) + b_ref[...]
    rp = jnp.dot(p_ref[...], r.astype(BF16), preferred_element_type=F32)
    o_ref[...] = rp.astype(BF16).reshape(o_ref.shape)


def _qkv_proj(xb, w, b, g, n, d):
    _, dil = ATTN_GROUPS[g]
    stride = SUBLANES * dil
    gw = HEADS_PER_GROUP * HEAD_DIM
    tm = max(_pick(n, (1024, 512, 256)), 16 * stride)
    per = tm // stride
    new = jnp.arange(tm, dtype=jnp.int32)
    old = (new % per) * stride + new // per
    perm = (old[:, None] == jnp.arange(tm, dtype=jnp.int32)[None, :]).astype(BF16)
    return pl.pallas_call(
        _qkv_kernel,
        out_shape=jax.ShapeDtypeStruct((stride, n // stride, 3 * gw), BF16),
        grid=(n // tm, 3),
        in_specs=[
            pl.BlockSpec((tm, d), lambda i, j: (i, 0)),
            pl.BlockSpec((d, gw), lambda i, j: (0, j)),
            pl.BlockSpec((1, gw), lambda i, j: (0, j)),
            pl.BlockSpec((tm, tm), lambda i, j: (0, 0), pipeline_mode=pl.Buffered(1)),
        ],
        out_specs=pl.BlockSpec((stride, per, gw), lambda i, j: (0, i, j)),
        compiler_params=_params(("arbitrary", "arbitrary")),
        name=f"qkv_proj_g{g}",
    )(xb, w, b, perm)


def _lru_kernel(u_ref, g_ref, cw_ref, cb_ref, wa_ref, ba_ref, wx_ref, bx_ref, lam_ref, y_ref,
                carry_ref, h_ref, a_ref, b_ref, *, tt, nblk, bw):
    ti = pl.program_id(1)
    rows = tt * SUBLANES
    halo = SUBLANES * (CONV_WIDTH - 1)

    @pl.when(ti == 0)
    def _():
        carry_ref[...] = jnp.zeros_like(carry_ref)
        h_ref[...] = jnp.zeros_like(h_ref)

    u = u_ref[...].astype(F32)
    ext = jnp.concatenate([carry_ref[...], u], axis=0)
    uc = cb_ref[...] + cw_ref[0:1, :] * ext[0:rows, :]
    for j in range(1, CONV_WIDTH):
        uc = uc + cw_ref[j:j + 1, :] * ext[SUBLANES * j:SUBLANES * j + rows, :]
    carry_ref[...] = u[rows - halo:, :]

    ucb = uc.astype(BF16)
    r_parts, i_parts = [], []
    for kb in range(nblk):
        blk = ucb[:, kb * bw:(kb + 1) * bw]
        r_parts.append(jnp.dot(blk, wa_ref[kb], preferred_element_type=F32))
        i_parts.append(jnp.dot(blk, wx_ref[kb], preferred_element_type=F32))
    r = _sigmoid(jnp.concatenate(r_parts, axis=1) + ba_ref[...])
    ig = _sigmoid(jnp.concatenate(i_parts, axis=1) + bx_ref[...])

    nl = -lam_ref[...]
    softplus = jnp.maximum(nl, 0.0) + jnp.log(1.0 + jnp.exp(-jnp.abs(nl)))
    a = jnp.exp((-LRU_C) * r * softplus)
    mult = jnp.sqrt(1.0 - a * a)
    row = lax.broadcasted_iota(jnp.int32, a.shape, 0)
    mult = jnp.where((row < SUBLANES) & (ti == 0), 1.0, mult)
    a_ref[...] = a
    b_ref[...] = uc * ig * mult

    def step(t, h):
        r0 = pl.multiple_of(t * SUBLANES, SUBLANES)
        h = a_ref[pl.ds(r0, SUBLANES), :] * h + b_ref[pl.ds(r0, SUBLANES), :]
        b_ref[pl.ds(r0, SUBLANES), :] = h
        return h

    h_ref[...] = lax.fori_loop(0, tt, step, h_ref[...], unroll=8)

    g = g_ref[...].astype(F32)
    gelu = 0.5 * g * (1.0 + jnp.tanh(0.7978845608028654 * (g + 0.044715 * (g * g * g))))
    y_ref[...] = (b_ref[...] * gelu).astype(y_ref.dtype)


def _lru(z, conv_w, conv_b, wa, ba, wx, bx, lam, n, c, col_u, col_g):
    nblk_total, bw, _ = wa.shape
    ct = _pick(c, (512, 256, 128))
    tt = _pick(n // SUBLANES, (128, 64, 32, 16, 8))
    rows = tt * SUBLANES
    nblk = ct // bw
    vec = lambda: pl.BlockSpec((1, ct), lambda ci, ti: (0, ci))
    return pl.pallas_call(
        functools.partial(_lru_kernel, tt=tt, nblk=nblk, bw=bw),
        out_shape=jax.ShapeDtypeStruct((n, c), BF16),
        grid=(c // ct, n // rows),
        in_specs=[
            pl.BlockSpec((rows, ct), lambda ci, ti: (ti, col_u // ct + ci)),
            pl.BlockSpec((rows, ct), lambda ci, ti: (ti, col_g // ct + ci)),
            pl.BlockSpec((CONV_WIDTH, ct), lambda ci, ti: (0, ci)),
            vec(),
            pl.BlockSpec((nblk, bw, bw), lambda ci, ti: (ci, 0, 0)),
            vec(),
            pl.BlockSpec((nblk, bw, bw), lambda ci, ti: (ci, 0, 0)),
            vec(),
            vec(),
        ],
        out_specs=pl.BlockSpec((rows, ct), lambda ci, ti: (ti, ci)),
        scratch_shapes=[
            pltpu.VMEM((SUBLANES * (CONV_WIDTH - 1), ct), F32),
            pltpu.VMEM((SUBLANES, ct), F32),
            pltpu.VMEM((rows, ct), F32),
            pltpu.VMEM((rows, ct), F32),
        ],
        compiler_params=_params(("arbitrary", "arbitrary")),
        name="rg_lru",
    )(z, z, conv_w, conv_b, wa, ba, wx, bx, lam)


def _attn_kernel(q_ref, k_ref, v_ref, o_ref, st_ref, *, band, dil, slopes):
    n = pl.program_id(1)
    prev = jnp.maximum(n - 1, 0)
    cur0 = pl.multiple_of(n * band, band)
    prev0 = pl.multiple_of(prev * band, band)
    q = q_ref[...]
    kc = k_ref[pl.ds(cur0, band), :]
    kp = k_ref[pl.ds(prev0, band), :]
    vc = v_ref[pl.ds(cur0, band), :]
    vp = v_ref[pl.ds(prev0, band), :]

    qi = lax.broadcasted_iota(jnp.int32, (band, band), 0)
    kj = lax.broadcasted_iota(jnp.int32, (band, band), 1)
    dist_c = (qi - kj).astype(F32)
    dist_p = dist_c + float(band)
    valid_c = kj <= qi
    valid_p = (kj >= qi) & (n > 0)
    scale = HEAD_DIM ** -0.5
    nt = (((1,), (1,)), ((), ()))
    lane = lax.broadcasted_iota(jnp.int32, (band, LANES), 1)

    outs = []
    stat = jnp.zeros((band, LANES), F32)
    for h in range(HEADS_PER_GROUP):
        sl = slice(h * HEAD_DIM, (h + 1) * HEAD_DIM)
        bias = slopes[h] * dil
        sc = lax.dot_general(q[:, sl], kc[:, sl], nt, preferred_element_type=F32)
        sp = lax.dot_general(q[:, sl], kp[:, sl], nt, preferred_element_type=F32)
        sc = jnp.where(valid_c, sc * scale - bias * dist_c, -jnp.inf)
        sp = jnp.where(valid_p, sp * scale - bias * dist_p, -jnp.inf)
        m = jnp.maximum(jnp.max(sc, axis=1, keepdims=True), jnp.max(sp, axis=1, keepdims=True))
        pc = jnp.exp(sc - m)
        pp = jnp.exp(sp - m)
        l = jnp.sum(pc, axis=1, keepdims=True) + jnp.sum(pp, axis=1, keepdims=True)
        acc = (jnp.dot(pc.astype(BF16), vc[:, sl], preferred_element_type=F32)
               + jnp.dot(pp.astype(BF16), vp[:, sl], preferred_element_type=F32))
        outs.append(acc / l)
        stat = jnp.where(lane == h, m + jnp.log(l), stat)
    o_ref[...] = jnp.concatenate(outs, axis=1).astype(o_ref.dtype)
    st_ref[...] = stat


def _attn_group(qkv, g, n):
    window, dil = ATTN_GROUPS[g]
    band = window // dil
    stride = SUBLANES * dil
    lp = n // stride
    gw = HEADS_PER_GROUP * HEAD_DIM
    n_heads = HEADS_PER_GROUP * len(ATTN_GROUPS)
    slopes = tuple(2.0 ** (-8.0 * (g * HEADS_PER_GROUP + h + 1) / n_heads) for h in range(HEADS_PER_GROUP))
    o, st = pl.pallas_call(
        functools.partial(_attn_kernel, band=band, dil=float(dil), slopes=slopes),
        out_shape=(jax.ShapeDtypeStruct((lp, stride * gw), BF16),
                   jax.ShapeDtypeStruct((lp, stride * LANES), F32)),
        grid=(stride, lp // band),
        in_specs=[
            pl.BlockSpec((None, band, gw), lambda c, i: (c, i, 0)),
            pl.BlockSpec((None, lp, gw), lambda c, i: (c, 0, 1)),
            pl.BlockSpec((None, lp, gw), lambda c, i: (c, 0, 2)),
        ],
        out_specs=(pl.BlockSpec((band, gw), lambda c, i: (i, c)),
                   pl.BlockSpec((band, LANES), lambda c, i: (i, c))),
        compiler_params=_params(("arbitrary", "arbitrary")),
        name=f"dilated_attn_g{g}",
    )(qkv, qkv, qkv)
    return o.reshape(n, gw), st.reshape(n, LANES)


def _merge_kernel(x_ref, yr_ref, gr_ref, ga_ref, o0_ref, o1_ref, o2_ref, s0_ref, s1_ref, s2_ref,
                  wr_ref, wa_ref, wo_ref, g1_ref, b1_ref, x1_ref, *, tt, d):
    t1 = jnp.dot(yr_ref[...], wr_ref[...], preferred_element_type=F32)
    merged = _sigmoid(gr_ref[...].astype(F32)) * t1

    stats = [s0_ref[...], s1_ref[...], s2_ref[...]]
    outs = [o0_ref, o1_ref, o2_ref]
    heads = []
    for h in range(HEADS_PER_GROUP):
        lse = [s[:, h:h + 1] for s in stats]
        mx = jnp.maximum(jnp.maximum(lse[0], lse[1]), lse[2])
        e = [jnp.exp(v - mx) for v in lse]
        tot = e[0] + e[1] + e[2]
        acc = None
        for gi in range(len(ATTN_GROUPS)):
            term = (e[gi] / tot) * outs[gi][:, h * HEAD_DIM:(h + 1) * HEAD_DIM].astype(F32)
            acc = term if acc is None else acc + term
        heads.append(acc)
    oa = jnp.concatenate(heads, axis=1).astype(BF16)
    t2 = jnp.dot(oa, wa_ref[...], preferred_element_type=F32)
    merged = merged + _sigmoid(ga_ref[...].astype(F32)) * t2
    mix = jnp.dot(merged.astype(BF16), wo_ref[...], preferred_element_type=F32)

    v = DN_ALPHA * _time_major_rows(x_ref, tt, d) + mix
    mu = jnp.mean(v, axis=-1, keepdims=True)
    cen = v - mu
    var = jnp.mean(cen * cen, axis=-1, keepdims=True)
    x1_ref[...] = cen * lax.rsqrt(var + LN_EPS) * g1_ref[...] + b1_ref[...]


def _merge(xv, yr, z, os_, sts, wr, wa, wo, g1, b1, n, d, col_gr, col_ga):
    tm = _pick(n, (256, 128, 64))
    tt = tm // SUBLANES
    gw = HEADS_PER_GROUP * HEAD_DIM
    const = lambda shape: pl.BlockSpec(shape, lambda i: (0, 0), pipeline_mode=pl.Buffered(1))
    row = lambda w: pl.BlockSpec((tm, w), lambda i: (i, 0))
    return pl.pallas_call(
        functools.partial(_merge_kernel, tt=tt, d=d),
        out_shape=jax.ShapeDtypeStruct((n, d), F32),
        grid=(n // tm,),
        in_specs=[
            pl.BlockSpec((SUBLANES, tt * d), lambda i: (0, i)),
            row(yr.shape[1]),
            pl.BlockSpec((tm, d), lambda i: (i, col_gr // d)),
            pl.BlockSpec((tm, d), lambda i: (i, col_ga // d)),
            row(gw), row(gw), row(gw),
            row(LANES), row(LANES), row(LANES),
            const(wr.shape), const(wa.shape), const(wo.shape),
            const((1, d)), const((1, d)),
        ],
        out_specs=row(d),
        compiler_params=_params(("arbitrary",)),
        name="merge_ln1",
    )(xv, yr, z, z, *os_, *sts, wr, wa, wo, g1, b1)


def _router_kernel(x_ref, wr_ref, bias_ref, idx_ref, w_ref, rank_ref, cnt_ref, base_ref, tri_ref, *, t, e):
    i = pl.program_id(0)

    @pl.when(i == 0)
    def _():
        base_ref[...] = jnp.zeros_like(base_ref)
        rr = lax.broadcasted_iota(jnp.int32, (t, t), 0)
        cc = lax.broadcasted_iota(jnp.int32, (t, t), 1)
        tri_ref[...] = jnp.where(rr < cc, 1.0, 0.0).astype(BF16)

    nt = (((1,), (1,)), ((), ()))
    logits = lax.dot_general(wr_ref[...], x_ref[...].astype(BF16), nt, preferred_element_type=F32)
    scores = _sigmoid(logits)
    biased = scores + bias_ref[...]

    per = e // N_GROUP
    sub = lax.broadcasted_iota(jnp.int32, (per, t), 0).astype(F32)
    blocks, gscore = [], []
    for g in range(N_GROUP):
        blk = biased[g * per:(g + 1) * per, :]
        m1 = jnp.max(blk, axis=0, keepdims=True)
        first = jnp.min(jnp.where(blk == m1, sub, float(per)), axis=0, keepdims=True)
        m2 = jnp.max(jnp.where(sub == first, -jnp.inf, blk), axis=0, keepdims=True)
        blocks.append(blk)
        gscore.append(m1 + m2)
    masked_blocks = []
    for g in range(N_GROUP):
        beaten = jnp.zeros((1, t), F32)
        for g2 in range(N_GROUP):
            if g2 == g:
                continue
            wins = (gscore[g2] > gscore[g]) | ((gscore[g2] == gscore[g]) & (g2 < g))
            beaten = beaten + jnp.where(wins, 1.0, 0.0)
        keep = beaten < float(TOPK_GROUP)
        masked_blocks.append(jnp.where(keep, blocks[g], -jnp.inf))
    masked = jnp.concatenate(masked_blocks, axis=0)

    eidx = lax.broadcasted_iota(jnp.int32, (e, t), 0).astype(F32)
    member = jnp.zeros((e, t), F32)
    sels, ws = [], []
    for _ in range(TOP_K):
        m = jnp.max(masked, axis=0, keepdims=True)
        sel = jnp.min(jnp.where(masked == m, eidx, float(e)), axis=0, keepdims=True)
        hit = eidx == sel
        ws.append(jnp.sum(jnp.where(hit, scores, 0.0), axis=0, keepdims=True))
        masked = jnp.where(hit, -jnp.inf, masked)
        member = jnp.where(hit, 1.0, member)
        sels.append(sel)
    wsum = ws[0]
    for k in range(1, TOP_K):
        wsum = wsum + ws[k]

    before = jnp.dot(member.astype(BF16), tri_ref[...], preferred_element_type=F32) + base_ref[...]
    for k in range(TOP_K):
        idx_ref[k:k + 1, :] = sels[k].astype(jnp.int32)
        w_ref[k:k + 1, :] = ws[k] / wsum * ROUTED_SCALE
        rk = jnp.sum(jnp.where(eidx == sels[k], before, 0.0), axis=0, keepdims=True)
        rank_ref[k:k + 1, :] = rk.astype(jnp.int32)
    base_ref[...] = base_ref[...] + jnp.sum(member, axis=1, keepdims=True)
    cnt_ref[...] = jnp.broadcast_to(base_ref[...], cnt_ref.shape)


def _router(x1, wrt, bias, n, d):
    e = wrt.shape[0]
    t = _pick(n, (512, 256, 128))
    tok = lambda dt: jax.ShapeDtypeStruct((TOP_K, n), dt)
    return pl.pallas_call(
        functools.partial(_router_kernel, t=t, e=e),
        out_shape=(tok(jnp.int32), tok(F32), tok(jnp.int32), jax.ShapeDtypeStruct((e, LANES), F32)),
        grid=(n // t,),
        in_specs=[
            pl.BlockSpec((t, d), lambda i: (i, 0)),
            pl.BlockSpec((e, d), lambda i: (0, 0)),
            pl.BlockSpec((e, 1), lambda i: (0, 0)),
        ],
        out_specs=(pl.BlockSpec((TOP_K, t), lambda i: (0, i)),
                   pl.BlockSpec((TOP_K, t), lambda i: (0, i)),
                   pl.BlockSpec((TOP_K, t), lambda i: (0, i)),
                   pl.BlockSpec((e, LANES), lambda i: (0, 0))),
        scratch_shapes=[pltpu.VMEM((e, 1), F32), pltpu.VMEM((t, t), BF16)],
        compiler_params=_params(("arbitrary",)),
        name="router_topk",
    )(x1, wrt, bias)


def _dest_kernel(idx_ref, rank_ref, start_ref, dest_ref, *, t, e):
    eidx = lax.broadcasted_iota(jnp.int32, (e, t), 0)
    for k in range(TOP_K):
        hit = eidx == idx_ref[k:k + 1, :]
        st = jnp.sum(jnp.where(hit, start_ref[...], 0.0), axis=0, keepdims=True)
        dest_ref[0, k:k + 1, :] = st.astype(jnp.int32) + rank_ref[k:k + 1, :]


def _dest(idx, rank, start_col, n, t):
    e = start_col.shape[0]
    return pl.pallas_call(
        functools.partial(_dest_kernel, t=t, e=e),
        out_shape=jax.ShapeDtypeStruct((n // t, TOP_K, t), jnp.int32),
        grid=(n // t,),
        in_specs=[
            pl.BlockSpec((TOP_K, t), lambda i: (0, i)),
            pl.BlockSpec((TOP_K, t), lambda i: (0, i)),
            pl.BlockSpec((e, 1), lambda i: (0, 0)),
        ],
        out_specs=pl.BlockSpec((1, TOP_K, t), lambda i: (i, 0, 0)),
        compiler_params=_params(("arbitrary",)),
        name="moe_dest",
    )(idx, rank, start_col)


def _row_copy(src_ref, src_row, dst_ref, dst_row, sem):
    return pltpu.make_async_copy(src_ref.at[pl.ds(src_row, 1), :], dst_ref.at[pl.ds(dst_row, 1), :], sem)


def _dispatch_kernel(dest_hbm, x_ref, xs_hbm, dsm, sem_idx, sem, *, t):
    i = pl.program_id(0)
    cp = pltpu.make_async_copy(dest_hbm.at[i], dsm, sem_idx)
    cp.start()
    cp.wait()

    def issue(tok, c):
        for k in range(TOP_K):
            _row_copy(x_ref, tok, xs_hbm, dsm[k, tok], sem).start()
        return c

    lax.fori_loop(0, t, issue, 0)

    def drain(tok, c):
        for k in range(TOP_K):
            _row_copy(x_ref, tok, xs_hbm, dsm[k, tok], sem).wait()
        return c

    lax.fori_loop(0, t, drain, 0)


def _dispatch(dest3, x1, n, d, t):
    return pl.pallas_call(
        functools.partial(_dispatch_kernel, t=t),
        out_shape=jax.ShapeDtypeStruct((n * TOP_K, d), F32),
        grid=(n // t,),
        in_specs=[
            pl.BlockSpec(memory_space=pl.ANY),
            pl.BlockSpec((t, d), lambda i: (i, 0)),
        ],
        out_specs=pl.BlockSpec(memory_space=pl.ANY),
        scratch_shapes=[pltpu.SMEM((TOP_K, t), jnp.int32), pltpu.SemaphoreType.DMA, pltpu.SemaphoreType.DMA],
        compiler_params=_params(("arbitrary",)),
        name="moe_dispatch",
    )(dest3, x1)


def _expert_kernel(wb_ref, we_ref, nw_ref, st_ref, en_ref, xs_ref, wg_ref, wu_ref, wd_ref, o_ref,
                   wgb_ref, wub_ref, wdb_ref, *, tm):
    w = pl.program_id(0)

    @pl.when(w < nw_ref[0])
    def _():
        e = we_ref[w]
        b = wb_ref[w]
        pw = jnp.maximum(w - 1, 0)

        @pl.when((w == 0) | (e != we_ref[pw]))
        def _():
            wgb_ref[...] = wg_ref[...].astype(BF16)
            wub_ref[...] = wu_ref[...].astype(BF16)
            wdb_ref[...] = wd_ref[...].astype(BF16)

        x = xs_ref[...].astype(BF16)
        g = jnp.dot(x, wgb_ref[...], preferred_element_type=F32)
        u = jnp.dot(x, wub_ref[...], preferred_element_type=F32)
        h = (g * _sigmoid(g)) * u
        o = jnp.dot(h.astype(BF16), wdb_ref[...], preferred_element_type=F32)

        rows = b * tm + lax.broadcasted_iota(jnp.int32, (tm, 1), 0)
        mine = (rows >= st_ref[e]) & (rows < en_ref[e])
        first = (w == 0) | (b != wb_ref[pw])

        @pl.when(first)
        def _():
            o_ref[...] = jnp.where(mine, o, 0.0)

        @pl.when(jnp.logical_not(first))
        def _():
            o_ref[...] = jnp.where(mine, o, o_ref[...])


def _experts(wb, we, nw, start, end, xs, w_gate, w_up, w_down, tm):
    m, d = xs.shape
    de = w_gate.shape[2]
    n_work = wb.shape[0]
    return pl.pallas_call(
        functools.partial(_expert_kernel, tm=tm),
        out_shape=jax.ShapeDtypeStruct((m, d), F32),
        grid_spec=pltpu.PrefetchScalarGridSpec(
            num_scalar_prefetch=5,
            grid=(n_work,),
            in_specs=[
                pl.BlockSpec((tm, d), lambda w, wb, we, nw, st, en: (wb[w], 0)),
                pl.BlockSpec((None, d, de), lambda w, wb, we, nw, st, en: (we[w], 0, 0)),
                pl.BlockSpec((None, d, de), lambda w, wb, we, nw, st, en: (we[w], 0, 0)),
                pl.BlockSpec((None, de, d), lambda w, wb, we, nw, st, en: (we[w], 0, 0)),
            ],
            out_specs=pl.BlockSpec((tm, d), lambda w, wb, we, nw, st, en: (wb[w], 0)),
            scratch_shapes=[pltpu.VMEM((d, de), BF16), pltpu.VMEM((d, de), BF16), pltpu.VMEM((de, d), BF16)],
        ),
        compiler_params=_params(("arbitrary",)),
        name="moe_experts",
    )(wb, we, nw, start, end, xs, w_gate, w_up, w_down)


def _combine_kernel(dest_hbm, x1_ref, wt_ref, o_hbm, wsg_ref, wsu_ref, wsd_ref, g2_ref, b2_ref, y_ref,
                    dsm, gbuf, sem_idx, sem, *, t, tt, d):
    i = pl.program_id(0)
    cp = pltpu.make_async_copy(dest_hbm.at[i], dsm, sem_idx)
    cp.start()
    cp.wait()

    def issue(tok, c):
        for k in range(TOP_K):
            _row_copy(o_hbm, dsm[k, tok], gbuf.at[k], tok, sem).start()
        return c

    lax.fori_loop(0, t, issue, 0)

    x1 = x1_ref[...]
    xb = x1.astype(BF16)
    g = jnp.dot(xb, wsg_ref[...], preferred_element_type=F32)
    u = jnp.dot(xb, wsu_ref[...], preferred_element_type=F32)
    h = (g * _sigmoid(g)) * u
    shared = jnp.dot(h.astype(BF16), wsd_ref[...], preferred_element_type=F32)

    def drain(tok, c):
        for k in range(TOP_K):
            _row_copy(o_hbm, dsm[k, tok], gbuf.at[k], tok, sem).wait()
        return c

    lax.fori_loop(0, t, drain, 0)

    routed = wt_ref[:, 0:1] * gbuf[0]
    for k in range(1, TOP_K):
        routed = routed + wt_ref[:, k:k + 1] * gbuf[k]

    v = DN_ALPHA * x1 + (routed + shared)
    mu = jnp.mean(v, axis=-1, keepdims=True)
    cen = v - mu
    var = jnp.mean(cen * cen, axis=-1, keepdims=True)
    y = cen * lax.rsqrt(var + LN_EPS) * g2_ref[...] + b2_ref[...]
    for s in range(tt):
        y_ref[:, s * d:(s + 1) * d] = y[s * SUBLANES:(s + 1) * SUBLANES, :]


def _combine(dest3, x1, wt, o, wsg, wsu, wsd, g2, b2, n, d, t):
    tt = t // SUBLANES
    const = lambda shape: pl.BlockSpec(shape, lambda i: (0, 0), pipeline_mode=pl.Buffered(1))
    return pl.pallas_call(
        functools.partial(_combine_kernel, t=t, tt=tt, d=d),
        out_shape=jax.ShapeDtypeStruct((SUBLANES, (n // SUBLANES) * d), F32),
        grid=(n // t,),
        in_specs=[
            pl.BlockSpec(memory_space=pl.ANY),
            pl.BlockSpec((t, d), lambda i: (i, 0)),
            pl.BlockSpec((t, TOP_K), lambda i: (i, 0)),
            pl.BlockSpec(memory_space=pl.ANY),
            const(wsg.shape), const(wsu.shape), const(wsd.shape),
            const((1, d)), const((1, d)),
        ],
        out_specs=pl.BlockSpec((SUBLANES, tt * d), lambda i: (0, i)),
        scratch_shapes=[pltpu.SMEM((TOP_K, t), jnp.int32), pltpu.VMEM((TOP_K, t, d), F32),
                        pltpu.SemaphoreType.DMA, pltpu.SemaphoreType.DMA],
        compiler_params=_params(("arbitrary",)),
        name="moe_combine_ln2",
    )(dest3, x1, wt, o, wsg, wsu, wsd, g2, b2)


def _work_items(counts, m, tm):
    e = counts.shape[0]
    end = jnp.cumsum(counts)
    start = end - counts
    fb = start // tm
    lb = (end - 1) // tm
    nblk = jnp.where(counts > 0, lb - fb + 1, 0)
    cum = jnp.cumsum(nblk)
    off = cum - nblk
    n_work = m // tm + e - 1
    w = jnp.arange(n_work, dtype=jnp.int32)
    we = jnp.minimum(jnp.searchsorted(cum, w, side="right"), e - 1).astype(jnp.int32)
    wb = (fb[we] + (w - off[we])).astype(jnp.int32)
    nw = cum[-1].astype(jnp.int32)
    last = jnp.maximum(nw - 1, 0)
    valid = w < nw
    we = jnp.where(valid, we, we[last])
    wb = jnp.where(valid, wb, wb[last])
    return wb, we, nw.reshape(1), start.astype(jnp.int32), end.astype(jnp.int32)


def kernel(x, w_in, b_in, conv_w, conv_b, lru_wa, lru_ba, lru_wx, lru_bx, lru_lambda, w_rnn_br, w_attn_br, w_out, ln1_g, ln1_b, w_router, router_bias, w_gate, w_up, w_down, ws_gate, ws_up, ws_down, ln2_g, ln2_b):
    bsz, s, d = x.shape
    assert bsz == SUBLANES, "time-major rows need batch == 8"
    assert DEPTH == 1 and w_in.shape[0] == 1
    n = bsz * s
    c = lru_lambda.shape[1]
    a = HEADS_PER_GROUP * len(ATTN_GROUPS) * HEAD_DIM
    assert c == d and w_in.shape[2] == 2 * c + 3 * a + 2 * d
    layer = 0

    gw = HEADS_PER_GROUP * HEAD_DIM
    src = (0, c, 2 * c, 2 * c + a, 2 * c + 2 * a, 2 * c + 3 * a, 2 * c + 3 * a + d, 2 * c + 3 * a + 2 * d)
    part = lambda v, p, lo=0, hi=None: v[..., src[p]:src[p + 1]][..., lo:hi]
    cat = lambda v, pieces: jnp.concatenate([part(v, *p) for p in pieces], axis=-1)
    main = ((0,), (1,), (5,), (6,))
    w_main = cat(w_in[layer], main).astype(BF16)
    b_main = cat(b_in[layer], main).reshape(1, -1)
    col_u, col_g, col_gr, col_ga = 0, c, 2 * c, 2 * c + d
    row = lambda v: v.reshape(1, -1)

    xv = x.reshape(bsz, s * d)
    z, xb = _in_proj(xv, w_main, b_main, n, d)

    yr = _lru(z, conv_w[layer], row(conv_b[layer]), lru_wa[layer].astype(BF16), row(lru_ba[layer]),
              lru_wx[layer].astype(BF16), row(lru_bx[layer]), row(lru_lambda[layer]), n, c, col_u, col_g)

    os_, sts = [], []
    for g in range(len(ATTN_GROUPS)):
        cols = tuple((p, g * gw, (g + 1) * gw) for p in (2, 3, 4))
        qkv = _qkv_proj(xb, cat(w_in[layer], cols).astype(BF16), cat(b_in[layer], cols).reshape(1, -1), g, n, d)
        o, st = _attn_group(qkv, g, n)
        os_.append(o)
        sts.append(st)

    x1 = _merge(xv, yr, z, os_, sts, w_rnn_br[layer].astype(BF16), w_attn_br[layer].astype(BF16),
                w_out[layer].astype(BF16), row(ln1_g[layer]), row(ln1_b[layer]), n, d, col_gr, col_ga)

    n_exp = w_router.shape[2]
    idx, wts, rank, cnt = _router(x1, w_router[layer].T.astype(BF16), router_bias[layer].reshape(n_exp, 1), n, d)

    tm = 256
    t_tok = _pick(n, (256, 128))
    m = n * TOP_K
    counts = cnt[:, 0].astype(jnp.int32)
    wb, we, nw, start, end = _work_items(counts, m, tm)
    dest3 = _dest(idx, rank, start.astype(F32).reshape(n_exp, 1), n, t_tok)

    xs = _dispatch(dest3, x1, n, d, t_tok)
    o = _experts(wb, we, nw, start, end, xs, w_gate[layer], w_up[layer], w_down[layer], tm)
    y = _combine(dest3, x1, wts.T, o, ws_gate[layer].astype(BF16), ws_up[layer].astype(BF16),
                 ws_down[layer].astype(BF16), row(ln2_g[layer]), row(ln2_b[layer]), n, d, t_tok)
    return y.reshape(bsz, s, d)
```

```python
import functools

import jax
import jax.numpy as jnp
from jax import lax
from jax.experimental import pallas as pl
from jax.experimental.pallas import tpu as pltpu

HEAD_DIM = 128
ATTN_GROUPS = ((128, 1), (512, 4), (2048, 16))
HEADS_PER_GROUP = 4
CONV_WIDTH = 4
LRU_C = 8.0
N_GROUP = 8
TOPK_GROUP = 4
TOP_K = 8
ROUTED_SCALE = 2.5
DEPTH = 1
DN_ALPHA = (2 * DEPTH) ** 0.25
LN_EPS = 1e-5

SUBLANES = 8
LANES = 128
V7X_VMEM_LIMIT_BYTES = 56 * 1024 * 1024

F32 = jnp.float32
BF16 = jnp.bfloat16


def _pick(n, cands):
    for c in cands:
        if n % c == 0:
            return c
    raise ValueError(f"no tile in {cands} divides {n}")


def _sigmoid(x):
    return 1.0 / (1.0 + jnp.exp(-x))


def _params(sem, vmem=V7X_VMEM_LIMIT_BYTES):
    return pltpu.CompilerParams(dimension_semantics=sem, vmem_limit_bytes=vmem)


def _time_major_rows(x_ref, tt, d):
    return jnp.concatenate([x_ref[:, t * d:(t + 1) * d] for t in range(tt)], axis=0)


U32 = jnp.uint32
HI_MASK = 0xFFFF0000


def _rows_per_token(d):
    assert d % (2 * LANES) == 0
    return d // (2 * LANES)


def _token_rows(base, m, rpt, j):
    return pl.ds(base + j, m, stride=rpt)


def _pack_words(x):
    half = x.shape[1] // 2
    bits = pltpu.bitcast(x.astype(BF16).astype(F32), U32)
    words = (bits[:, :half] >> 16) | (bits[:, half:] & U32(HI_MASK))
    return [words[:, j * LANES:(j + 1) * LANES] for j in range(half // LANES)]


def _store_packed(ref, base, x):
    chunks = _pack_words(x)
    for j, chunk in enumerate(chunks):
        ref[_token_rows(base, x.shape[0], len(chunks), j), :] = chunk


def _unpack_words(chunks):
    lo = [pltpu.bitcast(c << 16, F32) for c in chunks]
    hi = [pltpu.bitcast(c & U32(HI_MASK), F32) for c in chunks]
    return jnp.concatenate(lo + hi, axis=1)


def _in_proj_kernel(x_ref, w_ref, b_ref, z_ref, xb_ref, *, tt, d):
    @pl.when(pl.program_id(1) == 0)
    def _():
        for t in range(0, tt, 2):
            rows = jnp.concatenate([x_ref[:, t * d:(t + 1) * d], x_ref[:, (t + 1) * d:(t + 2) * d]], axis=0)
            xb_ref[t * SUBLANES:(t + 2) * SUBLANES, :] = rows.astype(BF16)

    z = jnp.dot(xb_ref[...], w_ref[...], preferred_element_type=F32) + b_ref[...]
    z_ref[...] = z.astype(z_ref.dtype)


def _in_proj(xv, w, b, n, d):
    d_out = w.shape[1]
    tm = _pick(n, (1024, 512, 256, 128, 64, 32, 16))
    tn = _pick(d_out, (1024, 512, 256, 128))
    tt = tm // SUBLANES
    return pl.pallas_call(
        functools.partial(_in_proj_kernel, tt=tt, d=d),
        out_shape=(jax.ShapeDtypeStruct((n, d_out), BF16), jax.ShapeDtypeStruct((n, d), BF16)),
        grid=(n // tm, d_out // tn),
        in_specs=[
            pl.BlockSpec((SUBLANES, tt * d), lambda i, j: (0, i)),
            pl.BlockSpec((d, tn), lambda i, j: (0, j)),
            pl.BlockSpec((1, tn), lambda i, j: (0, j)),
        ],
        out_specs=(pl.BlockSpec((tm, tn), lambda i, j: (i, j)),
                   pl.BlockSpec((tm, d), lambda i, j: (i, 0))),
        compiler_params=_params(("arbitrary", "arbitrary")),
        name="in_proj",
    )(xv, w, b)


def _qkv_kernel(xb_ref, w_ref, b_ref, p_ref, o_ref):
    r = jnp.dot(xb_ref[...], w_ref[...], preferred_element_type=F32) + b_ref[...]
    rp = jnp.dot(p_ref[...], r.astype(BF16), preferred_element_type=F32)
    o_ref[...] = rp.astype(BF16).reshape(o_ref.shape)


def _qkv_proj(xb, w, b, g, n, d):
    _, dil = ATTN_GROUPS[g]
    stride = SUBLANES * dil
    gw = HEADS_PER_GROUP * HEAD_DIM
    tm = max(_pick(n, (1024, 512, 256)), 16 * stride)
    per = tm // stride
    new = jnp.arange(tm, dtype=jnp.int32)
    old = (new % per) * stride + new // per
    perm = (old[:, None] == jnp.arange(tm, dtype=jnp.int32)[None, :]).astype(BF16)
    return pl.pallas_call(
        _qkv_kernel,
        out_shape=jax.ShapeDtypeStruct((stride, n // stride, 3 * gw), BF16),
        grid=(n // tm, 3),
        in_specs=[
            pl.BlockSpec((tm, d), lambda i, j: (i, 0)),
            pl.BlockSpec((d, gw), lambda i, j: (0, j)),
            pl.BlockSpec((1, gw), lambda i, j: (0, j)),
            pl.BlockSpec((tm, tm), lambda i, j: (0, 0), pipeline_mode=pl.Buffered(1)),
        ],
        out_specs=pl.BlockSpec((stride, per, gw), lambda i, j: (0, i, j)),
        compiler_params=_params(("arbitrary", "arbitrary")),
        name=f"qkv_proj_g{g}",
    )(xb, w, b, perm)


def _lru_kernel(u_ref, g_ref, cw_ref, cb_ref, wa_ref, ba_ref, wx_ref, bx_ref, lam_ref, y_ref,
                carry_ref, h_ref, a_ref, b_ref, *, tt, nblk, bw):
    ti = pl.program_id(1)
    rows = tt * SUBLANES
    halo = SUBLANES * (CONV_WIDTH - 1)

    @pl.when(ti == 0)
    def _():
        carry_ref[...] = jnp.zeros_like(carry_ref)
        h_ref[...] = jnp.zeros_like(h_ref)

    u = u_ref[...].astype(F32)
    ext = jnp.concatenate([carry_ref[...], u], axis=0)
    uc = cb_ref[...] + cw_ref[0:1, :] * ext[0:rows, :]
    for j in range(1, CONV_WIDTH):
        uc = uc + cw_ref[j:j + 1, :] * ext[SUBLANES * j:SUBLANES * j + rows, :]
    carry_ref[...] = u[rows - halo:, :]

    ucb = uc.astype(BF16)
    r_parts, i_parts = [], []
    for kb in range(nblk):
        blk = ucb[:, kb * bw:(kb + 1) * bw]
        r_parts.append(jnp.dot(blk, wa_ref[kb], preferred_element_type=F32))
        i_parts.append(jnp.dot(blk, wx_ref[kb], preferred_element_type=F32))
    r = _sigmoid(jnp.concatenate(r_parts, axis=1) + ba_ref[...])
    ig = _sigmoid(jnp.concatenate(i_parts, axis=1) + bx_ref[...])

    nl = -lam_ref[...]
    softplus = jnp.maximum(nl, 0.0) + jnp.log(1.0 + jnp.exp(-jnp.abs(nl)))
    a = jnp.exp((-LRU_C) * r * softplus)
    mult = jnp.sqrt(1.0 - a * a)
    row = lax.broadcasted_iota(jnp.int32, a.shape, 0)
    mult = jnp.where((row < SUBLANES) & (ti == 0), 1.0, mult)
    a_ref[...] = a
    b_ref[...] = uc * ig * mult

    def step(t, h):
        r0 = pl.multiple_of(t * SUBLANES, SUBLANES)
        h = a_ref[pl.ds(r0, SUBLANES), :] * h + b_ref[pl.ds(r0, SUBLANES), :]
        b_ref[pl.ds(r0, SUBLANES), :] = h
        return h

    h_ref[...] = lax.fori_loop(0, tt, step, h_ref[...], unroll=8)

    g = g_ref[...].astype(F32)
    gelu = 0.5 * g * (1.0 + jnp.tanh(0.7978845608028654 * (g + 0.044715 * (g * g * g))))
    y_ref[...] = (b_ref[...] * gelu).astype(y_ref.dtype)


def _lru(z, conv_w, conv_b, wa, ba, wx, bx, lam, n, c, col_u, col_g):
    nblk_total, bw, _ = wa.shape
    ct = _pick(c, (512, 256, 128))
    tt = _pick(n // SUBLANES, (128, 64, 32, 16, 8))
    rows = tt * SUBLANES
    nblk = ct // bw
    vec = lambda: pl.BlockSpec((1, ct), lambda ci, ti: (0, ci))
    return pl.pallas_call(
        functools.partial(_lru_kernel, tt=tt, nblk=nblk, bw=bw),
        out_shape=jax.ShapeDtypeStruct((n, c), BF16),
        grid=(c // ct, n // rows),
        in_specs=[
            pl.BlockSpec((rows, ct), lambda ci, ti: (ti, col_u // ct + ci)),
            pl.BlockSpec((rows, ct), lambda ci, ti: (ti, col_g // ct + ci)),
            pl.BlockSpec((CONV_WIDTH, ct), lambda ci, ti: (0, ci)),
            vec(),
            pl.BlockSpec((nblk, bw, bw), lambda ci, ti: (ci, 0, 0)),
            vec(),
            pl.BlockSpec((nblk, bw, bw), lambda ci, ti: (ci, 0, 0)),
            vec(),
            vec(),
        ],
        out_specs=pl.BlockSpec((rows, ct), lambda ci, ti: (ti, ci)),
        scratch_shapes=[
            pltpu.VMEM((SUBLANES * (CONV_WIDTH - 1), ct), F32),
            pltpu.VMEM((SUBLANES, ct), F32),
            pltpu.VMEM((rows, ct), F32),
            pltpu.VMEM((rows, ct), F32),
        ],
        compiler_params=_params(("arbitrary", "arbitrary")),
        name="rg_lru",
    )(z, z, conv_w, conv_b, wa, ba, wx, bx, lam)


def _attn_kernel(q_ref, k_ref, v_ref, o_ref, st_ref, *, band, dil, slopes):
    n = pl.program_id(1)
    prev = jnp.maximum(n - 1, 0)
    cur0 = pl.multiple_of(n * band, band)
    prev0 = pl.multiple_of(prev * band, band)
    q = q_ref[...]
    kc = k_ref[pl.ds(cur0, band), :]
    kp = k_ref[pl.ds(prev0, band), :]
    vc = v_ref[pl.ds(cur0, band), :]
    vp = v_ref[pl.ds(prev0, band), :]

    qi = lax.broadcasted_iota(jnp.int32, (band, band), 0)
    kj = lax.broadcasted_iota(jnp.int32, (band, band), 1)
    dist_c = (qi - kj).astype(F32)
    dist_p = dist_c + float(band)
    valid_c = kj <= qi
    valid_p = (kj >= qi) & (n > 0)
    scale = HEAD_DIM ** -0.5
    nt = (((1,), (1,)), ((), ()))
    lane = lax.broadcasted_iota(jnp.int32, (band, LANES), 1)

    outs = []
    stat = jnp.zeros((band, LANES), F32)
    for h in range(HEADS_PER_GROUP):
        sl = slice(h * HEAD_DIM, (h + 1) * HEAD_DIM)
        bias = slopes[h] * dil
        sc = lax.dot_general(q[:, sl], kc[:, sl], nt, preferred_element_type=F32)
        sp = lax.dot_general(q[:, sl], kp[:, sl], nt, preferred_element_type=F32)
        sc = jnp.where(valid_c, sc * scale - bias * dist_c, -jnp.inf)
        sp = jnp.where(valid_p, sp * scale - bias * dist_p, -jnp.inf)
        m = jnp.maximum(jnp.max(sc, axis=1, keepdims=True), jnp.max(sp, axis=1, keepdims=True))
        pc = jnp.exp(sc - m)
        pp = jnp.exp(sp - m)
        l = jnp.sum(pc, axis=1, keepdims=True) + jnp.sum(pp, axis=1, keepdims=True)
        acc = (jnp.dot(pc.astype(BF16), vc[:, sl], preferred_element_type=F32)
               + jnp.dot(pp.astype(BF16), vp[:, sl], preferred_element_type=F32))
        outs.append(acc / l)
        stat = jnp.where(lane == h, m + jnp.log(l), stat)
    o_ref[...] = jnp.concatenate(outs, axis=1).astype(o_ref.dtype)
    st_ref[...] = stat


def _attn_group(qkv, g, n):
    window, dil = ATTN_GROUPS[g]
    band = window // dil
    stride = SUBLANES * dil
    lp = n // stride
    gw = HEADS_PER_GROUP * HEAD_DIM
    n_heads = HEADS_PER_GROUP * len(ATTN_GROUPS)
    slopes = tuple(2.0 ** (-8.0 * (g * HEADS_PER_GROUP + h + 1) / n_heads) for h in range(HEADS_PER_GROUP))
    o, st = pl.pallas_call(
        functools.partial(_attn_kernel, band=band, dil=float(dil), slopes=slopes),
        out_shape=(jax.ShapeDtypeStruct((lp, stride * gw), BF16),
                   jax.ShapeDtypeStruct((lp, stride * LANES), F32)),
        grid=(stride, lp // band),
        in_specs=[
            pl.BlockSpec((None, band, gw), lambda c, i: (c, i, 0)),
            pl.BlockSpec((None, lp, gw), lambda c, i: (c, 0, 1)),
            pl.BlockSpec((None, lp, gw), lambda c, i: (c, 0, 2)),
        ],
        out_specs=(pl.BlockSpec((band, gw), lambda c, i: (i, c)),
                   pl.BlockSpec((band, LANES), lambda c, i: (i, c))),
        compiler_params=_params(("arbitrary", "arbitrary")),
        name=f"dilated_attn_g{g}",
    )(qkv, qkv, qkv)
    return o.reshape(n, gw), st.reshape(n, LANES)


def _merge_kernel(x_ref, yr_ref, gr_ref, ga_ref, o0_ref, o1_ref, o2_ref, s0_ref, s1_ref, s2_ref,
                  wr_ref, wa_ref, wo_ref, g1_ref, b1_ref, x1_ref, x1p_ref, *, tt, d):
    t1 = jnp.dot(yr_ref[...], wr_ref[...], preferred_element_type=F32)
    merged = _sigmoid(gr_ref[...].astype(F32)) * t1

    stats = [s0_ref[...], s1_ref[...], s2_ref[...]]
    outs = [o0_ref, o1_ref, o2_ref]
    heads = []
    for h in range(HEADS_PER_GROUP):
        lse = [s[:, h:h + 1] for s in stats]
        mx = jnp.maximum(jnp.maximum(lse[0], lse[1]), lse[2])
        e = [jnp.exp(v - mx) for v in lse]
        tot = e[0] + e[1] + e[2]
        acc = None
        for gi in range(len(ATTN_GROUPS)):
            term = (e[gi] / tot) * outs[gi][:, h * HEAD_DIM:(h + 1) * HEAD_DIM].astype(F32)
            acc = term if acc is None else acc + term
        heads.append(acc)
    oa = jnp.concatenate(heads, axis=1).astype(BF16)
    t2 = jnp.dot(oa, wa_ref[...], preferred_element_type=F32)
    merged = merged + _sigmoid(ga_ref[...].astype(F32)) * t2
    mix = jnp.dot(merged.astype(BF16), wo_ref[...], preferred_element_type=F32)

    v = DN_ALPHA * _time_major_rows(x_ref, tt, d) + mix
    mu = jnp.mean(v, axis=-1, keepdims=True)
    cen = v - mu
    var = jnp.mean(cen * cen, axis=-1, keepdims=True)
    x1 = cen * lax.rsqrt(var + LN_EPS) * g1_ref[...] + b1_ref[...]
    x1_ref[...] = x1
    _store_packed(x1p_ref, 0, x1)


def _merge(xv, yr, z, os_, sts, wr, wa, wo, g1, b1, n, d, col_gr, col_ga):
    tm = _pick(n, (256, 128, 64))
    tt = tm // SUBLANES
    gw = HEADS_PER_GROUP * HEAD_DIM
    rpt = _rows_per_token(d)
    const = lambda shape: pl.BlockSpec(shape, lambda i: (0, 0), pipeline_mode=pl.Buffered(1))
    row = lambda w: pl.BlockSpec((tm, w), lambda i: (i, 0))
    return pl.pallas_call(
        functools.partial(_merge_kernel, tt=tt, d=d),
        out_shape=(jax.ShapeDtypeStruct((n, d), F32), jax.ShapeDtypeStruct((n * rpt, LANES), U32)),
        grid=(n // tm,),
        in_specs=[
            pl.BlockSpec((SUBLANES, tt * d), lambda i: (0, i)),
            row(yr.shape[1]),
            pl.BlockSpec((tm, d), lambda i: (i, col_gr // d)),
            pl.BlockSpec((tm, d), lambda i: (i, col_ga // d)),
            row(gw), row(gw), row(gw),
            row(LANES), row(LANES), row(LANES),
            const(wr.shape), const(wa.shape), const(wo.shape),
            const((1, d)), const((1, d)),
        ],
        out_specs=(row(d), pl.BlockSpec((tm * rpt, LANES), lambda i: (i, 0))),
        compiler_params=_params(("arbitrary",)),
        name="merge_ln1",
    )(xv, yr, z, z, *os_, *sts, wr, wa, wo, g1, b1)


def _router_kernel(x_ref, wr_ref, bias_ref, idx_ref, w_ref, rank_ref, cnt_ref, base_ref, tri_ref, *, t, e):
    i = pl.program_id(0)

    @pl.when(i == 0)
    def _():
        base_ref[...] = jnp.zeros_like(base_ref)
        rr = lax.broadcasted_iota(jnp.int32, (t, t), 0)
        cc = lax.broadcasted_iota(jnp.int32, (t, t), 1)
        tri_ref[...] = jnp.where(rr < cc, 1.0, 0.0).astype(BF16)

    nt = (((1,), (1,)), ((), ()))
    logits = lax.dot_general(wr_ref[...], x_ref[...].astype(BF16), nt, preferred_element_type=F32)
    scores = _sigmoid(logits)
    biased = scores + bias_ref[...]

    per = e // N_GROUP
    sub = lax.broadcasted_iota(jnp.int32, (per, t), 0).astype(F32)
    blocks, gscore = [], []
    for g in range(N_GROUP):
        blk = biased[g * per:(g + 1) * per, :]
        m1 = jnp.max(blk, axis=0, keepdims=True)
        first = jnp.min(jnp.where(blk == m1, sub, float(per)), axis=0, keepdims=True)
        m2 = jnp.max(jnp.where(sub == first, -jnp.inf, blk), axis=0, keepdims=True)
        blocks.append(blk)
        gscore.append(m1 + m2)
    masked_blocks = []
    for g in range(N_GROUP):
        beaten = jnp.zeros((1, t), F32)
        for g2 in range(N_GROUP):
            if g2 == g:
                continue
            wins = (gscore[g2] > gscore[g]) | ((gscore[g2] == gscore[g]) & (g2 < g))
            beaten = beaten + jnp.where(wins, 1.0, 0.0)
        keep = beaten < float(TOPK_GROUP)
        masked_blocks.append(jnp.where(keep, blocks[g], -jnp.inf))
    masked = jnp.concatenate(masked_blocks, axis=0)

    eidx = lax.broadcasted_iota(jnp.int32, (e, t), 0).astype(F32)
    member = jnp.zeros((e, t), F32)
    sels, ws = [], []
    for _ in range(TOP_K):
        m = jnp.max(masked, axis=0, keepdims=True)
        sel = jnp.min(jnp.where(masked == m, eidx, float(e)), axis=0, keepdims=True)
        hit = eidx == sel
        ws.append(jnp.sum(jnp.where(hit, scores, 0.0), axis=0, keepdims=True))
        masked = jnp.where(hit, -jnp.inf, masked)
        member = jnp.where(hit, 1.0, member)
        sels.append(sel)
    wsum = ws[0]
    for k in range(1, TOP_K):
        wsum = wsum + ws[k]

    before = jnp.dot(member.astype(BF16), tri_ref[...], preferred_element_type=F32) + base_ref[...]
    for k in range(TOP_K):
        idx_ref[k:k + 1, :] = sels[k].astype(jnp.int32)
        w_ref[k:k + 1, :] = ws[k] / wsum * ROUTED_SCALE
        rk = jnp.sum(jnp.where(eidx == sels[k], before, 0.0), axis=0, keepdims=True)
        rank_ref[k:k + 1, :] = rk.astype(jnp.int32)
    base_ref[...] = base_ref[...] + jnp.sum(member, axis=1, keepdims=True)
    cnt_ref[...] = jnp.broadcast_to(base_ref[...], cnt_ref.shape)


def _router(x1, wrt, bias, n, d):
    e = wrt.shape[0]
    t = _pick(n, (512, 256, 128))
    tok = lambda dt: jax.ShapeDtypeStruct((TOP_K, n), dt)
    return pl.pallas_call(
        functools.partial(_router_kernel, t=t, e=e),
        out_shape=(tok(jnp.int32), tok(F32), tok(jnp.int32), jax.ShapeDtypeStruct((e, LANES), F32)),
        grid=(n // t,),
        in_specs=[
            pl.BlockSpec((t, d), lambda i: (i, 0)),
            pl.BlockSpec((e, d), lambda i: (0, 0)),
            pl.BlockSpec((e, 1), lambda i: (0, 0)),
        ],
        out_specs=(pl.BlockSpec((TOP_K, t), lambda i: (0, i)),
                   pl.BlockSpec((TOP_K, t), lambda i: (0, i)),
                   pl.BlockSpec((TOP_K, t), lambda i: (0, i)),
                   pl.BlockSpec((e, LANES), lambda i: (0, 0))),
        scratch_shapes=[pltpu.VMEM((e, 1), F32), pltpu.VMEM((t, t), BF16)],
        compiler_params=_params(("arbitrary",)),
        name="router_topk",
    )(x1, wrt, bias)


def _dest_kernel(idx_ref, rank_ref, start_ref, dest_ref, *, t, e):
    eidx = lax.broadcasted_iota(jnp.int32, (e, t), 0)
    for k in range(TOP_K):
        hit = eidx == idx_ref[k:k + 1, :]
        st = jnp.sum(jnp.where(hit, start_ref[...], 0.0), axis=0, keepdims=True)
        dest_ref[0, k:k + 1, :] = st.astype(jnp.int32) + rank_ref[k:k + 1, :]


def _dest(idx, rank, start_col, n, t):
    e = start_col.shape[0]
    return pl.pallas_call(
        functools.partial(_dest_kernel, t=t, e=e),
        out_shape=jax.ShapeDtypeStruct((n // t, TOP_K, t), jnp.int32),
        grid=(n // t,),
        in_specs=[
            pl.BlockSpec((TOP_K, t), lambda i: (0, i)),
            pl.BlockSpec((TOP_K, t), lambda i: (0, i)),
            pl.BlockSpec((e, 1), lambda i: (0, 0)),
        ],
        out_specs=pl.BlockSpec((1, TOP_K, t), lambda i: (i, 0, 0)),
        compiler_params=_params(("arbitrary",)),
        name="moe_dest",
    )(idx, rank, start_col)


def _token_copy(src_ref, src_tok, dst_ref, dst_tok, sem, rpt):
    s0 = pl.multiple_of(src_tok * rpt, rpt)
    d0 = pl.multiple_of(dst_tok * rpt, rpt)
    return pltpu.make_async_copy(src_ref.at[pl.ds(s0, rpt), :], dst_ref.at[pl.ds(d0, rpt), :], sem)


TOKENS_PER_SLOT_ROW = LANES // TOP_K


def _for_each_token_slots(table, t, fn):
    def body(r, c):
        for pair in range(0, TOKENS_PER_SLOT_ROW, 2):
            slots = [[table[r, (pair + p) * TOP_K + k] for k in range(TOP_K)] for p in range(2)]
            for p in range(2):
                fn(r * TOKENS_PER_SLOT_ROW + pair + p, slots[p])
        return c

    lax.fori_loop(0, t // TOKENS_PER_SLOT_ROW, body, 0)


def _slot_table_copy(dest_hbm, step, dsm, slot, sem_idx):
    return pltpu.make_async_copy(dest_hbm.at[step], dsm.at[slot], sem_idx.at[slot])


def _dispatch_kernel(dest_hbm, x_hbm, xs_hbm, dsm, sem_idx, sem, *, t, rpt, n_steps):
    i = pl.program_id(0)
    slot = i % 2

    @pl.when(i == 0)
    def _():
        _slot_table_copy(dest_hbm, 0, dsm, 0, sem_idx).start()

    _slot_table_copy(dest_hbm, i, dsm, slot, sem_idx).wait()

    @pl.when(i + 1 < n_steps)
    def _():
        _slot_table_copy(dest_hbm, i + 1, dsm, 1 - slot, sem_idx).start()

    def issue(tl, slots):
        for k in range(TOP_K):
            _token_copy(x_hbm, i * t + tl, xs_hbm, slots[k], sem.at[slot], rpt).start()

    _for_each_token_slots(dsm.at[slot], t, issue)

    def drain(sl):
        for _ in range(TOP_K):
            pltpu.make_async_copy(x_hbm.at[pl.ds(0, t * rpt), :], xs_hbm.at[pl.ds(0, t * rpt), :],
                                  sem.at[sl]).wait()

    @pl.when(i > 0)
    def _():
        drain(1 - slot)

    @pl.when(i == n_steps - 1)
    def _():
        drain(slot)


def _dispatch(dest3, x1p, n, d, t):
    rpt = _rows_per_token(d)
    return pl.pallas_call(
        functools.partial(_dispatch_kernel, t=t, rpt=rpt, n_steps=n // t),
        out_shape=jax.ShapeDtypeStruct((n * TOP_K * rpt, LANES), U32),
        grid=(n // t,),
        in_specs=[pl.BlockSpec(memory_space=pl.ANY), pl.BlockSpec(memory_space=pl.ANY)],
        out_specs=pl.BlockSpec(memory_space=pl.ANY),
        scratch_shapes=[pltpu.SMEM((2, TOP_K * t // LANES, LANES), jnp.int32), pltpu.SemaphoreType.DMA((2,)),
                        pltpu.SemaphoreType.DMA((2,))],
        compiler_params=_params(("arbitrary",)),
        name="moe_dispatch",
    )(dest3, x1p)


def _expert_kernel(wb_ref, we_ref, nw_ref, st_ref, en_ref, xs_ref, wg_ref, wu_ref, wd_ref, o_ref,
                   wgb_ref, wub_ref, wdb_ref, *, tm, rpt):
    w = pl.program_id(0)

    @pl.when(w < nw_ref[0])
    def _():
        e = we_ref[w]
        b = wb_ref[w]
        pw = jnp.maximum(w - 1, 0)

        @pl.when((w == 0) | (e != we_ref[pw]))
        def _():
            wgb_ref[...] = wg_ref[...].astype(BF16)
            wub_ref[...] = wu_ref[...].astype(BF16)
            wdb_ref[...] = wd_ref[...].astype(BF16)

        x = _unpack_words([xs_ref[_token_rows(0, tm, rpt, j), :] for j in range(rpt)]).astype(BF16)
        g = jnp.dot(x, wgb_ref[...], preferred_element_type=F32)
        u = jnp.dot(x, wub_ref[...], preferred_element_type=F32)
        h = (g * _sigmoid(g)) * u
        o = jnp.dot(h.astype(BF16), wdb_ref[...], preferred_element_type=F32)
        new = _pack_words(o)

        rows = b * tm + lax.broadcasted_iota(jnp.int32, (tm, 1), 0)
        mine = (rows >= st_ref[e]) & (rows < en_ref[e])
        first = (w == 0) | (b != wb_ref[pw])

        @pl.when(first)
        def _():
            for j in range(rpt):
                o_ref[_token_rows(0, tm, rpt, j), :] = jnp.where(mine, new[j], U32(0))

        @pl.when(jnp.logical_not(first))
        def _():
            for j in range(rpt):
                rows_j = _token_rows(0, tm, rpt, j)
                o_ref[rows_j, :] = jnp.where(mine, new[j], o_ref[rows_j, :])


def _experts(wb, we, nw, start, end, xs, w_gate, w_up, w_down, tm):
    _, d, de = w_gate.shape
    rpt = _rows_per_token(d)
    n_work = wb.shape[0]
    return pl.pallas_call(
        functools.partial(_expert_kernel, tm=tm, rpt=rpt),
        out_shape=jax.ShapeDtypeStruct(xs.shape, U32),
        grid_spec=pltpu.PrefetchScalarGridSpec(
            num_scalar_prefetch=5,
            grid=(n_work,),
            in_specs=[
                pl.BlockSpec((tm * rpt, LANES), lambda w, wb, we, nw, st, en: (wb[w], 0)),
                pl.BlockSpec((None, d, de), lambda w, wb, we, nw, st, en: (we[w], 0, 0)),
                pl.BlockSpec((None, d, de), lambda w, wb, we, nw, st, en: (we[w], 0, 0)),
                pl.BlockSpec((None, de, d), lambda w, wb, we, nw, st, en: (we[w], 0, 0)),
            ],
            out_specs=pl.BlockSpec((tm * rpt, LANES), lambda w, wb, we, nw, st, en: (wb[w], 0)),
            scratch_shapes=[pltpu.VMEM((d, de), BF16), pltpu.VMEM((d, de), BF16), pltpu.VMEM((de, d), BF16)],
        ),
        compiler_params=_params(("arbitrary",)),
        name="moe_experts",
    )(wb, we, nw, start, end, xs, w_gate, w_up, w_down)


def _combine_kernel(dest_hbm, x1_ref, wt_ref, o_hbm, wsg_ref, wsu_ref, wsd_ref, g2_ref, b2_ref, y_ref,
                    dsm, gbuf, sem_idx, sem, *, t, tt, d, rpt, n_steps):
    i = pl.program_id(0)
    slot = i % 2

    def issue(step_slot):
        def gather(tl, slots):
            for k in range(TOP_K):
                _token_copy(o_hbm, slots[k], gbuf.at[step_slot], k * t + tl, sem.at[step_slot], rpt).start()

        _for_each_token_slots(dsm.at[step_slot], t, gather)

    @pl.when(i == 0)
    def _():
        first = _slot_table_copy(dest_hbm, 0, dsm, 0, sem_idx)
        first.start()
        first.wait()
        issue(0)
        if n_steps > 1:
            _slot_table_copy(dest_hbm, 1, dsm, 1, sem_idx).start()

    @pl.when(i + 1 < n_steps)
    def _():
        _slot_table_copy(dest_hbm, i + 1, dsm, 1 - slot, sem_idx).wait()
        issue(1 - slot)

        @pl.when(i + 2 < n_steps)
        def _():
            _slot_table_copy(dest_hbm, i + 2, dsm, slot, sem_idx).start()

    x1 = x1_ref[...]
    xb = x1.astype(BF16)
    g = jnp.dot(xb, wsg_ref[...], preferred_element_type=F32)
    u = jnp.dot(xb, wsu_ref[...], preferred_element_type=F32)
    h = (g * _sigmoid(g)) * u
    shared = jnp.dot(h.astype(BF16), wsd_ref[...], preferred_element_type=F32)

    gcur = gbuf.at[slot]
    pltpu.make_async_copy(o_hbm.at[pl.ds(0, TOP_K * t * rpt), :], gcur, sem.at[slot]).wait()

    acc_lo = [None] * rpt
    acc_hi = [None] * rpt
    for k in range(TOP_K):
        wk = wt_ref[:, k:k + 1]
        for j in range(rpt):
            words = gcur[_token_rows(k * t * rpt, t, rpt, j), :]
            lo = wk * pltpu.bitcast(words << 16, F32)
            hi = wk * pltpu.bitcast(words & U32(HI_MASK), F32)
            acc_lo[j] = lo if k == 0 else acc_lo[j] + lo
            acc_hi[j] = hi if k == 0 else acc_hi[j] + hi
    routed = jnp.concatenate(acc_lo + acc_hi, axis=1)

    v = DN_ALPHA * x1 + (routed + shared)
    mu = jnp.mean(v, axis=-1, keepdims=True)
    cen = v - mu
    var = jnp.mean(cen * cen, axis=-1, keepdims=True)
    y = cen * lax.rsqrt(var + LN_EPS) * g2_ref[...] + b2_ref[...]
    for s in range(tt):
        y_ref[:, s * d:(s + 1) * d] = y[s * SUBLANES:(s + 1) * SUBLANES, :]


def _combine(dest3, x1, wt, o, wsg, wsu, wsd, g2, b2, n, d, t):
    tt = t // SUBLANES
    rpt = _rows_per_token(d)
    const = lambda shape: pl.BlockSpec(shape, lambda i: (0, 0), pipeline_mode=pl.Buffered(1))
    return pl.pallas_call(
        functools.partial(_combine_kernel, t=t, tt=tt, d=d, rpt=rpt, n_steps=n // t),
        out_shape=jax.ShapeDtypeStruct((SUBLANES, (n // SUBLANES) * d), F32),
        grid=(n // t,),
        in_specs=[
            pl.BlockSpec(memory_space=pl.ANY),
            pl.BlockSpec((t, d), lambda i: (i, 0)),
            pl.BlockSpec((t, TOP_K), lambda i: (i, 0)),
            pl.BlockSpec(memory_space=pl.ANY),
            const(wsg.shape), const(wsu.shape), const(wsd.shape),
            const((1, d)), const((1, d)),
        ],
        out_specs=pl.BlockSpec((SUBLANES, tt * d), lambda i: (0, i)),
        scratch_shapes=[pltpu.SMEM((2, TOP_K * t // LANES, LANES), jnp.int32), pltpu.VMEM((2, TOP_K * t * rpt, LANES), U32),
                        pltpu.SemaphoreType.DMA((2,)), pltpu.SemaphoreType.DMA((2,))],
        compiler_params=_params(("arbitrary",)),
        name="moe_combine_ln2",
    )(dest3, x1, wt, o, wsg, wsu, wsd, g2, b2)


def _work_items(counts, m, tm):
    e = counts.shape[0]
    end = jnp.cumsum(counts)
    start = end - counts
    fb = start // tm
    lb = (end - 1) // tm
    nblk = jnp.where(counts > 0, lb - fb + 1, 0)
    cum = jnp.cumsum(nblk)
    off = cum - nblk
    n_work = m // tm + e - 1
    w = jnp.arange(n_work, dtype=jnp.int32)
    we = jnp.minimum(jnp.searchsorted(cum, w, side="right"), e - 1).astype(jnp.int32)
    wb = (fb[we] + (w - off[we])).astype(jnp.int32)
    nw = cum[-1].astype(jnp.int32)
    last = jnp.maximum(nw - 1, 0)
    valid = w < nw
    we = jnp.where(valid, we, we[last])
    wb = jnp.where(valid, wb, wb[last])
    return wb, we, nw.reshape(1), start.astype(jnp.int32), end.astype(jnp.int32)


def kernel(x, w_in, b_in, conv_w, conv_b, lru_wa, lru_ba, lru_wx, lru_bx, lru_lambda, w_rnn_br, w_attn_br, w_out, ln1_g, ln1_b, w_router, router_bias, w_gate, w_up, w_down, ws_gate, ws_up, ws_down, ln2_g, ln2_b):
    bsz, s, d = x.shape
    assert bsz == SUBLANES, "time-major rows need batch == 8"
    assert DEPTH == 1 and w_in.shape[0] == 1
    n = bsz * s
    c = lru_lambda.shape[1]
    a = HEADS_PER_GROUP * len(ATTN_GROUPS) * HEAD_DIM
    assert c == d and w_in.shape[2] == 2 * c + 3 * a + 2 * d
    layer = 0

    gw = HEADS_PER_GROUP * HEAD_DIM
    src = (0, c, 2 * c, 2 * c + a, 2 * c + 2 * a, 2 * c + 3 * a, 2 * c + 3 * a + d, 2 * c + 3 * a + 2 * d)
    part = lambda v, p, lo=0, hi=None: v[..., src[p]:src[p + 1]][..., lo:hi]
    cat = lambda v, pieces: jnp.concatenate([part(v, *p) for p in pieces], axis=-1)
    main = ((0,), (1,), (5,), (6,))
    w_main = cat(w_in[layer], main).astype(BF16)
    b_main = cat(b_in[layer], main).reshape(1, -1)
    col_u, col_g, col_gr, col_ga = 0, c, 2 * c, 2 * c + d
    row = lambda v: v.reshape(1, -1)

    xv = x.reshape(bsz, s * d)
    z, xb = _in_proj(xv, w_main, b_main, n, d)

    yr = _lru(z, conv_w[layer], row(conv_b[layer]), lru_wa[layer].astype(BF16), row(lru_ba[layer]),
              lru_wx[layer].astype(BF16), row(lru_bx[layer]), row(lru_lambda[layer]), n, c, col_u, col_g)

    os_, sts = [], []
    for g in range(len(ATTN_GROUPS)):
        cols = tuple((p, g * gw, (g + 1) * gw) for p in (2, 3, 4))
        qkv = _qkv_proj(xb, cat(w_in[layer], cols).astype(BF16), cat(b_in[layer], cols).reshape(1, -1), g, n, d)
        o, st = _attn_group(qkv, g, n)
        os_.append(o)
        sts.append(st)

    x1, x1p = _merge(xv, yr, z, os_, sts, w_rnn_br[layer].astype(BF16), w_attn_br[layer].astype(BF16),
                     w_out[layer].astype(BF16), row(ln1_g[layer]), row(ln1_b[layer]), n, d, col_gr, col_ga)

    n_exp = w_router.shape[2]
    idx, wts, rank, cnt = _router(x1, w_router[layer].T.astype(BF16), router_bias[layer].reshape(n_exp, 1), n, d)

    tm = 256
    t_tok = _pick(n, (256, 128))
    m = n * TOP_K
    counts = cnt[:, 0].astype(jnp.int32)
    wb, we, nw, start, end = _work_items(counts, m, tm)
    dest3 = _dest(idx, rank, start.astype(F32).reshape(n_exp, 1), n, t_tok)
    dest3 = dest3.transpose(0, 2, 1).reshape(n // t_tok, t_tok * TOP_K // LANES, LANES)

    xs = _dispatch(dest3, x1p, n, d, t_tok)
    o = _experts(wb, we, nw, start, end, xs, w_gate[layer], w_up[layer], w_down[layer], tm)
    y = _combine(dest3, x1, wts.T, o, ws_gate[layer].astype(BF16), ws_up[layer].astype(BF16),
                 ws_down[layer].astype(BF16), row(ln2_g[layer]), row(ln2_b[layer]), n, d, t_tok)
    return y.reshape(bsz, s, d)
```

```python
import functools

import jax
import jax.numpy as jnp
from jax import lax
from jax.experimental import pallas as pl
from jax.experimental.pallas import tpu as pltpu

HEAD_DIM = 128
ATTN_GROUPS = ((128, 1), (512, 4), (2048, 16))
HEADS_PER_GROUP = 4
CONV_WIDTH = 4
LRU_C = 8.0
N_GROUP = 8
TOPK_GROUP = 4
TOP_K = 8
ROUTED_SCALE = 2.5
DEPTH = 1
DN_ALPHA = (2 * DEPTH) ** 0.25
LN_EPS = 1e-5

SUBLANES = 8
LANES = 128
V7X_VMEM_LIMIT_BYTES = 56 * 1024 * 1024

F32 = jnp.float32
BF16 = jnp.bfloat16


def _pick(n, cands):
    for c in cands:
        if n % c == 0:
            return c
    raise ValueError(f"no tile in {cands} divides {n}")


def _sigmoid(x):
    return 1.0 / (1.0 + jnp.exp(-x))


def _params(sem, vmem=V7X_VMEM_LIMIT_BYTES):
    return pltpu.CompilerParams(dimension_semantics=sem, vmem_limit_bytes=vmem)


def _time_major_rows(x_ref, tt, d):
    return jnp.concatenate([x_ref[:, t * d:(t + 1) * d] for t in range(tt)], axis=0)


U32 = jnp.uint32
HI_MASK = 0xFFFF0000


def _rows_per_token(d):
    assert d % (2 * LANES) == 0
    return d // (2 * LANES)


def _token_rows(base, m, rpt, j):
    return pl.ds(base + j, m, stride=rpt)


def _pack_words(x):
    half = x.shape[1] // 2
    bits = pltpu.bitcast(x.astype(BF16).astype(F32), U32)
    words = (bits[:, :half] >> 16) | (bits[:, half:] & U32(HI_MASK))
    return [words[:, j * LANES:(j + 1) * LANES] for j in range(half // LANES)]


def _store_packed(ref, base, x):
    chunks = _pack_words(x)
    for j, chunk in enumerate(chunks):
        ref[_token_rows(base, x.shape[0], len(chunks), j), :] = chunk


def _unpack_words(chunks):
    lo = [pltpu.bitcast(c << 16, F32) for c in chunks]
    hi = [pltpu.bitcast(c & U32(HI_MASK), F32) for c in chunks]
    return jnp.concatenate(lo + hi, axis=1)


def _in_proj_kernel(x_ref, w_ref, b_ref, z_ref, xb_ref, *, tt, d):
    @pl.when(pl.program_id(1) == 0)
    def _():
        for t in range(0, tt, 2):
            rows = jnp.concatenate([x_ref[:, t * d:(t + 1) * d], x_ref[:, (t + 1) * d:(t + 2) * d]], axis=0)
            xb_ref[t * SUBLANES:(t + 2) * SUBLANES, :] = rows.astype(BF16)

    z = jnp.dot(xb_ref[...], w_ref[...], preferred_element_type=F32) + b_ref[...]
    z_ref[...] = z.astype(z_ref.dtype)


def _in_proj(xv, w, b, n, d):
    d_out = w.shape[1]
    tm = _pick(n, (1024, 512, 256, 128, 64, 32, 16))
    tn = _pick(d_out, (1024, 512, 256, 128))
    tt = tm // SUBLANES
    return pl.pallas_call(
        functools.partial(_in_proj_kernel, tt=tt, d=d),
        out_shape=(jax.ShapeDtypeStruct((n, d_out), BF16), jax.ShapeDtypeStruct((n, d), BF16)),
        grid=(n // tm, d_out // tn),
        in_specs=[
            pl.BlockSpec((SUBLANES, tt * d), lambda i, j: (0, i)),
            pl.BlockSpec((d, tn), lambda i, j: (0, j)),
            pl.BlockSpec((1, tn), lambda i, j: (0, j)),
        ],
        out_specs=(pl.BlockSpec((tm, tn), lambda i, j: (i, j)),
                   pl.BlockSpec((tm, d), lambda i, j: (i, 0))),
        compiler_params=_params(("arbitrary", "arbitrary")),
        name="in_proj",
    )(xv, w, b)


def _qkv_kernel(xb_ref, w_ref, b_ref, p_ref, o_ref):
    r = jnp.dot(xb_ref[...], w_ref[...], preferred_element_type=F32) + b_ref[...]
    rp = jnp.dot(p_ref[...], r.astype(BF16), preferred_element_type=F32)
    o_ref[...] = rp.astype(BF16).reshape(o_ref.shape)


def _qkv_proj(xb, w, b, g, n, d):
    _, dil = ATTN_GROUPS[g]
    stride = SUBLANES * dil
    gw = HEADS_PER_GROUP * HEAD_DIM
    tm = max(_pick(n, (1024, 512, 256)), 16 * stride)
    per = tm // stride
    new = jnp.arange(tm, dtype=jnp.int32)
    old = (new % per) * stride + new // per
    perm = (old[:, None] == jnp.arange(tm, dtype=jnp.int32)[None, :]).astype(BF16)
    return pl.pallas_call(
        _qkv_kernel,
        out_shape=jax.ShapeDtypeStruct((stride, n // stride, 3 * gw), BF16),
        grid=(n // tm, 3),
        in_specs=[
            pl.BlockSpec((tm, d), lambda i, j: (i, 0)),
            pl.BlockSpec((d, gw), lambda i, j: (0, j)),
            pl.BlockSpec((1, gw), lambda i, j: (0, j)),
            pl.BlockSpec((tm, tm), lambda i, j: (0, 0), pipeline_mode=pl.Buffered(1)),
        ],
        out_specs=pl.BlockSpec((stride, per, gw), lambda i, j: (0, i, j)),
        compiler_params=_params(("arbitrary", "arbitrary")),
        name=f"qkv_proj_g{g}",
    )(xb, w, b, perm)


def _lru_kernel(u_ref, g_ref, cw_ref, cb_ref, wa_ref, ba_ref, wx_ref, bx_ref, lam_ref, y_ref,
                carry_ref, h_ref, a_ref, b_ref, *, tt, nblk, bw):
    ti = pl.program_id(1)
    rows = tt * SUBLANES
    halo = SUBLANES * (CONV_WIDTH - 1)

    @pl.when(ti == 0)
    def _():
        carry_ref[...] = jnp.zeros_like(carry_ref)
        h_ref[...] = jnp.zeros_like(h_ref)

    u = u_ref[...].astype(F32)
    ext = jnp.concatenate([carry_ref[...], u], axis=0)
    uc = cb_ref[...] + cw_ref[0:1, :] * ext[0:rows, :]
    for j in range(1, CONV_WIDTH):
        uc = uc + cw_ref[j:j + 1, :] * ext[SUBLANES * j:SUBLANES * j + rows, :]
    carry_ref[...] = u[rows - halo:, :]

    ucb = uc.astype(BF16)
    r_parts, i_parts = [], []
    for kb in range(nblk):
        blk = ucb[:, kb * bw:(kb + 1) * bw]
        r_parts.append(jnp.dot(blk, wa_ref[kb], preferred_element_type=F32))
        i_parts.append(jnp.dot(blk, wx_ref[kb], preferred_element_type=F32))
    r = _sigmoid(jnp.concatenate(r_parts, axis=1) + ba_ref[...])
    ig = _sigmoid(jnp.concatenate(i_parts, axis=1) + bx_ref[...])

    nl = -lam_ref[...]
    softplus = jnp.maximum(nl, 0.0) + jnp.log(1.0 + jnp.exp(-jnp.abs(nl)))
    a = jnp.exp((-LRU_C) * r * softplus)
    mult = jnp.sqrt(1.0 - a * a)
    row = lax.broadcasted_iota(jnp.int32, a.shape, 0)
    mult = jnp.where((row < SUBLANES) & (ti == 0), 1.0, mult)
    a_ref[...] = a
    b_ref[...] = uc * ig * mult

    def step(t, h):
        r0 = pl.multiple_of(t * SUBLANES, SUBLANES)
        h = a_ref[pl.ds(r0, SUBLANES), :] * h + b_ref[pl.ds(r0, SUBLANES), :]
        b_ref[pl.ds(r0, SUBLANES), :] = h
        return h

    h_ref[...] = lax.fori_loop(0, tt, step, h_ref[...], unroll=8)

    g = g_ref[...].astype(F32)
    gelu = 0.5 * g * (1.0 + jnp.tanh(0.7978845608028654 * (g + 0.044715 * (g * g * g))))
    y_ref[...] = (b_ref[...] * gelu).astype(y_ref.dtype)


def _lru(z, conv_w, conv_b, wa, ba, wx, bx, lam, n, c, col_u, col_g):
    nblk_total, bw, _ = wa.shape
    ct = _pick(c, (512, 256, 128))
    tt = _pick(n // SUBLANES, (128, 64, 32, 16, 8))
    rows = tt * SUBLANES
    nblk = ct // bw
    vec = lambda: pl.BlockSpec((1, ct), lambda ci, ti: (0, ci))
    return pl.pallas_call(
        functools.partial(_lru_kernel, tt=tt, nblk=nblk, bw=bw),
        out_shape=jax.ShapeDtypeStruct((n, c), BF16),
        grid=(c // ct, n // rows),
        in_specs=[
            pl.BlockSpec((rows, ct), lambda ci, ti: (ti, col_u // ct + ci)),
            pl.BlockSpec((rows, ct), lambda ci, ti: (ti, col_g // ct + ci)),
            pl.BlockSpec((CONV_WIDTH, ct), lambda ci, ti: (0, ci)),
            vec(),
            pl.BlockSpec((nblk, bw, bw), lambda ci, ti: (ci, 0, 0)),
            vec(),
            pl.BlockSpec((nblk, bw, bw), lambda ci, ti: (ci, 0, 0)),
            vec(),
            vec(),
        ],
        out_specs=pl.BlockSpec((rows, ct), lambda ci, ti: (ti, ci)),
        scratch_shapes=[
            pltpu.VMEM((SUBLANES * (CONV_WIDTH - 1), ct), F32),
            pltpu.VMEM((SUBLANES, ct), F32),
            pltpu.VMEM((rows, ct), F32),
            pltpu.VMEM((rows, ct), F32),
        ],
        compiler_params=_params(("arbitrary", "arbitrary")),
        name="rg_lru",
    )(z, z, conv_w, conv_b, wa, ba, wx, bx, lam)


def _attn_kernel(q_ref, k_ref, v_ref, o_ref, st_ref, *, band, dil, slopes):
    n = pl.program_id(1)
    prev = jnp.maximum(n - 1, 0)
    cur0 = pl.multiple_of(n * band, band)
    prev0 = pl.multiple_of(prev * band, band)
    q = q_ref[...]
    kc = k_ref[pl.ds(cur0, band), :]
    kp = k_ref[pl.ds(prev0, band), :]
    vc = v_ref[pl.ds(cur0, band), :]
    vp = v_ref[pl.ds(prev0, band), :]

    qi = lax.broadcasted_iota(jnp.int32, (band, band), 0)
    kj = lax.broadcasted_iota(jnp.int32, (band, band), 1)
    dist_c = (qi - kj).astype(F32)
    dist_p = dist_c + float(band)
    valid_c = kj <= qi
    valid_p = (kj >= qi) & (n > 0)
    scale = HEAD_DIM ** -0.5
    nt = (((1,), (1,)), ((), ()))
    lane = lax.broadcasted_iota(jnp.int32, (band, LANES), 1)

    outs = []
    stat = jnp.zeros((band, LANES), F32)
    for h in range(HEADS_PER_GROUP):
        sl = slice(h * HEAD_DIM, (h + 1) * HEAD_DIM)
        bias = slopes[h] * dil
        sc = lax.dot_general(q[:, sl], kc[:, sl], nt, preferred_element_type=F32)
        sp = lax.dot_general(q[:, sl], kp[:, sl], nt, preferred_element_type=F32)
        sc = jnp.where(valid_c, sc * scale - bias * dist_c, -jnp.inf)
        sp = jnp.where(valid_p, sp * scale - bias * dist_p, -jnp.inf)
        m = jnp.maximum(jnp.max(sc, axis=1, keepdims=True), jnp.max(sp, axis=1, keepdims=True))
        pc = jnp.exp(sc - m)
        pp = jnp.exp(sp - m)
        l = jnp.sum(pc, axis=1, keepdims=True) + jnp.sum(pp, axis=1, keepdims=True)
        acc = (jnp.dot(pc.astype(BF16), vc[:, sl], preferred_element_type=F32)
               + jnp.dot(pp.astype(BF16), vp[:, sl], preferred_element_type=F32))
        outs.append(acc / l)
        stat = jnp.where(lane == h, m + jnp.log(l), stat)
    o_ref[...] = jnp.concatenate(outs, axis=1).astype(o_ref.dtype)
    st_ref[...] = stat


def _attn_group(qkv, g, n):
    window, dil = ATTN_GROUPS[g]
    band = window // dil
    stride = SUBLANES * dil
    lp = n // stride
    gw = HEADS_PER_GROUP * HEAD_DIM
    n_heads = HEADS_PER_GROUP * len(ATTN_GROUPS)
    slopes = tuple(2.0 ** (-8.0 * (g * HEADS_PER_GROUP + h + 1) / n_heads) for h in range(HEADS_PER_GROUP))
    o, st = pl.pallas_call(
        functools.partial(_attn_kernel, band=band, dil=float(dil), slopes=slopes),
        out_shape=(jax.ShapeDtypeStruct((lp, stride * gw), BF16),
                   jax.ShapeDtypeStruct((lp, stride * LANES), F32)),
        grid=(stride, lp // band),
        in_specs=[
            pl.BlockSpec((None, band, gw), lambda c, i: (c, i, 0)),
            pl.BlockSpec((None, lp, gw), lambda c, i: (c, 0, 1)),
            pl.BlockSpec((None, lp, gw), lambda c, i: (c, 0, 2)),
        ],
        out_specs=(pl.BlockSpec((band, gw), lambda c, i: (i, c)),
                   pl.BlockSpec((band, LANES), lambda c, i: (i, c))),
        compiler_params=_params(("arbitrary", "arbitrary")),
        name=f"dilated_attn_g{g}",
    )(qkv, qkv, qkv)
    return o.reshape(n, gw), st.reshape(n, LANES)


def _merge_kernel(x_ref, yr_ref, gr_ref, ga_ref, o0_ref, o1_ref, o2_ref, s0_ref, s1_ref, s2_ref,
                  wr_ref, wa_ref, wo_ref, g1_ref, b1_ref, x1_ref, x1p_ref, *, tt, d):
    t1 = jnp.dot(yr_ref[...], wr_ref[...], preferred_element_type=F32)
    merged = _sigmoid(gr_ref[...].astype(F32)) * t1

    stats = [s0_ref[...], s1_ref[...], s2_ref[...]]
    outs = [o0_ref, o1_ref, o2_ref]
    heads = []
    for h in range(HEADS_PER_GROUP):
        lse = [s[:, h:h + 1] for s in stats]
        mx = jnp.maximum(jnp.maximum(lse[0], lse[1]), lse[2])
        e = [jnp.exp(v - mx) for v in lse]
        tot = e[0] + e[1] + e[2]
        acc = None
        for gi in range(len(ATTN_GROUPS)):
            term = (e[gi] / tot) * outs[gi][:, h * HEAD_DIM:(h + 1) * HEAD_DIM].astype(F32)
            acc = term if acc is None else acc + term
        heads.append(acc)
    oa = jnp.concatenate(heads, axis=1).astype(BF16)
    t2 = jnp.dot(oa, wa_ref[...], preferred_element_type=F32)
    merged = merged + _sigmoid(ga_ref[...].astype(F32)) * t2
    mix = jnp.dot(merged.astype(BF16), wo_ref[...], preferred_element_type=F32)

    v = DN_ALPHA * _time_major_rows(x_ref, tt, d) + mix
    mu = jnp.mean(v, axis=-1, keepdims=True)
    cen = v - mu
    var = jnp.mean(cen * cen, axis=-1, keepdims=True)
    x1 = cen * lax.rsqrt(var + LN_EPS) * g1_ref[...] + b1_ref[...]
    x1_ref[...] = x1
    _store_packed(x1p_ref, 0, x1)


def _merge(xv, yr, z, os_, sts, wr, wa, wo, g1, b1, n, d, col_gr, col_ga):
    tm = _pick(n, (256, 128, 64))
    tt = tm // SUBLANES
    gw = HEADS_PER_GROUP * HEAD_DIM
    rpt = _rows_per_token(d)
    const = lambda shape: pl.BlockSpec(shape, lambda i: (0, 0), pipeline_mode=pl.Buffered(1))
    row = lambda w: pl.BlockSpec((tm, w), lambda i: (i, 0))
    return pl.pallas_call(
        functools.partial(_merge_kernel, tt=tt, d=d),
        out_shape=(jax.ShapeDtypeStruct((n, d), F32), jax.ShapeDtypeStruct((n * rpt, LANES), U32)),
        grid=(n // tm,),
        in_specs=[
            pl.BlockSpec((SUBLANES, tt * d), lambda i: (0, i)),
            row(yr.shape[1]),
            pl.BlockSpec((tm, d), lambda i: (i, col_gr // d)),
            pl.BlockSpec((tm, d), lambda i: (i, col_ga // d)),
            row(gw), row(gw), row(gw),
            row(LANES), row(LANES), row(LANES),
            const(wr.shape), const(wa.shape), const(wo.shape),
            const((1, d)), const((1, d)),
        ],
        out_specs=(row(d), pl.BlockSpec((tm * rpt, LANES), lambda i: (i, 0))),
        compiler_params=_params(("arbitrary",)),
        name="merge_ln1",
    )(xv, yr, z, z, *os_, *sts, wr, wa, wo, g1, b1)


def _router_kernel(x_ref, wr_ref, bias_ref, idx_ref, w_ref, rank_ref, cnt_ref, base_ref, tri_ref, *, t, e):
    i = pl.program_id(0)

    @pl.when(i == 0)
    def _():
        base_ref[...] = jnp.zeros_like(base_ref)
        rr = lax.broadcasted_iota(jnp.int32, (t, t), 0)
        cc = lax.broadcasted_iota(jnp.int32, (t, t), 1)
        tri_ref[...] = jnp.where(rr < cc, 1.0, 0.0).astype(BF16)

    nt = (((1,), (1,)), ((), ()))
    logits = lax.dot_general(wr_ref[...], x_ref[...].astype(BF16), nt, preferred_element_type=F32)
    scores = _sigmoid(logits)
    biased = scores + bias_ref[...]

    per = e // N_GROUP
    sub = lax.broadcasted_iota(jnp.int32, (per, t), 0).astype(F32)
    blocks, gscore = [], []
    for g in range(N_GROUP):
        blk = biased[g * per:(g + 1) * per, :]
        m1 = jnp.max(blk, axis=0, keepdims=True)
        first = jnp.min(jnp.where(blk == m1, sub, float(per)), axis=0, keepdims=True)
        m2 = jnp.max(jnp.where(sub == first, -jnp.inf, blk), axis=0, keepdims=True)
        blocks.append(blk)
        gscore.append(m1 + m2)
    masked_blocks = []
    for g in range(N_GROUP):
        beaten = jnp.zeros((1, t), F32)
        for g2 in range(N_GROUP):
            if g2 == g:
                continue
            wins = (gscore[g2] > gscore[g]) | ((gscore[g2] == gscore[g]) & (g2 < g))
            beaten = beaten + jnp.where(wins, 1.0, 0.0)
        keep = beaten < float(TOPK_GROUP)
        masked_blocks.append(jnp.where(keep, blocks[g], -jnp.inf))
    masked = jnp.concatenate(masked_blocks, axis=0)

    eidx = lax.broadcasted_iota(jnp.int32, (e, t), 0).astype(F32)
    member = jnp.zeros((e, t), F32)
    sels, ws = [], []
    for _ in range(TOP_K):
        m = jnp.max(masked, axis=0, keepdims=True)
        sel = jnp.min(jnp.where(masked == m, eidx, float(e)), axis=0, keepdims=True)
        hit = eidx == sel
        ws.append(jnp.sum(jnp.where(hit, scores, 0.0), axis=0, keepdims=True))
        masked = jnp.where(hit, -jnp.inf, masked)
        member = jnp.where(hit, 1.0, member)
        sels.append(sel)
    wsum = ws[0]
    for k in range(1, TOP_K):
        wsum = wsum + ws[k]

    before = jnp.dot(member.astype(BF16), tri_ref[...], preferred_element_type=F32) + base_ref[...]
    for k in range(TOP_K):
        idx_ref[k:k + 1, :] = sels[k].astype(jnp.int32)
        w_ref[k:k + 1, :] = ws[k] / wsum * ROUTED_SCALE
        rk = jnp.sum(jnp.where(eidx == sels[k], before, 0.0), axis=0, keepdims=True)
        rank_ref[k:k + 1, :] = rk.astype(jnp.int32)
    base_ref[...] = base_ref[...] + jnp.sum(member, axis=1, keepdims=True)
    cnt_ref[...] = jnp.broadcast_to(base_ref[...], cnt_ref.shape)


def _router(x1, wrt, bias, n, d):
    e = wrt.shape[0]
    t = _pick(n, (512, 256, 128))
    tok = lambda dt: jax.ShapeDtypeStruct((TOP_K, n), dt)
    return pl.pallas_call(
        functools.partial(_router_kernel, t=t, e=e),
        out_shape=(tok(jnp.int32), tok(F32), tok(jnp.int32), jax.ShapeDtypeStruct((e, LANES), F32)),
        grid=(n // t,),
        in_specs=[
            pl.BlockSpec((t, d), lambda i: (i, 0)),
            pl.BlockSpec((e, d), lambda i: (0, 0)),
            pl.BlockSpec((e, 1), lambda i: (0, 0)),
        ],
        out_specs=(pl.BlockSpec((TOP_K, t), lambda i: (0, i)),
                   pl.BlockSpec((TOP_K, t), lambda i: (0, i)),
                   pl.BlockSpec((TOP_K, t), lambda i: (0, i)),
                   pl.BlockSpec((e, LANES), lambda i: (0, 0))),
        scratch_shapes=[pltpu.VMEM((e, 1), F32), pltpu.VMEM((t, t), BF16)],
        compiler_params=_params(("arbitrary",)),
        name="router_topk",
    )(x1, wrt, bias)


def _dest_kernel(idx_ref, rank_ref, start_ref, dest_ref, *, t, e):
    eidx = lax.broadcasted_iota(jnp.int32, (e, t), 0)
    for k in range(TOP_K):
        hit = eidx == idx_ref[k:k + 1, :]
        st = jnp.sum(jnp.where(hit, start_ref[...], 0.0), axis=0, keepdims=True)
        dest_ref[0, k:k + 1, :] = st.astype(jnp.int32) + rank_ref[k:k + 1, :]


def _dest(idx, rank, start_col, n, t):
    e = start_col.shape[0]
    return pl.pallas_call(
        functools.partial(_dest_kernel, t=t, e=e),
        out_shape=jax.ShapeDtypeStruct((n // t, TOP_K, t), jnp.int32),
        grid=(n // t,),
        in_specs=[
            pl.BlockSpec((TOP_K, t), lambda i: (0, i)),
            pl.BlockSpec((TOP_K, t), lambda i: (0, i)),
            pl.BlockSpec((e, 1), lambda i: (0, 0)),
        ],
        out_specs=pl.BlockSpec((1, TOP_K, t), lambda i: (i, 0, 0)),
        compiler_params=_params(("arbitrary",)),
        name="moe_dest",
    )(idx, rank, start_col)


def _token_copy(src_ref, src_tok, dst_ref, dst_tok, sem, rpt):
    s0 = pl.multiple_of(src_tok * rpt, rpt)
    d0 = pl.multiple_of(dst_tok * rpt, rpt)
    return pltpu.make_async_copy(src_ref.at[pl.ds(s0, rpt), :], dst_ref.at[pl.ds(d0, rpt), :], sem)


TOKENS_PER_SLOT_ROW = LANES // TOP_K


def _for_each_token_slots(table, t, fn):
    def body(r, c):
        for pair in range(0, TOKENS_PER_SLOT_ROW, 2):
            slots = [[table[r, (pair + p) * TOP_K + k] for k in range(TOP_K)] for p in range(2)]
            for p in range(2):
                fn(r * TOKENS_PER_SLOT_ROW + pair + p, slots[p])
        return c

    lax.fori_loop(0, t // TOKENS_PER_SLOT_ROW, body, 0)


def _slot_table_copy(dest_hbm, step, dsm, slot, sem_idx):
    return pltpu.make_async_copy(dest_hbm.at[step], dsm.at[slot], sem_idx.at[slot])


def _dispatch_kernel(dest_hbm, x_ref, xs_hbm, dsm, sem_idx, sem, *, t, rpt, n_steps):
    i = pl.program_id(0)
    slot = i % 2

    @pl.when(i == 0)
    def _():
        _slot_table_copy(dest_hbm, 0, dsm, 0, sem_idx).start()

    _slot_table_copy(dest_hbm, i, dsm, slot, sem_idx).wait()

    @pl.when(i + 1 < n_steps)
    def _():
        _slot_table_copy(dest_hbm, i + 1, dsm, 1 - slot, sem_idx).start()

    def issue(tl, slots):
        for k in range(TOP_K):
            _token_copy(x_ref, tl, xs_hbm, slots[k], sem, rpt).start()

    _for_each_token_slots(dsm.at[slot], t, issue)

    for _ in range(TOP_K):
        pltpu.make_async_copy(x_ref, xs_hbm.at[pl.ds(0, t * rpt), :], sem).wait()


def _dispatch(dest3, x1p, n, d, t):
    rpt = _rows_per_token(d)
    return pl.pallas_call(
        functools.partial(_dispatch_kernel, t=t, rpt=rpt, n_steps=n // t),
        out_shape=jax.ShapeDtypeStruct((n * TOP_K * rpt, LANES), U32),
        grid=(n // t,),
        in_specs=[pl.BlockSpec(memory_space=pl.ANY), pl.BlockSpec((t * rpt, LANES), lambda i: (i, 0))],
        out_specs=pl.BlockSpec(memory_space=pl.ANY),
        scratch_shapes=[pltpu.SMEM((2, TOP_K * t // LANES, LANES), jnp.int32), pltpu.SemaphoreType.DMA((2,)),
                        pltpu.SemaphoreType.DMA],
        compiler_params=_params(("arbitrary",)),
        name="moe_dispatch",
    )(dest3, x1p)


def _expert_kernel(wb_ref, we_ref, nw_ref, st_ref, en_ref, nx_ref, par_ref, xs_ref, wg_hbm, wu_hbm, wd_hbm, o_ref,
                   wgf_ref, wuf_ref, wdf_ref, wgb_ref, wub_ref, wdb_ref, wsem, *, tm, rpt):
    w = pl.program_id(0)

    def weight_copies(expert, buf):
        return [pltpu.make_async_copy(src.at[expert], dst.at[buf], wsem.at[buf])
                for src, dst in ((wg_hbm, wgf_ref), (wu_hbm, wuf_ref), (wd_hbm, wdf_ref))]

    @pl.when(w < nw_ref[0])
    def _():
        e = we_ref[w]
        b = wb_ref[w]
        pw = jnp.maximum(w - 1, 0)

        @pl.when((w == 0) | (e != we_ref[pw]))
        def _():
            buf = par_ref[w]
            nxt = nx_ref[w]

            @pl.when(w == 0)
            def _():
                for c in weight_copies(e, buf):
                    c.start()

            @pl.when(nxt >= 0)
            def _():
                for c in weight_copies(nxt, 1 - buf):
                    c.start()

            for c in weight_copies(e, buf):
                c.wait()
            wgb_ref[...] = wgf_ref[buf].astype(BF16)
            wub_ref[...] = wuf_ref[buf].astype(BF16)
            wdb_ref[...] = wdf_ref[buf].astype(BF16)

        x = _unpack_words([xs_ref[_token_rows(0, tm, rpt, j), :] for j in range(rpt)]).astype(BF16)
        g = jnp.dot(x, wgb_ref[...], preferred_element_type=F32)
        u = jnp.dot(x, wub_ref[...], preferred_element_type=F32)
        h = (g * _sigmoid(g)) * u
        o = jnp.dot(h.astype(BF16), wdb_ref[...], preferred_element_type=F32)
        new = _pack_words(o)

        rows = b * tm + lax.broadcasted_iota(jnp.int32, (tm, 1), 0)
        mine = (rows >= st_ref[e]) & (rows < en_ref[e])
        first = (w == 0) | (b != wb_ref[pw])

        @pl.when(first)
        def _():
            for j in range(rpt):
                o_ref[_token_rows(0, tm, rpt, j), :] = jnp.where(mine, new[j], U32(0))

        @pl.when(jnp.logical_not(first))
        def _():
            for j in range(rpt):
                rows_j = _token_rows(0, tm, rpt, j)
                o_ref[rows_j, :] = jnp.where(mine, new[j], o_ref[rows_j, :])


def _experts(work, xs, w_gate, w_up, w_down, tm):
    _, d, de = w_gate.shape
    rpt = _rows_per_token(d)
    n_work = work[0].shape[0]
    rows = pl.BlockSpec((tm * rpt, LANES), lambda w, wb, *_: (wb[w], 0))
    hbm = pl.BlockSpec(memory_space=pl.ANY)
    return pl.pallas_call(
        functools.partial(_expert_kernel, tm=tm, rpt=rpt),
        out_shape=jax.ShapeDtypeStruct(xs.shape, U32),
        grid_spec=pltpu.PrefetchScalarGridSpec(
            num_scalar_prefetch=len(work),
            grid=(n_work,),
            in_specs=[rows, hbm, hbm, hbm],
            out_specs=rows,
            scratch_shapes=[pltpu.VMEM((2, d, de), F32), pltpu.VMEM((2, d, de), F32), pltpu.VMEM((2, de, d), F32),
                            pltpu.VMEM((d, de), BF16), pltpu.VMEM((d, de), BF16), pltpu.VMEM((de, d), BF16),
                            pltpu.SemaphoreType.DMA((2,))],
        ),
        compiler_params=_params(("arbitrary",)),
        name="moe_experts",
    )(*work, xs, w_gate, w_up, w_down)


def _combine_kernel(dest_hbm, x1_ref, wt_ref, o_hbm, wsg_ref, wsu_ref, wsd_ref, g2_ref, b2_ref, y_ref,
                    dsm, gbuf, sem_idx, sem, *, t, tt, d, rpt, n_steps):
    i = pl.program_id(0)
    slot = i % 2

    def issue(step_slot):
        def gather(tl, slots):
            for k in range(TOP_K):
                _token_copy(o_hbm, slots[k], gbuf.at[step_slot], k * t + tl, sem.at[step_slot], rpt).start()

        _for_each_token_slots(dsm.at[step_slot], t, gather)

    @pl.when(i == 0)
    def _():
        first = _slot_table_copy(dest_hbm, 0, dsm, 0, sem_idx)
        first.start()
        first.wait()
        issue(0)
        if n_steps > 1:
            _slot_table_copy(dest_hbm, 1, dsm, 1, sem_idx).start()

    @pl.when(i + 1 < n_steps)
    def _():
        _slot_table_copy(dest_hbm, i + 1, dsm, 1 - slot, sem_idx).wait()
        issue(1 - slot)

        @pl.when(i + 2 < n_steps)
        def _():
            _slot_table_copy(dest_hbm, i + 2, dsm, slot, sem_idx).start()

    x1 = x1_ref[...]
    xb = x1.astype(BF16)
    g = jnp.dot(xb, wsg_ref[...], preferred_element_type=F32)
    u = jnp.dot(xb, wsu_ref[...], preferred_element_type=F32)
    h = (g * _sigmoid(g)) * u
    shared = jnp.dot(h.astype(BF16), wsd_ref[...], preferred_element_type=F32)

    gcur = gbuf.at[slot]
    pltpu.make_async_copy(o_hbm.at[pl.ds(0, TOP_K * t * rpt), :], gcur, sem.at[slot]).wait()

    acc_lo = [None] * rpt
    acc_hi = [None] * rpt
    for k in range(TOP_K):
        wk = wt_ref[:, k:k + 1]
        for j in range(rpt):
            words = gcur[_token_rows(k * t * rpt, t, rpt, j), :]
            lo = wk * pltpu.bitcast(words << 16, F32)
            hi = wk * pltpu.bitcast(words & U32(HI_MASK), F32)
            acc_lo[j] = lo if k == 0 else acc_lo[j] + lo
            acc_hi[j] = hi if k == 0 else acc_hi[j] + hi
    routed = jnp.concatenate(acc_lo + acc_hi, axis=1)

    v = DN_ALPHA * x1 + (routed + shared)
    mu = jnp.mean(v, axis=-1, keepdims=True)
    cen = v - mu
    var = jnp.mean(cen * cen, axis=-1, keepdims=True)
    y = cen * lax.rsqrt(var + LN_EPS) * g2_ref[...] + b2_ref[...]
    for s in range(tt):
        y_ref[:, s * d:(s + 1) * d] = y[s * SUBLANES:(s + 1) * SUBLANES, :]


def _combine(dest3, x1, wt, o, wsg, wsu, wsd, g2, b2, n, d, t):
    tt = t // SUBLANES
    rpt = _rows_per_token(d)
    const = lambda shape: pl.BlockSpec(shape, lambda i: (0, 0), pipeline_mode=pl.Buffered(1))
    return pl.pallas_call(
        functools.partial(_combine_kernel, t=t, tt=tt, d=d, rpt=rpt, n_steps=n // t),
        out_shape=jax.ShapeDtypeStruct((SUBLANES, (n // SUBLANES) * d), F32),
        grid=(n // t,),
        in_specs=[
            pl.BlockSpec(memory_space=pl.ANY),
            pl.BlockSpec((t, d), lambda i: (i, 0)),
            pl.BlockSpec((t, TOP_K), lambda i: (i, 0)),
            pl.BlockSpec(memory_space=pl.ANY),
            const(wsg.shape), const(wsu.shape), const(wsd.shape),
            const((1, d)), const((1, d)),
        ],
        out_specs=pl.BlockSpec((SUBLANES, tt * d), lambda i: (0, i)),
        scratch_shapes=[pltpu.SMEM((2, TOP_K * t // LANES, LANES), jnp.int32), pltpu.VMEM((2, TOP_K * t * rpt, LANES), U32),
                        pltpu.SemaphoreType.DMA((2,)), pltpu.SemaphoreType.DMA((2,))],
        compiler_params=_params(("arbitrary",)),
        name="moe_combine_ln2",
    )(dest3, x1, wt, o, wsg, wsu, wsd, g2, b2)


def _work_items(counts, m, tm):
    e = counts.shape[0]
    end = jnp.cumsum(counts)
    start = end - counts
    fb = start // tm
    lb = (end - 1) // tm
    nblk = jnp.where(counts > 0, lb - fb + 1, 0)
    cum = jnp.cumsum(nblk)
    off = cum - nblk
    n_work = m // tm + e - 1
    w = jnp.arange(n_work, dtype=jnp.int32)
    we = jnp.minimum(jnp.searchsorted(cum, w, side="right"), e - 1).astype(jnp.int32)
    wb = (fb[we] + (w - off[we])).astype(jnp.int32)
    nw = cum[-1].astype(jnp.int32)
    last = jnp.maximum(nw - 1, 0)
    valid = w < nw
    we = jnp.where(valid, we, we[last])
    wb = jnp.where(valid, wb, wb[last])
    ids = jnp.arange(e, dtype=jnp.int32)
    later = lax.cummin(jnp.where(counts > 0, ids, e)[::-1])[::-1]
    nxt_of = jnp.concatenate([later[1:], jnp.full((1,), e, jnp.int32)])
    nxt_of = jnp.where(nxt_of >= e, -1, nxt_of).astype(jnp.int32)
    ordinal = jnp.cumsum((counts > 0).astype(jnp.int32)) - 1
    return (wb, we, nw.reshape(1), start.astype(jnp.int32), end.astype(jnp.int32),
            nxt_of[we], (ordinal[we] % 2).astype(jnp.int32))


def kernel(x, w_in, b_in, conv_w, conv_b, lru_wa, lru_ba, lru_wx, lru_bx, lru_lambda, w_rnn_br, w_attn_br, w_out, ln1_g, ln1_b, w_router, router_bias, w_gate, w_up, w_down, ws_gate, ws_up, ws_down, ln2_g, ln2_b):
    bsz, s, d = x.shape
    assert bsz == SUBLANES, "time-major rows need batch == 8"
    assert DEPTH == 1 and w_in.shape[0] == 1
    n = bsz * s
    c = lru_lambda.shape[1]
    a = HEADS_PER_GROUP * len(ATTN_GROUPS) * HEAD_DIM
    assert c == d and w_in.shape[2] == 2 * c + 3 * a + 2 * d
    layer = 0

    gw = HEADS_PER_GROUP * HEAD_DIM
    src = (0, c, 2 * c, 2 * c + a, 2 * c + 2 * a, 2 * c + 3 * a, 2 * c + 3 * a + d, 2 * c + 3 * a + 2 * d)
    part = lambda v, p, lo=0, hi=None: v[..., src[p]:src[p + 1]][..., lo:hi]
    cat = lambda v, pieces: jnp.concatenate([part(v, *p) for p in pieces], axis=-1)
    main = ((0,), (1,), (5,), (6,))
    w_main = cat(w_in[layer], main).astype(BF16)
    b_main = cat(b_in[layer], main).reshape(1, -1)
    col_u, col_g, col_gr, col_ga = 0, c, 2 * c, 2 * c + d
    row = lambda v: v.reshape(1, -1)

    xv = x.reshape(bsz, s * d)
    z, xb = _in_proj(xv, w_main, b_main, n, d)

    yr = _lru(z, conv_w[layer], row(conv_b[layer]), lru_wa[layer].astype(BF16), row(lru_ba[layer]),
              lru_wx[layer].astype(BF16), row(lru_bx[layer]), row(lru_lambda[layer]), n, c, col_u, col_g)

    os_, sts = [], []
    for g in range(len(ATTN_GROUPS)):
        cols = tuple((p, g * gw, (g + 1) * gw) for p in (2, 3, 4))
        qkv = _qkv_proj(xb, cat(w_in[layer], cols).astype(BF16), cat(b_in[layer], cols).reshape(1, -1), g, n, d)
        o, st = _attn_group(qkv, g, n)
        os_.append(o)
        sts.append(st)

    x1, x1p = _merge(xv, yr, z, os_, sts, w_rnn_br[layer].astype(BF16), w_attn_br[layer].astype(BF16),
                     w_out[layer].astype(BF16), row(ln1_g[layer]), row(ln1_b[layer]), n, d, col_gr, col_ga)

    n_exp = w_router.shape[2]
    idx, wts, rank, cnt = _router(x1, w_router[layer].T.astype(BF16), router_bias[layer].reshape(n_exp, 1), n, d)

    tm = 256
    t_tok = _pick(n, (256, 128))
    m = n * TOP_K
    counts = cnt[:, 0].astype(jnp.int32)
    work = _work_items(counts, m, tm)
    start = work[3]
    dest3 = _dest(idx, rank, start.astype(F32).reshape(n_exp, 1), n, t_tok)
    dest3 = dest3.transpose(0, 2, 1).reshape(n // t_tok, t_tok * TOP_K // LANES, LANES)

    xs = _dispatch(dest3, x1p, n, d, t_tok)
    o = _experts(work, xs, w_gate[layer], w_up[layer], w_down[layer], tm)
    y = _combine(dest3, x1, wts.T, o, ws_gate[layer].astype(BF16), ws_up[layer].astype(BF16),
                 ws_down[layer].astype(BF16), row(ln2_g[layer]), row(ln2_b[layer]), n, d, t_tok)
    return y.reshape(bsz, s, d)
```

```python
import functools

import jax
import jax.numpy as jnp
from jax import lax
from jax.experimental import pallas as pl
from jax.experimental.pallas import tpu as pltpu

HEAD_DIM = 128
ATTN_GROUPS = ((128, 1), (512, 4), (2048, 16))
HEADS_PER_GROUP = 4
CONV_WIDTH = 4
LRU_C = 8.0
N_GROUP = 8
TOPK_GROUP = 4
TOP_K = 8
ROUTED_SCALE = 2.5
DEPTH = 1
DN_ALPHA = (2 * DEPTH) ** 0.25
LN_EPS = 1e-5

SUBLANES = 8
LANES = 128
V7X_VMEM_LIMIT_BYTES = 56 * 1024 * 1024

F32 = jnp.float32
BF16 = jnp.bfloat16


def _pick(n, cands):
    for c in cands:
        if n % c == 0:
            return c
    raise ValueError(f"no tile in {cands} divides {n}")


def _sigmoid(x):
    return 1.0 / (1.0 + jnp.exp(-x))


def _params(sem, vmem=V7X_VMEM_LIMIT_BYTES):
    return pltpu.CompilerParams(dimension_semantics=sem, vmem_limit_bytes=vmem)


def _time_major_rows(x_ref, tt, d):
    return jnp.concatenate([x_ref[:, t * d:(t + 1) * d] for t in range(tt)], axis=0)


U32 = jnp.uint32
HI_MASK = 0xFFFF0000


def _rows_per_token(d):
    assert d % (2 * LANES) == 0
    return d // (2 * LANES)


def _token_rows(base, m, rpt, j):
    return pl.ds(base + j, m, stride=rpt)


def _pack_words(x):
    half = x.shape[1] // 2
    bits = pltpu.bitcast(x.astype(BF16).astype(F32), U32)
    words = (bits[:, :half] >> 16) | (bits[:, half:] & U32(HI_MASK))
    return [words[:, j * LANES:(j + 1) * LANES] for j in range(half // LANES)]


def _store_packed(ref, base, x):
    chunks = _pack_words(x)
    for j, chunk in enumerate(chunks):
        ref[_token_rows(base, x.shape[0], len(chunks), j), :] = chunk


def _unpack_words(chunks):
    lo = [pltpu.bitcast(c << 16, F32) for c in chunks]
    hi = [pltpu.bitcast(c & U32(HI_MASK), F32) for c in chunks]
    return jnp.concatenate(lo + hi, axis=1)


def _in_proj_kernel(x_ref, w_ref, b_ref, z_ref, xb_ref, *, tt, d):
    @pl.when(pl.program_id(1) == 0)
    def _():
        for t in range(0, tt, 2):
            rows = jnp.concatenate([x_ref[:, t * d:(t + 1) * d], x_ref[:, (t + 1) * d:(t + 2) * d]], axis=0)
            xb_ref[t * SUBLANES:(t + 2) * SUBLANES, :] = rows.astype(BF16)

    z = jnp.dot(xb_ref[...], w_ref[...], preferred_element_type=F32) + b_ref[...]
    z_ref[...] = z.astype(z_ref.dtype)


def _in_proj(xv, w, b, n, d):
    d_out = w.shape[1]
    tm = _pick(n, (1024, 512, 256, 128, 64, 32, 16))
    tn = _pick(d_out, (1024, 512, 256, 128))
    tt = tm // SUBLANES
    return pl.pallas_call(
        functools.partial(_in_proj_kernel, tt=tt, d=d),
        out_shape=(jax.ShapeDtypeStruct((n, d_out), BF16), jax.ShapeDtypeStruct((n, d), BF16)),
        grid=(n // tm, d_out // tn),
        in_specs=[
            pl.BlockSpec((SUBLANES, tt * d), lambda i, j: (0, i)),
            pl.BlockSpec((d, tn), lambda i, j: (0, j)),
            pl.BlockSpec((1, tn), lambda i, j: (0, j)),
        ],
        out_specs=(pl.BlockSpec((tm, tn), lambda i, j: (i, j)),
                   pl.BlockSpec((tm, d), lambda i, j: (i, 0))),
        compiler_params=_params(("arbitrary", "arbitrary")),
        name="in_proj",
    )(xv, w, b)


def _qkv_kernel(xb_ref, w_ref, b_ref, p_ref, o_ref):
    r = jnp.dot(xb_ref[...], w_ref[...], preferred_element_type=F32) + b_ref[...]
    rp = jnp.dot(p_ref[...], r.astype(BF16), preferred_element_type=F32)
    o_ref[...] = rp.astype(BF16).reshape(o_ref.shape)


def _qkv_proj(xb, w, b, g, n, d):
    _, dil = ATTN_GROUPS[g]
    stride = SUBLANES * dil
    gw = HEADS_PER_GROUP * HEAD_DIM
    tm = max(_pick(n, (1024, 512, 256)), 16 * stride)
    per = tm // stride
    new = jnp.arange(tm, dtype=jnp.int32)
    old = (new % per) * stride + new // per
    perm = (old[:, None] == jnp.arange(tm, dtype=jnp.int32)[None, :]).astype(BF16)
    return pl.pallas_call(
        _qkv_kernel,
        out_shape=jax.ShapeDtypeStruct((stride, n // stride, 3 * gw), BF16),
        grid=(n // tm, 3),
        in_specs=[
            pl.BlockSpec((tm, d), lambda i, j: (i, 0)),
            pl.BlockSpec((d, gw), lambda i, j: (0, j)),
            pl.BlockSpec((1, gw), lambda i, j: (0, j)),
            pl.BlockSpec((tm, tm), lambda i, j: (0, 0), pipeline_mode=pl.Buffered(1)),
        ],
        out_specs=pl.BlockSpec((stride, per, gw), lambda i, j: (0, i, j)),
        compiler_params=_params(("arbitrary", "arbitrary")),
        name=f"qkv_proj_g{g}",
    )(xb, w, b, perm)


def _lru_kernel(u_ref, g_ref, cw_ref, cb_ref, wa_ref, ba_ref, wx_ref, bx_ref, lam_ref, y_ref,
                carry_ref, h_ref, a_ref, b_ref, *, tt, nblk, bw):
    ti = pl.program_id(1)
    rows = tt * SUBLANES
    halo = SUBLANES * (CONV_WIDTH - 1)

    @pl.when(ti == 0)
    def _():
        carry_ref[...] = jnp.zeros_like(carry_ref)
        h_ref[...] = jnp.zeros_like(h_ref)

    u = u_ref[...].astype(F32)
    ext = jnp.concatenate([carry_ref[...], u], axis=0)
    uc = cb_ref[...] + cw_ref[0:1, :] * ext[0:rows, :]
    for j in range(1, CONV_WIDTH):
        uc = uc + cw_ref[j:j + 1, :] * ext[SUBLANES * j:SUBLANES * j + rows, :]
    carry_ref[...] = u[rows - halo:, :]

    ucb = uc.astype(BF16)
    r_parts, i_parts = [], []
    for kb in range(nblk):
        blk = ucb[:, kb * bw:(kb + 1) * bw]
        r_parts.append(jnp.dot(blk, wa_ref[kb], preferred_element_type=F32))
        i_parts.append(jnp.dot(blk, wx_ref[kb], preferred_element_type=F32))
    r = _sigmoid(jnp.concatenate(r_parts, axis=1) + ba_ref[...])
    ig = _sigmoid(jnp.concatenate(i_parts, axis=1) + bx_ref[...])

    nl = -lam_ref[...]
    softplus = jnp.maximum(nl, 0.0) + jnp.log(1.0 + jnp.exp(-jnp.abs(nl)))
    a = jnp.exp((-LRU_C) * r * softplus)
    mult = jnp.sqrt(1.0 - a * a)
    row = lax.broadcasted_iota(jnp.int32, a.shape, 0)
    mult = jnp.where((row < SUBLANES) & (ti == 0), 1.0, mult)
    a_ref[...] = a
    b_ref[...] = uc * ig * mult

    def step(t, h):
        r0 = pl.multiple_of(t * SUBLANES, SUBLANES)
        h = a_ref[pl.ds(r0, SUBLANES), :] * h + b_ref[pl.ds(r0, SUBLANES), :]
        b_ref[pl.ds(r0, SUBLANES), :] = h
        return h

    h_ref[...] = lax.fori_loop(0, tt, step, h_ref[...], unroll=8)

    g = g_ref[...].astype(F32)
    gelu = 0.5 * g * (1.0 + jnp.tanh(0.7978845608028654 * (g + 0.044715 * (g * g * g))))
    y_ref[...] = (b_ref[...] * gelu).astype(y_ref.dtype)


def _lru(z, conv_w, conv_b, wa, ba, wx, bx, lam, n, c, col_u, col_g):
    nblk_total, bw, _ = wa.shape
    ct = _pick(c, (512, 256, 128))
    tt = _pick(n // SUBLANES, (128, 64, 32, 16, 8))
    rows = tt * SUBLANES
    nblk = ct // bw
    vec = lambda: pl.BlockSpec((1, ct), lambda ci, ti: (0, ci))
    return pl.pallas_call(
        functools.partial(_lru_kernel, tt=tt, nblk=nblk, bw=bw),
        out_shape=jax.ShapeDtypeStruct((n, c), BF16),
        grid=(c // ct, n // rows),
        in_specs=[
            pl.BlockSpec((rows, ct), lambda ci, ti: (ti, col_u // ct + ci)),
            pl.BlockSpec((rows, ct), lambda ci, ti: (ti, col_g // ct + ci)),
            pl.BlockSpec((CONV_WIDTH, ct), lambda ci, ti: (0, ci)),
            vec(),
            pl.BlockSpec((nblk, bw, bw), lambda ci, ti: (ci, 0, 0)),
            vec(),
            pl.BlockSpec((nblk, bw, bw), lambda ci, ti: (ci, 0, 0)),
            vec(),
            vec(),
        ],
        out_specs=pl.BlockSpec((rows, ct), lambda ci, ti: (ti, ci)),
        scratch_shapes=[
            pltpu.VMEM((SUBLANES * (CONV_WIDTH - 1), ct), F32),
            pltpu.VMEM((SUBLANES, ct), F32),
            pltpu.VMEM((rows, ct), F32),
            pltpu.VMEM((rows, ct), F32),
        ],
        compiler_params=_params(("arbitrary", "arbitrary")),
        name="rg_lru",
    )(z, z, conv_w, conv_b, wa, ba, wx, bx, lam)


def _attn_kernel(q_ref, k_ref, v_ref, o_ref, st_ref, *, band, dil, slopes, qblocks):
    qi = lax.broadcasted_iota(jnp.int32, (band, band), 0)
    kj = lax.broadcasted_iota(jnp.int32, (band, band), 1)
    dist_c = (qi - kj).astype(F32)
    dist_p = dist_c + float(band)
    valid_c = kj <= qi
    in_band_p = kj >= qi
    scale = HEAD_DIM ** -0.5
    nt = (((1,), (1,)), ((), ()))
    lane = lax.broadcasted_iota(jnp.int32, (band, LANES), 1)

    for qq in range(qblocks):
        n = pl.program_id(1) * qblocks + qq
        prev = jnp.maximum(n - 1, 0)
        cur0 = pl.multiple_of(n * band, band)
        prev0 = pl.multiple_of(prev * band, band)
        rows = slice(qq * band, (qq + 1) * band)
        q = q_ref[rows, :]
        kc = k_ref[pl.ds(cur0, band), :]
        kp = k_ref[pl.ds(prev0, band), :]
        vc = v_ref[pl.ds(cur0, band), :]
        vp = v_ref[pl.ds(prev0, band), :]
        valid_p = in_band_p & (n > 0)

        outs = []
        stat = jnp.zeros((band, LANES), F32)
        for h in range(HEADS_PER_GROUP):
            sl = slice(h * HEAD_DIM, (h + 1) * HEAD_DIM)
            bias = slopes[h] * dil
            sc = lax.dot_general(q[:, sl], kc[:, sl], nt, preferred_element_type=F32)
            sp = lax.dot_general(q[:, sl], kp[:, sl], nt, preferred_element_type=F32)
            sc = jnp.where(valid_c, sc * scale - bias * dist_c, -jnp.inf)
            sp = jnp.where(valid_p, sp * scale - bias * dist_p, -jnp.inf)
            m = jnp.maximum(jnp.max(sc, axis=1, keepdims=True), jnp.max(sp, axis=1, keepdims=True))
            pc = jnp.exp(sc - m)
            pp = jnp.exp(sp - m)
            l = jnp.sum(pc, axis=1, keepdims=True) + jnp.sum(pp, axis=1, keepdims=True)
            acc = (jnp.dot(pc.astype(BF16), vc[:, sl], preferred_element_type=F32)
                   + jnp.dot(pp.astype(BF16), vp[:, sl], preferred_element_type=F32))
            outs.append(acc / l)
            stat = jnp.where(lane == h, m + jnp.log(l), stat)
        o_ref[rows, :] = jnp.concatenate(outs, axis=1).astype(o_ref.dtype)
        st_ref[rows, :] = stat


def _attn_group(qkv, g, n):
    window, dil = ATTN_GROUPS[g]
    band = window // dil
    stride = SUBLANES * dil
    lp = n // stride
    gw = HEADS_PER_GROUP * HEAD_DIM
    n_heads = HEADS_PER_GROUP * len(ATTN_GROUPS)
    slopes = tuple(2.0 ** (-8.0 * (g * HEADS_PER_GROUP + h + 1) / n_heads) for h in range(HEADS_PER_GROUP))
    qblocks = 2 if (lp // band) % 2 == 0 else 1
    tq = qblocks * band
    o, st = pl.pallas_call(
        functools.partial(_attn_kernel, band=band, dil=float(dil), slopes=slopes, qblocks=qblocks),
        out_shape=(jax.ShapeDtypeStruct((lp, stride * gw), BF16),
                   jax.ShapeDtypeStruct((lp, stride * LANES), F32)),
        grid=(stride, lp // tq),
        in_specs=[
            pl.BlockSpec((None, tq, gw), lambda c, i: (c, i, 0)),
            pl.BlockSpec((None, lp, gw), lambda c, i: (c, 0, 1)),
            pl.BlockSpec((None, lp, gw), lambda c, i: (c, 0, 2)),
        ],
        out_specs=(pl.BlockSpec((tq, gw), lambda c, i: (i, c)),
                   pl.BlockSpec((tq, LANES), lambda c, i: (i, c))),
        compiler_params=_params(("arbitrary", "arbitrary")),
        name=f"dilated_attn_g{g}",
    )(qkv, qkv, qkv)
    return o.reshape(n, gw), st.reshape(n, LANES)


def _merge_kernel(x_ref, yr_ref, gr_ref, ga_ref, o0_ref, o1_ref, o2_ref, s0_ref, s1_ref, s2_ref,
                  wr_ref, wa_ref, wo_ref, g1_ref, b1_ref, x1_ref, x1p_ref, *, tt, d):
    t1 = jnp.dot(yr_ref[...], wr_ref[...], preferred_element_type=F32)
    merged = _sigmoid(gr_ref[...].astype(F32)) * t1

    stats = [s0_ref[...], s1_ref[...], s2_ref[...]]
    outs = [o0_ref, o1_ref, o2_ref]
    heads = []
    for h in range(HEADS_PER_GROUP):
        lse = [s[:, h:h + 1] for s in stats]
        mx = jnp.maximum(jnp.maximum(lse[0], lse[1]), lse[2])
        e = [jnp.exp(v - mx) for v in lse]
        tot = e[0] + e[1] + e[2]
        acc = None
        for gi in range(len(ATTN_GROUPS)):
            term = (e[gi] / tot) * outs[gi][:, h * HEAD_DIM:(h + 1) * HEAD_DIM].astype(F32)
            acc = term if acc is None else acc + term
        heads.append(acc)
    oa = jnp.concatenate(heads, axis=1).astype(BF16)
    t2 = jnp.dot(oa, wa_ref[...], preferred_element_type=F32)
    merged = merged + _sigmoid(ga_ref[...].astype(F32)) * t2
    mix = jnp.dot(merged.astype(BF16), wo_ref[...], preferred_element_type=F32)

    v = DN_ALPHA * _time_major_rows(x_ref, tt, d) + mix
    mu = jnp.mean(v, axis=-1, keepdims=True)
    cen = v - mu
    var = jnp.mean(cen * cen, axis=-1, keepdims=True)
    x1 = cen * lax.rsqrt(var + LN_EPS) * g1_ref[...] + b1_ref[...]
    x1_ref[...] = x1
    _store_packed(x1p_ref, 0, x1)


def _merge(xv, yr, z, os_, sts, wr, wa, wo, g1, b1, n, d, col_gr, col_ga):
    tm = _pick(n, (256, 128, 64))
    tt = tm // SUBLANES
    gw = HEADS_PER_GROUP * HEAD_DIM
    rpt = _rows_per_token(d)
    const = lambda shape: pl.BlockSpec(shape, lambda i: (0, 0), pipeline_mode=pl.Buffered(1))
    row = lambda w: pl.BlockSpec((tm, w), lambda i: (i, 0))
    return pl.pallas_call(
        functools.partial(_merge_kernel, tt=tt, d=d),
        out_shape=(jax.ShapeDtypeStruct((n, d), F32), jax.ShapeDtypeStruct((n * rpt, LANES), U32)),
        grid=(n // tm,),
        in_specs=[
            pl.BlockSpec((SUBLANES, tt * d), lambda i: (0, i)),
            row(yr.shape[1]),
            pl.BlockSpec((tm, d), lambda i: (i, col_gr // d)),
            pl.BlockSpec((tm, d), lambda i: (i, col_ga // d)),
            row(gw), row(gw), row(gw),
            row(LANES), row(LANES), row(LANES),
            const(wr.shape), const(wa.shape), const(wo.shape),
            const((1, d)), const((1, d)),
        ],
        out_specs=(row(d), pl.BlockSpec((tm * rpt, LANES), lambda i: (i, 0))),
        compiler_params=_params(("arbitrary",)),
        name="merge_ln1",
    )(xv, yr, z, z, *os_, *sts, wr, wa, wo, g1, b1)


def _router_kernel(x_ref, wr_ref, bias_ref, idx_ref, w_ref, rank_ref, cnt_ref, base_ref, tri_ref, *, t, e):
    i = pl.program_id(0)

    @pl.when(i == 0)
    def _():
        base_ref[...] = jnp.zeros_like(base_ref)
        rr = lax.broadcasted_iota(jnp.int32, (t, t), 0)
        cc = lax.broadcasted_iota(jnp.int32, (t, t), 1)
        tri_ref[...] = jnp.where(rr < cc, 1.0, 0.0).astype(BF16)

    nt = (((1,), (1,)), ((), ()))
    logits = lax.dot_general(wr_ref[...], x_ref[...].astype(BF16), nt, preferred_element_type=F32)
    scores = _sigmoid(logits)
    biased = scores + bias_ref[...]

    per = e // N_GROUP
    sub = lax.broadcasted_iota(jnp.int32, (per, t), 0).astype(F32)
    blocks, gscore = [], []
    for g in range(N_GROUP):
        blk = biased[g * per:(g + 1) * per, :]
        m1 = jnp.max(blk, axis=0, keepdims=True)
        first = jnp.min(jnp.where(blk == m1, sub, float(per)), axis=0, keepdims=True)
        m2 = jnp.max(jnp.where(sub == first, -jnp.inf, blk), axis=0, keepdims=True)
        blocks.append(blk)
        gscore.append(m1 + m2)
    masked_blocks = []
    for g in range(N_GROUP):
        beaten = jnp.zeros((1, t), F32)
        for g2 in range(N_GROUP):
            if g2 == g:
                continue
            wins = (gscore[g2] > gscore[g]) | ((gscore[g2] == gscore[g]) & (g2 < g))
            beaten = beaten + jnp.where(wins, 1.0, 0.0)
        keep = beaten < float(TOPK_GROUP)
        masked_blocks.append(jnp.where(keep, blocks[g], -jnp.inf))
    masked = jnp.concatenate(masked_blocks, axis=0)

    eidx = lax.broadcasted_iota(jnp.int32, (e, t), 0).astype(F32)
    member = jnp.zeros((e, t), F32)
    sels, ws = [], []
    for _ in range(TOP_K):
        m = jnp.max(masked, axis=0, keepdims=True)
        sel = jnp.min(jnp.where(masked == m, eidx, float(e)), axis=0, keepdims=True)
        hit = eidx == sel
        ws.append(jnp.sum(jnp.where(hit, scores, 0.0), axis=0, keepdims=True))
        masked = jnp.where(hit, -jnp.inf, masked)
        member = jnp.where(hit, 1.0, member)
        sels.append(sel)
    wsum = ws[0]
    for k in range(1, TOP_K):
        wsum = wsum + ws[k]

    before = jnp.dot(member.astype(BF16), tri_ref[...], preferred_element_type=F32) + base_ref[...]
    for k in range(TOP_K):
        idx_ref[k:k + 1, :] = sels[k].astype(jnp.int32)
        w_ref[k:k + 1, :] = ws[k] / wsum * ROUTED_SCALE
        rk = jnp.sum(jnp.where(eidx == sels[k], before, 0.0), axis=0, keepdims=True)
        rank_ref[k:k + 1, :] = rk.astype(jnp.int32)
    base_ref[...] = base_ref[...] + jnp.sum(member, axis=1, keepdims=True)
    cnt_ref[...] = jnp.broadcast_to(base_ref[...], cnt_ref.shape)


def _router(x1, wrt, bias, n, d):
    e = wrt.shape[0]
    t = _pick(n, (512, 256, 128))
    tok = lambda dt: jax.ShapeDtypeStruct((TOP_K, n), dt)
    return pl.pallas_call(
        functools.partial(_router_kernel, t=t, e=e),
        out_shape=(tok(jnp.int32), tok(F32), tok(jnp.int32), jax.ShapeDtypeStruct((e, LANES), F32)),
        grid=(n // t,),
        in_specs=[
            pl.BlockSpec((t, d), lambda i: (i, 0)),
            pl.BlockSpec((e, d), lambda i: (0, 0)),
            pl.BlockSpec((e, 1), lambda i: (0, 0)),
        ],
        out_specs=(pl.BlockSpec((TOP_K, t), lambda i: (0, i)),
                   pl.BlockSpec((TOP_K, t), lambda i: (0, i)),
                   pl.BlockSpec((TOP_K, t), lambda i: (0, i)),
                   pl.BlockSpec((e, LANES), lambda i: (0, 0))),
        scratch_shapes=[pltpu.VMEM((e, 1), F32), pltpu.VMEM((t, t), BF16)],
        compiler_params=_params(("arbitrary",)),
        name="router_topk",
    )(x1, wrt, bias)


def _dest_kernel(idx_ref, rank_ref, start_ref, dest_ref, *, t, e):
    eidx = lax.broadcasted_iota(jnp.int32, (e, t), 0)
    for k in range(TOP_K):
        hit = eidx == idx_ref[k:k + 1, :]
        st = jnp.sum(jnp.where(hit, start_ref[...], 0.0), axis=0, keepdims=True)
        dest_ref[0, k:k + 1, :] = st.astype(jnp.int32) + rank_ref[k:k + 1, :]


def _dest(idx, rank, start_col, n, t):
    e = start_col.shape[0]
    return pl.pallas_call(
        functools.partial(_dest_kernel, t=t, e=e),
        out_shape=jax.ShapeDtypeStruct((n // t, TOP_K, t), jnp.int32),
        grid=(n // t,),
        in_specs=[
            pl.BlockSpec((TOP_K, t), lambda i: (0, i)),
            pl.BlockSpec((TOP_K, t), lambda i: (0, i)),
            pl.BlockSpec((e, 1), lambda i: (0, 0)),
        ],
        out_specs=pl.BlockSpec((1, TOP_K, t), lambda i: (i, 0, 0)),
        compiler_params=_params(("arbitrary",)),
        name="moe_dest",
    )(idx, rank, start_col)


def _token_copy(src_ref, src_tok, dst_ref, dst_tok, sem, rpt):
    s0 = pl.multiple_of(src_tok * rpt, rpt)
    d0 = pl.multiple_of(dst_tok * rpt, rpt)
    return pltpu.make_async_copy(src_ref.at[pl.ds(s0, rpt), :], dst_ref.at[pl.ds(d0, rpt), :], sem)


TOKENS_PER_SLOT_ROW = LANES // TOP_K


def _for_each_token_slots(table, t, fn):
    def body(r, c):
        for pair in range(0, TOKENS_PER_SLOT_ROW, 2):
            slots = [[table[r, (pair + p) * TOP_K + k] for k in range(TOP_K)] for p in range(2)]
            for p in range(2):
                fn(r * TOKENS_PER_SLOT_ROW + pair + p, slots[p])
        return c

    lax.fori_loop(0, t // TOKENS_PER_SLOT_ROW, body, 0)


def _slot_table_copy(dest_hbm, step, dsm, slot, sem_idx):
    return pltpu.make_async_copy(dest_hbm.at[step], dsm.at[slot], sem_idx.at[slot])


def _dispatch_kernel(dest_hbm, x_ref, xs_hbm, dsm, sem_idx, sem, *, t, rpt, n_steps):
    i = pl.program_id(0)
    slot = i % 2

    @pl.when(i == 0)
    def _():
        _slot_table_copy(dest_hbm, 0, dsm, 0, sem_idx).start()

    _slot_table_copy(dest_hbm, i, dsm, slot, sem_idx).wait()

    @pl.when(i + 1 < n_steps)
    def _():
        _slot_table_copy(dest_hbm, i + 1, dsm, 1 - slot, sem_idx).start()

    def issue(tl, slots):
        for k in range(TOP_K):
            _token_copy(x_ref, tl, xs_hbm, slots[k], sem, rpt).start(priority=k % 2)

    _for_each_token_slots(dsm.at[slot], t, issue)

    for _ in range(TOP_K):
        pltpu.make_async_copy(x_ref, xs_hbm.at[pl.ds(0, t * rpt), :], sem).wait()


def _dispatch(dest3, x1p, n, d, t):
    rpt = _rows_per_token(d)
    return pl.pallas_call(
        functools.partial(_dispatch_kernel, t=t, rpt=rpt, n_steps=n // t),
        out_shape=jax.ShapeDtypeStruct((n * TOP_K * rpt, LANES), U32),
        grid=(n // t,),
        in_specs=[pl.BlockSpec(memory_space=pl.ANY), pl.BlockSpec((t * rpt, LANES), lambda i: (i, 0))],
        out_specs=pl.BlockSpec(memory_space=pl.ANY),
        scratch_shapes=[pltpu.SMEM((2, TOP_K * t // LANES, LANES), jnp.int32), pltpu.SemaphoreType.DMA((2,)),
                        pltpu.SemaphoreType.DMA],
        compiler_params=_params(("arbitrary",)),
        name="moe_dispatch",
    )(dest3, x1p)


def _expert_kernel(wb_ref, we_ref, nw_ref, st_ref, en_ref, nx_ref, par_ref, xs_ref, wg_hbm, wu_hbm, wd_hbm, o_ref,
                   wgf_ref, wuf_ref, wdf_ref, wgb_ref, wub_ref, wdb_ref, wsem, *, tm, rpt):
    w = pl.program_id(0)

    def weight_copies(expert, buf):
        return [pltpu.make_async_copy(src.at[expert], dst.at[buf], wsem.at[buf])
                for src, dst in ((wg_hbm, wgf_ref), (wu_hbm, wuf_ref), (wd_hbm, wdf_ref))]

    @pl.when(w < nw_ref[0])
    def _():
        e = we_ref[w]
        b = wb_ref[w]
        pw = jnp.maximum(w - 1, 0)

        @pl.when((w == 0) | (e != we_ref[pw]))
        def _():
            buf = par_ref[w]
            nxt = nx_ref[w]

            @pl.when(w == 0)
            def _():
                for c in weight_copies(e, buf):
                    c.start()

            @pl.when(nxt >= 0)
            def _():
                for c in weight_copies(nxt, 1 - buf):
                    c.start()

            for c in weight_copies(e, buf):
                c.wait()
            wgb_ref[...] = wgf_ref[buf].astype(BF16)
            wub_ref[...] = wuf_ref[buf].astype(BF16)
            wdb_ref[...] = wdf_ref[buf].astype(BF16)

        x = _unpack_words([xs_ref[_token_rows(0, tm, rpt, j), :] for j in range(rpt)]).astype(BF16)
        g = jnp.dot(x, wgb_ref[...], preferred_element_type=F32)
        u = jnp.dot(x, wub_ref[...], preferred_element_type=F32)
        h = (g * _sigmoid(g)) * u
        o = jnp.dot(h.astype(BF16), wdb_ref[...], preferred_element_type=F32)
        new = _pack_words(o)

        rows = b * tm + lax.broadcasted_iota(jnp.int32, (tm, 1), 0)
        mine = (rows >= st_ref[e]) & (rows < en_ref[e])
        first = (w == 0) | (b != wb_ref[pw])

        @pl.when(first)
        def _():
            for j in range(rpt):
                o_ref[_token_rows(0, tm, rpt, j), :] = jnp.where(mine, new[j], U32(0))

        @pl.when(jnp.logical_not(first))
        def _():
            for j in range(rpt):
                rows_j = _token_rows(0, tm, rpt, j)
                o_ref[rows_j, :] = jnp.where(mine, new[j], o_ref[rows_j, :])


def _experts(work, xs, w_gate, w_up, w_down, tm):
    _, d, de = w_gate.shape
    rpt = _rows_per_token(d)
    n_work = work[0].shape[0]
    rows = pl.BlockSpec((tm * rpt, LANES), lambda w, wb, *_: (wb[w], 0))
    hbm = pl.BlockSpec(memory_space=pl.ANY)
    return pl.pallas_call(
        functools.partial(_expert_kernel, tm=tm, rpt=rpt),
        out_shape=jax.ShapeDtypeStruct(xs.shape, U32),
        grid_spec=pltpu.PrefetchScalarGridSpec(
            num_scalar_prefetch=len(work),
            grid=(n_work,),
            in_specs=[rows, hbm, hbm, hbm],
            out_specs=rows,
            scratch_shapes=[pltpu.VMEM((2, d, de), F32), pltpu.VMEM((2, d, de), F32), pltpu.VMEM((2, de, d), F32),
                            pltpu.VMEM((d, de), BF16), pltpu.VMEM((d, de), BF16), pltpu.VMEM((de, d), BF16),
                            pltpu.SemaphoreType.DMA((2,))],
        ),
        compiler_params=_params(("arbitrary",)),
        name="moe_experts",
    )(*work, xs, w_gate, w_up, w_down)


def _combine_kernel(dest_hbm, x1_ref, wt_ref, o_hbm, wsg_ref, wsu_ref, wsd_ref, g2_ref, b2_ref, y_ref,
                    dsm, gbuf, sem_idx, sem, *, t, tt, d, rpt, n_steps):
    i = pl.program_id(0)
    slot = i % 2

    def issue(step_slot):
        def gather(tl, slots):
            for k in range(TOP_K):
                _token_copy(o_hbm, slots[k], gbuf.at[step_slot], k * t + tl, sem.at[step_slot],
                            rpt).start(priority=k % 2)

        _for_each_token_slots(dsm.at[step_slot], t, gather)

    @pl.when(i == 0)
    def _():
        first = _slot_table_copy(dest_hbm, 0, dsm, 0, sem_idx)
        first.start()
        first.wait()
        issue(0)
        if n_steps > 1:
            _slot_table_copy(dest_hbm, 1, dsm, 1, sem_idx).start()

    @pl.when(i + 1 < n_steps)
    def _():
        _slot_table_copy(dest_hbm, i + 1, dsm, 1 - slot, sem_idx).wait()
        issue(1 - slot)

        @pl.when(i + 2 < n_steps)
        def _():
            _slot_table_copy(dest_hbm, i + 2, dsm, slot, sem_idx).start()

    x1 = x1_ref[...]
    xb = x1.astype(BF16)
    g = jnp.dot(xb, wsg_ref[...], preferred_element_type=F32)
    u = jnp.dot(xb, wsu_ref[...], preferred_element_type=F32)
    h = (g * _sigmoid(g)) * u
    shared = jnp.dot(h.astype(BF16), wsd_ref[...], preferred_element_type=F32)

    gcur = gbuf.at[slot]
    pltpu.make_async_copy(o_hbm.at[pl.ds(0, TOP_K * t * rpt), :], gcur, sem.at[slot]).wait()

    acc_lo = [None] * rpt
    acc_hi = [None] * rpt
    for k in range(TOP_K):
        wk = wt_ref[:, k:k + 1]
        for j in range(rpt):
            words = gcur[_token_rows(k * t * rpt, t, rpt, j), :]
            lo = wk * pltpu.bitcast(words << 16, F32)
            hi = wk * pltpu.bitcast(words & U32(HI_MASK), F32)
            acc_lo[j] = lo if k == 0 else acc_lo[j] + lo
            acc_hi[j] = hi if k == 0 else acc_hi[j] + hi
    routed = jnp.concatenate(acc_lo + acc_hi, axis=1)

    v = DN_ALPHA * x1 + (routed + shared)
    mu = jnp.mean(v, axis=-1, keepdims=True)
    cen = v - mu
    var = jnp.mean(cen * cen, axis=-1, keepdims=True)
    y = cen * lax.rsqrt(var + LN_EPS) * g2_ref[...] + b2_ref[...]
    for s in range(tt):
        y_ref[:, s * d:(s + 1) * d] = y[s * SUBLANES:(s + 1) * SUBLANES, :]


def _combine(dest3, x1, wt, o, wsg, wsu, wsd, g2, b2, n, d, t):
    tt = t // SUBLANES
    rpt = _rows_per_token(d)
    const = lambda shape: pl.BlockSpec(shape, lambda i: (0, 0), pipeline_mode=pl.Buffered(1))
    return pl.pallas_call(
        functools.partial(_combine_kernel, t=t, tt=tt, d=d, rpt=rpt, n_steps=n // t),
        out_shape=jax.ShapeDtypeStruct((SUBLANES, (n // SUBLANES) * d), F32),
        grid=(n // t,),
        in_specs=[
            pl.BlockSpec(memory_space=pl.ANY),
            pl.BlockSpec((t, d), lambda i: (i, 0)),
            pl.BlockSpec((t, TOP_K), lambda i: (i, 0)),
            pl.BlockSpec(memory_space=pl.ANY),
            const(wsg.shape), const(wsu.shape), const(wsd.shape),
            const((1, d)), const((1, d)),
        ],
        out_specs=pl.BlockSpec((SUBLANES, tt * d), lambda i: (0, i)),
        scratch_shapes=[pltpu.SMEM((2, TOP_K * t // LANES, LANES), jnp.int32), pltpu.VMEM((2, TOP_K * t * rpt, LANES), U32),
                        pltpu.SemaphoreType.DMA((2,)), pltpu.SemaphoreType.DMA((2,))],
        compiler_params=_params(("arbitrary",)),
        name="moe_combine_ln2",
    )(dest3, x1, wt, o, wsg, wsu, wsd, g2, b2)


def _work_items(counts, m, tm):
    e = counts.shape[0]
    end = jnp.cumsum(counts)
    start = end - counts
    fb = start // tm
    lb = (end - 1) // tm
    nblk = jnp.where(counts > 0, lb - fb + 1, 0)
    cum = jnp.cumsum(nblk)
    off = cum - nblk
    n_work = m // tm + e - 1
    w = jnp.arange(n_work, dtype=jnp.int32)
    we = jnp.minimum(jnp.sum(cum[None, :] <= w[:, None], axis=1), e - 1).astype(jnp.int32)
    wb = (fb[we] + (w - off[we])).astype(jnp.int32)
    nw = cum[-1].astype(jnp.int32)
    last = jnp.maximum(nw - 1, 0)
    valid = w < nw
    we = jnp.where(valid, we, we[last])
    wb = jnp.where(valid, wb, wb[last])
    ids = jnp.arange(e, dtype=jnp.int32)
    later = lax.cummin(jnp.where(counts > 0, ids, e)[::-1])[::-1]
    nxt_of = jnp.concatenate([later[1:], jnp.full((1,), e, jnp.int32)])
    nxt_of = jnp.where(nxt_of >= e, -1, nxt_of).astype(jnp.int32)
    ordinal = jnp.cumsum((counts > 0).astype(jnp.int32)) - 1
    return (wb, we, nw.reshape(1), start.astype(jnp.int32), end.astype(jnp.int32),
            nxt_of[we], (ordinal[we] % 2).astype(jnp.int32))


def kernel(x, w_in, b_in, conv_w, conv_b, lru_wa, lru_ba, lru_wx, lru_bx, lru_lambda, w_rnn_br, w_attn_br, w_out, ln1_g, ln1_b, w_router, router_bias, w_gate, w_up, w_down, ws_gate, ws_up, ws_down, ln2_g, ln2_b):
    bsz, s, d = x.shape
    assert bsz == SUBLANES, "time-major rows need batch == 8"
    assert DEPTH == 1 and w_in.shape[0] == 1
    n = bsz * s
    c = lru_lambda.shape[1]
    a = HEADS_PER_GROUP * len(ATTN_GROUPS) * HEAD_DIM
    assert c == d and w_in.shape[2] == 2 * c + 3 * a + 2 * d
    layer = 0

    gw = HEADS_PER_GROUP * HEAD_DIM
    src = (0, c, 2 * c, 2 * c + a, 2 * c + 2 * a, 2 * c + 3 * a, 2 * c + 3 * a + d, 2 * c + 3 * a + 2 * d)
    part = lambda v, p, lo=0, hi=None: v[..., src[p]:src[p + 1]][..., lo:hi]
    cat = lambda v, pieces: jnp.concatenate([part(v, *p) for p in pieces], axis=-1)
    main = ((0,), (1,), (5,), (6,))
    w_main = cat(w_in[layer], main).astype(BF16)
    b_main = cat(b_in[layer], main).reshape(1, -1)
    col_u, col_g, col_gr, col_ga = 0, c, 2 * c, 2 * c + d
    row = lambda v: v.reshape(1, -1)

    xv = x.reshape(bsz, s * d)
    z, xb = _in_proj(xv, w_main, b_main, n, d)

    yr = _lru(z, conv_w[layer], row(conv_b[layer]), lru_wa[layer].astype(BF16), row(lru_ba[layer]),
              lru_wx[layer].astype(BF16), row(lru_bx[layer]), row(lru_lambda[layer]), n, c, col_u, col_g)

    os_, sts = [], []
    for g in range(len(ATTN_GROUPS)):
        cols = tuple((p, g * gw, (g + 1) * gw) for p in (2, 3, 4))
        qkv = _qkv_proj(xb, cat(w_in[layer], cols).astype(BF16), cat(b_in[layer], cols).reshape(1, -1), g, n, d)
        o, st = _attn_group(qkv, g, n)
        os_.append(o)
        sts.append(st)

    x1, x1p = _merge(xv, yr, z, os_, sts, w_rnn_br[layer].astype(BF16), w_attn_br[layer].astype(BF16),
                     w_out[layer].astype(BF16), row(ln1_g[layer]), row(ln1_b[layer]), n, d, col_gr, col_ga)

    n_exp = w_router.shape[2]
    idx, wts, rank, cnt = _router(x1, w_router[layer].T.astype(BF16), router_bias[layer].reshape(n_exp, 1), n, d)

    tm = 256
    t_tok = _pick(n, (256, 128))
    m = n * TOP_K
    counts = cnt[:, 0].astype(jnp.int32)
    work = _work_items(counts, m, tm)
    start = work[3]
    dest3 = _dest(idx, rank, start.astype(F32).reshape(n_exp, 1), n, t_tok)
    dest3 = dest3.transpose(0, 2, 1).reshape(n // t_tok, t_tok * TOP_K // LANES, LANES)

    xs = _dispatch(dest3, x1p, n, d, t_tok)
    o = _experts(work, xs, w_gate[layer], w_up[layer], w_down[layer], tm)
    y = _combine(dest3, x1, wts.T, o, ws_gate[layer].astype(BF16), ws_up[layer].astype(BF16),
                 ws_down[layer].astype(BF16), row(ln2_g[layer]), row(ln2_b[layer]), n, d, t_tok)
    return y.reshape(bsz, s, d)
```

```python
import functools

import jax
import jax.numpy as jnp
from jax import lax
from jax.experimental import pallas as pl
from jax.experimental.pallas import tpu as pltpu

HEAD_DIM = 128
ATTN_GROUPS = ((128, 1), (512, 4), (2048, 16))
HEADS_PER_GROUP = 4
CONV_WIDTH = 4
LRU_C = 8.0
N_GROUP = 8
TOPK_GROUP = 4
TOP_K = 8
ROUTED_SCALE = 2.5
DEPTH = 1
DN_ALPHA = (2 * DEPTH) ** 0.25
LN_EPS = 1e-5

SUBLANES = 8
LANES = 128
V7X_VMEM_LIMIT_BYTES = 56 * 1024 * 1024

F32 = jnp.float32
BF16 = jnp.bfloat16


def _pick(n, cands):
    for c in cands:
        if n % c == 0:
            return c
    raise ValueError(f"no tile in {cands} divides {n}")


def _sigmoid(x):
    return 1.0 / (1.0 + jnp.exp(-x))


def _params(sem, vmem=V7X_VMEM_LIMIT_BYTES):
    return pltpu.CompilerParams(dimension_semantics=sem, vmem_limit_bytes=vmem)


def _time_major_rows(x_ref, tt, d):
    return jnp.concatenate([x_ref[:, t * d:(t + 1) * d] for t in range(tt)], axis=0)


U32 = jnp.uint32
HI_MASK = 0xFFFF0000


def _rows_per_token(d):
    assert d % (2 * LANES) == 0
    return d // (2 * LANES)


def _token_rows(base, m, rpt, j):
    return pl.ds(base + j, m, stride=rpt)


def _pack_words(x):
    half = x.shape[1] // 2
    bits = pltpu.bitcast(x.astype(BF16).astype(F32), U32)
    words = (bits[:, :half] >> 16) | (bits[:, half:] & U32(HI_MASK))
    return [words[:, j * LANES:(j + 1) * LANES] for j in range(half // LANES)]


def _store_packed(ref, base, x):
    chunks = _pack_words(x)
    for j, chunk in enumerate(chunks):
        ref[_token_rows(base, x.shape[0], len(chunks), j), :] = chunk


def _unpack_words(chunks):
    lo = [pltpu.bitcast(c << 16, F32) for c in chunks]
    hi = [pltpu.bitcast(c & U32(HI_MASK), F32) for c in chunks]
    return jnp.concatenate(lo + hi, axis=1)


def _in_proj_kernel(x_ref, w_ref, b_ref, z_ref, xb_ref, *, tt, d):
    @pl.when(pl.program_id(1) == 0)
    def _():
        for t in range(0, tt, 2):
            rows = jnp.concatenate([x_ref[:, t * d:(t + 1) * d], x_ref[:, (t + 1) * d:(t + 2) * d]], axis=0)
            xb_ref[t * SUBLANES:(t + 2) * SUBLANES, :] = rows.astype(BF16)

    z = jnp.dot(xb_ref[...], w_ref[...], preferred_element_type=F32) + b_ref[...]
    z_ref[...] = z.astype(z_ref.dtype)


def _in_proj(xv, w, b, n, d):
    d_out = w.shape[1]
    tm = _pick(n, (1024, 512, 256, 128, 64, 32, 16))
    tn = _pick(d_out, (1024, 512, 256, 128))
    tt = tm // SUBLANES
    return pl.pallas_call(
        functools.partial(_in_proj_kernel, tt=tt, d=d),
        out_shape=(jax.ShapeDtypeStruct((n, d_out), BF16), jax.ShapeDtypeStruct((n, d), BF16)),
        grid=(n // tm, d_out // tn),
        in_specs=[
            pl.BlockSpec((SUBLANES, tt * d), lambda i, j: (0, i)),
            pl.BlockSpec((d, tn), lambda i, j: (0, j)),
            pl.BlockSpec((1, tn), lambda i, j: (0, j)),
        ],
        out_specs=(pl.BlockSpec((tm, tn), lambda i, j: (i, j)),
                   pl.BlockSpec((tm, d), lambda i, j: (i, 0))),
        compiler_params=_params(("arbitrary", "arbitrary")),
        name="in_proj",
    )(xv, w, b)


def _qkv_kernel(xb_ref, w_ref, b_ref, p_ref, o_ref):
    r = jnp.dot(xb_ref[...], w_ref[...], preferred_element_type=F32) + b_ref[...]
    rp = jnp.dot(p_ref[...], r.astype(BF16), preferred_element_type=F32)
    o_ref[...] = rp.astype(BF16).reshape(o_ref.shape)


def _qkv_proj(xb, w, b, g, n, d):
    _, dil = ATTN_GROUPS[g]
    stride = SUBLANES * dil
    gw = HEADS_PER_GROUP * HEAD_DIM
    tm = max(_pick(n, (1024, 512, 256)), 16 * stride)
    per = tm // stride
    new = jnp.arange(tm, dtype=jnp.int32)
    old = (new % per) * stride + new // per
    perm = (old[:, None] == jnp.arange(tm, dtype=jnp.int32)[None, :]).astype(BF16)
    return pl.pallas_call(
        _qkv_kernel,
        out_shape=jax.ShapeDtypeStruct((stride, n // stride, 3 * gw), BF16),
        grid=(n // tm, 3),
        in_specs=[
            pl.BlockSpec((tm, d), lambda i, j: (i, 0)),
            pl.BlockSpec((d, gw), lambda i, j: (0, j)),
            pl.BlockSpec((1, gw), lambda i, j: (0, j)),
            pl.BlockSpec((tm, tm), lambda i, j: (0, 0), pipeline_mode=pl.Buffered(1)),
        ],
        out_specs=pl.BlockSpec((stride, per, gw), lambda i, j: (0, i, j)),
        compiler_params=_params(("arbitrary", "arbitrary")),
        name=f"qkv_proj_g{g}",
    )(xb, w, b, perm)


def _lru_kernel(u_ref, g_ref, cw_ref, cb_ref, wa_ref, ba_ref, wx_ref, bx_ref, lam_ref, y_ref,
                carry_ref, h_ref, a_ref, b_ref, *, tt, nblk, bw):
    ti = pl.program_id(1)
    rows = tt * SUBLANES
    halo = SUBLANES * (CONV_WIDTH - 1)

    @pl.when(ti == 0)
    def _():
        carry_ref[...] = jnp.zeros_like(carry_ref)
        h_ref[...] = jnp.zeros_like(h_ref)

    u = u_ref[...].astype(F32)
    ext = jnp.concatenate([carry_ref[...], u], axis=0)
    uc = cb_ref[...] + cw_ref[0:1, :] * ext[0:rows, :]
    for j in range(1, CONV_WIDTH):
        uc = uc + cw_ref[j:j + 1, :] * ext[SUBLANES * j:SUBLANES * j + rows, :]
    carry_ref[...] = u[rows - halo:, :]

    ucb = uc.astype(BF16)
    r_parts, i_parts = [], []
    for kb in range(nblk):
        blk = ucb[:, kb * bw:(kb + 1) * bw]
        r_parts.append(jnp.dot(blk, wa_ref[kb], preferred_element_type=F32))
        i_parts.append(jnp.dot(blk, wx_ref[kb], preferred_element_type=F32))
    r = _sigmoid(jnp.concatenate(r_parts, axis=1) + ba_ref[...])
    ig = _sigmoid(jnp.concatenate(i_parts, axis=1) + bx_ref[...])

    nl = -lam_ref[...]
    softplus = jnp.maximum(nl, 0.0) + jnp.log(1.0 + jnp.exp(-jnp.abs(nl)))
    a = jnp.exp((-LRU_C) * r * softplus)
    mult = jnp.sqrt(1.0 - a * a)
    row = lax.broadcasted_iota(jnp.int32, a.shape, 0)
    mult = jnp.where((row < SUBLANES) & (ti == 0), 1.0, mult)
    a_ref[...] = a
    b_ref[...] = uc * ig * mult

    def step(t, h):
        r0 = pl.multiple_of(t * SUBLANES, SUBLANES)
        h = a_ref[pl.ds(r0, SUBLANES), :] * h + b_ref[pl.ds(r0, SUBLANES), :]
        b_ref[pl.ds(r0, SUBLANES), :] = h
        return h

    h_ref[...] = lax.fori_loop(0, tt, step, h_ref[...], unroll=8)

    g = g_ref[...].astype(F32)
    gelu = 0.5 * g * (1.0 + jnp.tanh(0.7978845608028654 * (g + 0.044715 * (g * g * g))))
    y_ref[...] = (b_ref[...] * gelu).astype(y_ref.dtype)


def _lru(z, conv_w, conv_b, wa, ba, wx, bx, lam, n, c, col_u, col_g):
    nblk_total, bw, _ = wa.shape
    ct = _pick(c, (512, 256, 128))
    tt = _pick(n // SUBLANES, (128, 64, 32, 16, 8))
    rows = tt * SUBLANES
    nblk = ct // bw
    vec = lambda: pl.BlockSpec((1, ct), lambda ci, ti: (0, ci))
    return pl.pallas_call(
        functools.partial(_lru_kernel, tt=tt, nblk=nblk, bw=bw),
        out_shape=jax.ShapeDtypeStruct((n, c), BF16),
        grid=(c // ct, n // rows),
        in_specs=[
            pl.BlockSpec((rows, ct), lambda ci, ti: (ti, col_u // ct + ci)),
            pl.BlockSpec((rows, ct), lambda ci, ti: (ti, col_g // ct + ci)),
            pl.BlockSpec((CONV_WIDTH, ct), lambda ci, ti: (0, ci)),
            vec(),
            pl.BlockSpec((nblk, bw, bw), lambda ci, ti: (ci, 0, 0)),
            vec(),
            pl.BlockSpec((nblk, bw, bw), lambda ci, ti: (ci, 0, 0)),
            vec(),
            vec(),
        ],
        out_specs=pl.BlockSpec((rows, ct), lambda ci, ti: (ti, ci)),
        scratch_shapes=[
            pltpu.VMEM((SUBLANES * (CONV_WIDTH - 1), ct), F32),
            pltpu.VMEM((SUBLANES, ct), F32),
            pltpu.VMEM((rows, ct), F32),
            pltpu.VMEM((rows, ct), F32),
        ],
        compiler_params=_params(("arbitrary", "arbitrary")),
        name="rg_lru",
    )(z, z, conv_w, conv_b, wa, ba, wx, bx, lam)


def _attn_kernel(q_ref, k_ref, v_ref, o_ref, st_ref, *, band, dil, slopes, qblocks):
    qi = lax.broadcasted_iota(jnp.int32, (band, band), 0)
    kj = lax.broadcasted_iota(jnp.int32, (band, band), 1)
    dist_c = (qi - kj).astype(F32)
    dist_p = dist_c + float(band)
    valid_c = kj <= qi
    in_band_p = kj >= qi
    scale = HEAD_DIM ** -0.5
    nt = (((1,), (1,)), ((), ()))
    lane = lax.broadcasted_iota(jnp.int32, (band, LANES), 1)

    for qq in range(qblocks):
        n = pl.program_id(1) * qblocks + qq
        prev = jnp.maximum(n - 1, 0)
        cur0 = pl.multiple_of(n * band, band)
        prev0 = pl.multiple_of(prev * band, band)
        rows = slice(qq * band, (qq + 1) * band)
        q = q_ref[rows, :]
        kc = k_ref[pl.ds(cur0, band), :]
        kp = k_ref[pl.ds(prev0, band), :]
        vc = v_ref[pl.ds(cur0, band), :]
        vp = v_ref[pl.ds(prev0, band), :]
        valid_p = in_band_p & (n > 0)

        outs = []
        stat = jnp.zeros((band, LANES), F32)
        for h in range(HEADS_PER_GROUP):
            sl = slice(h * HEAD_DIM, (h + 1) * HEAD_DIM)
            bias = slopes[h] * dil
            sc = lax.dot_general(q[:, sl], kc[:, sl], nt, preferred_element_type=F32)
            sp = lax.dot_general(q[:, sl], kp[:, sl], nt, preferred_element_type=F32)
            sc = jnp.where(valid_c, sc * scale - bias * dist_c, -jnp.inf)
            sp = jnp.where(valid_p, sp * scale - bias * dist_p, -jnp.inf)
            m = jnp.maximum(jnp.max(sc, axis=1, keepdims=True), jnp.max(sp, axis=1, keepdims=True))
            pc = jnp.exp(sc - m)
            pp = jnp.exp(sp - m)
            l = jnp.sum(pc, axis=1, keepdims=True) + jnp.sum(pp, axis=1, keepdims=True)
            acc = (jnp.dot(pc.astype(BF16), vc[:, sl], preferred_element_type=F32)
                   + jnp.dot(pp.astype(BF16), vp[:, sl], preferred_element_type=F32))
            outs.append(acc / l)
            stat = jnp.where(lane == h, m + jnp.log(l), stat)
        o_ref[rows, :] = jnp.concatenate(outs, axis=1).astype(o_ref.dtype)
        st_ref[rows, :] = stat


def _attn_group(qkv, g, n):
    window, dil = ATTN_GROUPS[g]
    band = window // dil
    stride = SUBLANES * dil
    lp = n // stride
    gw = HEADS_PER_GROUP * HEAD_DIM
    n_heads = HEADS_PER_GROUP * len(ATTN_GROUPS)
    slopes = tuple(2.0 ** (-8.0 * (g * HEADS_PER_GROUP + h + 1) / n_heads) for h in range(HEADS_PER_GROUP))
    qblocks = 2 if (lp // band) % 2 == 0 else 1
    tq = qblocks * band
    o, st = pl.pallas_call(
        functools.partial(_attn_kernel, band=band, dil=float(dil), slopes=slopes, qblocks=qblocks),
        out_shape=(jax.ShapeDtypeStruct((lp, stride * gw), BF16),
                   jax.ShapeDtypeStruct((lp, stride * LANES), F32)),
        grid=(stride, lp // tq),
        in_specs=[
            pl.BlockSpec((None, tq, gw), lambda c, i: (c, i, 0)),
            pl.BlockSpec((None, lp, gw), lambda c, i: (c, 0, 1)),
            pl.BlockSpec((None, lp, gw), lambda c, i: (c, 0, 2)),
        ],
        out_specs=(pl.BlockSpec((tq, gw), lambda c, i: (i, c)),
                   pl.BlockSpec((tq, LANES), lambda c, i: (i, c))),
        compiler_params=_params(("arbitrary", "arbitrary")),
        name=f"dilated_attn_g{g}",
    )(qkv, qkv, qkv)
    return o.reshape(n, gw), st.reshape(n, LANES)


def _merge_kernel(x_ref, yr_ref, gr_ref, ga_ref, o0_ref, o1_ref, o2_ref, s0_ref, s1_ref, s2_ref,
                  wr_ref, wa_ref, wo_ref, g1_ref, b1_ref, x1_ref, x1p_ref, *, tt, d):
    t1 = jnp.dot(yr_ref[...], wr_ref[...], preferred_element_type=F32)
    merged = _sigmoid(gr_ref[...].astype(F32)) * t1

    stats = [s0_ref[...], s1_ref[...], s2_ref[...]]
    outs = [o0_ref, o1_ref, o2_ref]
    heads = []
    for h in range(HEADS_PER_GROUP):
        lse = [s[:, h:h + 1] for s in stats]
        mx = jnp.maximum(jnp.maximum(lse[0], lse[1]), lse[2])
        e = [jnp.exp(v - mx) for v in lse]
        tot = e[0] + e[1] + e[2]
        acc = None
        for gi in range(len(ATTN_GROUPS)):
            term = (e[gi] / tot) * outs[gi][:, h * HEAD_DIM:(h + 1) * HEAD_DIM].astype(F32)
            acc = term if acc is None else acc + term
        heads.append(acc)
    oa = jnp.concatenate(heads, axis=1).astype(BF16)
    t2 = jnp.dot(oa, wa_ref[...], preferred_element_type=F32)
    merged = merged + _sigmoid(ga_ref[...].astype(F32)) * t2
    mix = jnp.dot(merged.astype(BF16), wo_ref[...], preferred_element_type=F32)

    v = DN_ALPHA * _time_major_rows(x_ref, tt, d) + mix
    mu = jnp.mean(v, axis=-1, keepdims=True)
    cen = v - mu
    var = jnp.mean(cen * cen, axis=-1, keepdims=True)
    x1 = cen * lax.rsqrt(var + LN_EPS) * g1_ref[...] + b1_ref[...]
    x1_ref[...] = x1
    _store_packed(x1p_ref, 0, x1)


def _merge(xv, yr, z, os_, sts, wr, wa, wo, g1, b1, n, d, col_gr, col_ga):
    tm = _pick(n, (256, 128, 64))
    tt = tm // SUBLANES
    gw = HEADS_PER_GROUP * HEAD_DIM
    rpt = _rows_per_token(d)
    const = lambda shape: pl.BlockSpec(shape, lambda i: (0, 0), pipeline_mode=pl.Buffered(1))
    row = lambda w: pl.BlockSpec((tm, w), lambda i: (i, 0))
    return pl.pallas_call(
        functools.partial(_merge_kernel, tt=tt, d=d),
        out_shape=(jax.ShapeDtypeStruct((n, d), F32), jax.ShapeDtypeStruct((n * rpt, LANES), U32)),
        grid=(n // tm,),
        in_specs=[
            pl.BlockSpec((SUBLANES, tt * d), lambda i: (0, i)),
            row(yr.shape[1]),
            pl.BlockSpec((tm, d), lambda i: (i, col_gr // d)),
            pl.BlockSpec((tm, d), lambda i: (i, col_ga // d)),
            row(gw), row(gw), row(gw),
            row(LANES), row(LANES), row(LANES),
            const(wr.shape), const(wa.shape), const(wo.shape),
            const((1, d)), const((1, d)),
        ],
        out_specs=(row(d), pl.BlockSpec((tm * rpt, LANES), lambda i: (i, 0))),
        compiler_params=_params(("arbitrary",)),
        name="merge_ln1",
    )(xv, yr, z, z, *os_, *sts, wr, wa, wo, g1, b1)


def _router_kernel(x_ref, wr_ref, bias_ref, idx_ref, w_ref, rank_ref, cnt_ref, base_ref, tri_ref, *, t, e):
    i = pl.program_id(0)

    @pl.when(i == 0)
    def _():
        base_ref[...] = jnp.zeros_like(base_ref)
        rr = lax.broadcasted_iota(jnp.int32, (t, t), 0)
        cc = lax.broadcasted_iota(jnp.int32, (t, t), 1)
        tri_ref[...] = jnp.where(rr < cc, 1.0, 0.0).astype(BF16)

    nt = (((1,), (1,)), ((), ()))
    logits = lax.dot_general(wr_ref[...], x_ref[...].astype(BF16), nt, preferred_element_type=F32)
    scores = _sigmoid(logits)
    biased = scores + bias_ref[...]

    per = e // N_GROUP
    sub = lax.broadcasted_iota(jnp.int32, (per, t), 0).astype(F32)
    blocks, gscore = [], []
    for g in range(N_GROUP):
        blk = biased[g * per:(g + 1) * per, :]
        m1 = jnp.max(blk, axis=0, keepdims=True)
        first = jnp.min(jnp.where(blk == m1, sub, float(per)), axis=0, keepdims=True)
        m2 = jnp.max(jnp.where(sub == first, -jnp.inf, blk), axis=0, keepdims=True)
        blocks.append(blk)
        gscore.append(m1 + m2)
    masked_blocks = []
    for g in range(N_GROUP):
        beaten = jnp.zeros((1, t), F32)
        for g2 in range(N_GROUP):
            if g2 == g:
                continue
            wins = (gscore[g2] > gscore[g]) | ((gscore[g2] == gscore[g]) & (g2 < g))
            beaten = beaten + jnp.where(wins, 1.0, 0.0)
        keep = beaten < float(TOPK_GROUP)
        masked_blocks.append(jnp.where(keep, blocks[g], -jnp.inf))
    masked = jnp.concatenate(masked_blocks, axis=0)

    eidx = lax.broadcasted_iota(jnp.int32, (e, t), 0).astype(F32)
    member = jnp.zeros((e, t), F32)
    sels, ws = [], []
    for _ in range(TOP_K):
        m = jnp.max(masked, axis=0, keepdims=True)
        sel = jnp.min(jnp.where(masked == m, eidx, float(e)), axis=0, keepdims=True)
        hit = eidx == sel
        ws.append(jnp.sum(jnp.where(hit, scores, 0.0), axis=0, keepdims=True))
        masked = jnp.where(hit, -jnp.inf, masked)
        member = jnp.where(hit, 1.0, member)
        sels.append(sel)
    wsum = ws[0]
    for k in range(1, TOP_K):
        wsum = wsum + ws[k]

    before = jnp.dot(member.astype(BF16), tri_ref[...], preferred_element_type=F32) + base_ref[...]
    for k in range(TOP_K):
        idx_ref[k:k + 1, :] = sels[k].astype(jnp.int32)
        w_ref[k:k + 1, :] = ws[k] / wsum * ROUTED_SCALE
        rk = jnp.sum(jnp.where(eidx == sels[k], before, 0.0), axis=0, keepdims=True)
        rank_ref[k:k + 1, :] = rk.astype(jnp.int32)
    base_ref[...] = base_ref[...] + jnp.sum(member, axis=1, keepdims=True)
    cnt_ref[...] = jnp.broadcast_to(base_ref[...], cnt_ref.shape)


def _router(x1, wrt, bias, n, d):
    e = wrt.shape[0]
    t = _pick(n, (512, 256, 128))
    tok = lambda dt: jax.ShapeDtypeStruct((TOP_K, n), dt)
    return pl.pallas_call(
        functools.partial(_router_kernel, t=t, e=e),
        out_shape=(tok(jnp.int32), tok(F32), tok(jnp.int32), jax.ShapeDtypeStruct((e, LANES), F32)),
        grid=(n // t,),
        in_specs=[
            pl.BlockSpec((t, d), lambda i: (i, 0)),
            pl.BlockSpec((e, d), lambda i: (0, 0)),
            pl.BlockSpec((e, 1), lambda i: (0, 0)),
        ],
        out_specs=(pl.BlockSpec((TOP_K, t), lambda i: (0, i)),
                   pl.BlockSpec((TOP_K, t), lambda i: (0, i)),
                   pl.BlockSpec((TOP_K, t), lambda i: (0, i)),
                   pl.BlockSpec((e, LANES), lambda i: (0, 0))),
        scratch_shapes=[pltpu.VMEM((e, 1), F32), pltpu.VMEM((t, t), BF16)],
        compiler_params=_params(("arbitrary",)),
        name="router_topk",
    )(x1, wrt, bias)


def _dest_kernel(idx_ref, rank_ref, start_ref, dest_ref, *, t, e):
    eidx = lax.broadcasted_iota(jnp.int32, (e, t), 0)
    for k in range(TOP_K):
        hit = eidx == idx_ref[k:k + 1, :]
        st = jnp.sum(jnp.where(hit, start_ref[...], 0.0), axis=0, keepdims=True)
        dest_ref[0, k:k + 1, :] = st.astype(jnp.int32) + rank_ref[k:k + 1, :]


def _dest(idx, rank, start_col, n, t):
    e = start_col.shape[0]
    return pl.pallas_call(
        functools.partial(_dest_kernel, t=t, e=e),
        out_shape=jax.ShapeDtypeStruct((n // t, TOP_K, t), jnp.int32),
        grid=(n // t,),
        in_specs=[
            pl.BlockSpec((TOP_K, t), lambda i: (0, i)),
            pl.BlockSpec((TOP_K, t), lambda i: (0, i)),
            pl.BlockSpec((e, 1), lambda i: (0, 0)),
        ],
        out_specs=pl.BlockSpec((1, TOP_K, t), lambda i: (i, 0, 0)),
        compiler_params=_params(("arbitrary",)),
        name="moe_dest",
    )(idx, rank, start_col)


def _token_copy(src_ref, src_tok, dst_ref, dst_tok, sem, rpt):
    s0 = pl.multiple_of(src_tok * rpt, rpt)
    d0 = pl.multiple_of(dst_tok * rpt, rpt)
    return pltpu.make_async_copy(src_ref.at[pl.ds(s0, rpt), :], dst_ref.at[pl.ds(d0, rpt), :], sem)


TOKENS_PER_SLOT_ROW = LANES // TOP_K


def _for_each_token_slots(table, t, fn):
    def body(r, c):
        for pair in range(0, TOKENS_PER_SLOT_ROW, 2):
            slots = [[table[r, (pair + p) * TOP_K + k] for k in range(TOP_K)] for p in range(2)]
            for p in range(2):
                fn(r * TOKENS_PER_SLOT_ROW + pair + p, slots[p])
        return c

    lax.fori_loop(0, t // TOKENS_PER_SLOT_ROW, body, 0)


def _slot_table_copy(dest_hbm, step, dsm, slot, sem_idx):
    return pltpu.make_async_copy(dest_hbm.at[step], dsm.at[slot], sem_idx.at[slot])


def _dispatch_kernel(dest_hbm, x_ref, wsg_ref, wsu_ref, wsd_ref, xs_hbm, sh_ref, dsm, zero_ref, sem_idx, sem,
                     sem_pad, *, t, rpt, n_steps, pad_rows):
    i = pl.program_id(0)
    slot = i % 2

    @pl.when(i == 0)
    def _():
        _slot_table_copy(dest_hbm, 0, dsm, 0, sem_idx).start()
        zero_ref[...] = jnp.zeros_like(zero_ref)
        pad = pltpu.make_async_copy(zero_ref, xs_hbm.at[pl.ds(n_steps * t * TOP_K * rpt, pad_rows), :], sem_pad)
        pad.start()
        pad.wait()

    _slot_table_copy(dest_hbm, i, dsm, slot, sem_idx).wait()

    @pl.when(i + 1 < n_steps)
    def _():
        _slot_table_copy(dest_hbm, i + 1, dsm, 1 - slot, sem_idx).start()

    def issue(tl, slots):
        for k in range(TOP_K):
            _token_copy(x_ref, tl, xs_hbm, slots[k], sem, rpt).start(priority=k % 2)

    _for_each_token_slots(dsm.at[slot], t, issue)

    x = _unpack_words([x_ref[_token_rows(0, t, rpt, j), :] for j in range(rpt)]).astype(BF16)
    g = jnp.dot(x, wsg_ref[...], preferred_element_type=F32)
    u = jnp.dot(x, wsu_ref[...], preferred_element_type=F32)
    h = (g * _sigmoid(g)) * u
    sh_ref[...] = jnp.dot(h.astype(BF16), wsd_ref[...], preferred_element_type=F32).astype(sh_ref.dtype)

    for _ in range(TOP_K):
        pltpu.make_async_copy(x_ref, xs_hbm.at[pl.ds(0, t * rpt), :], sem).wait()


def _dispatch(dest3, x1p, wsg, wsu, wsd, n, d, t, pad_tokens):
    rpt = _rows_per_token(d)
    const = lambda shape: pl.BlockSpec(shape, lambda i: (0, 0), pipeline_mode=pl.Buffered(1))
    return pl.pallas_call(
        functools.partial(_dispatch_kernel, t=t, rpt=rpt, n_steps=n // t, pad_rows=pad_tokens * rpt),
        out_shape=(jax.ShapeDtypeStruct(((n * TOP_K + pad_tokens) * rpt, LANES), U32),
                   jax.ShapeDtypeStruct((n, d), BF16)),
        grid=(n // t,),
        in_specs=[pl.BlockSpec(memory_space=pl.ANY), pl.BlockSpec((t * rpt, LANES), lambda i: (i, 0)),
                  const(wsg.shape), const(wsu.shape), const(wsd.shape)],
        out_specs=(pl.BlockSpec(memory_space=pl.ANY), pl.BlockSpec((t, d), lambda i: (i, 0))),
        scratch_shapes=[pltpu.SMEM((2, TOP_K * t // LANES, LANES), jnp.int32),
                        pltpu.VMEM((pad_tokens * rpt, LANES), U32),
                        pltpu.SemaphoreType.DMA((2,)), pltpu.SemaphoreType.DMA, pltpu.SemaphoreType.DMA],
        compiler_params=_params(("arbitrary",)),
        name="moe_dispatch",
    )(dest3, x1p, wsg, wsu, wsd)


def _expert_kernel(r0_ref, we_ref, nw_ref, nx_ref, par_ref, xs_hbm, wg_hbm, wu_hbm, wd_hbm, o_hbm,
                   xin_ref, out_ref, wgf_ref, wuf_ref, wdf_ref, wgb_ref, wub_ref, wdb_ref,
                   wsem, isem, osem, *, tm, rpt):
    w = pl.program_id(0)
    nw = nw_ref[0]
    rows = tm * rpt

    def weight_copies(expert, buf):
        return [pltpu.make_async_copy(src.at[expert], dst.at[buf], wsem.at[buf])
                for src, dst in ((wg_hbm, wgf_ref), (wu_hbm, wuf_ref), (wd_hbm, wdf_ref))]

    def rows_in(item, buf):
        r0 = pl.multiple_of(r0_ref[item] * rpt, rpt)
        return pltpu.make_async_copy(xs_hbm.at[pl.ds(r0, rows), :], xin_ref.at[buf], isem.at[buf])

    def rows_out(item, buf):
        r0 = pl.multiple_of(r0_ref[item] * rpt, rpt)
        return pltpu.make_async_copy(out_ref.at[buf], o_hbm.at[pl.ds(r0, rows), :], osem.at[buf])

    @pl.when(w < nw)
    def _():
        e = we_ref[w]
        pw = jnp.maximum(w - 1, 0)
        buf_io = w % 2

        @pl.when(w == 0)
        def _():
            rows_in(0, 0).start()

        @pl.when(w + 1 < nw)
        def _():
            rows_in(w + 1, 1 - buf_io).start()

        @pl.when((w == 0) | (e != we_ref[pw]))
        def _():
            buf = par_ref[w]
            nxt = nx_ref[w]

            @pl.when(w == 0)
            def _():
                for c in weight_copies(e, buf):
                    c.start()

            @pl.when(nxt >= 0)
            def _():
                for c in weight_copies(nxt, 1 - buf):
                    c.start()

            for c in weight_copies(e, buf):
                c.wait()
            wgb_ref[...] = wgf_ref[buf].astype(BF16)
            wub_ref[...] = wuf_ref[buf].astype(BF16)
            wdb_ref[...] = wdf_ref[buf].astype(BF16)

        rows_in(w, buf_io).wait()
        xin = xin_ref.at[buf_io]
        x = _unpack_words([xin[_token_rows(0, tm, rpt, j), :] for j in range(rpt)]).astype(BF16)
        g = jnp.dot(x, wgb_ref[...], preferred_element_type=F32)
        u = jnp.dot(x, wub_ref[...], preferred_element_type=F32)
        h = (g * _sigmoid(g)) * u
        o = jnp.dot(h.astype(BF16), wdb_ref[...], preferred_element_type=F32)

        _store_packed(out_ref.at[buf_io], 0, o)

        @pl.when(w > 0)
        def _():
            rows_out(pw, 1 - buf_io).wait()

        rows_out(w, buf_io).start()

        @pl.when(w == nw - 1)
        def _():
            rows_out(w, buf_io).wait()


def _experts(work, xs, w_gate, w_up, w_down, tm):
    _, d, de = w_gate.shape
    rpt = _rows_per_token(d)
    n_work = work[0].shape[0]
    hbm = pl.BlockSpec(memory_space=pl.ANY)
    return pl.pallas_call(
        functools.partial(_expert_kernel, tm=tm, rpt=rpt),
        out_shape=jax.ShapeDtypeStruct(xs.shape, U32),
        grid_spec=pltpu.PrefetchScalarGridSpec(
            num_scalar_prefetch=len(work),
            grid=(n_work,),
            in_specs=[hbm, hbm, hbm, hbm],
            out_specs=hbm,
            scratch_shapes=[pltpu.VMEM((2, tm * rpt, LANES), U32), pltpu.VMEM((2, tm * rpt, LANES), U32),
                            pltpu.VMEM((2, d, de), F32), pltpu.VMEM((2, d, de), F32), pltpu.VMEM((2, de, d), F32),
                            pltpu.VMEM((d, de), BF16), pltpu.VMEM((d, de), BF16), pltpu.VMEM((de, d), BF16),
                            pltpu.SemaphoreType.DMA((2,)), pltpu.SemaphoreType.DMA((2,)),
                            pltpu.SemaphoreType.DMA((2,))],
        ),
        compiler_params=_params(("arbitrary",)),
        name="moe_experts",
    )(*work, xs, w_gate, w_up, w_down)


def _combine_kernel(dest_hbm, x1_ref, wt_ref, sh_ref, o_hbm, g2_ref, b2_ref, y_ref,
                    dsm, gbuf, sem_idx, sem, *, t, tt, d, rpt, n_steps):
    i = pl.program_id(0)
    slot = i % 2

    def issue(step_slot):
        def gather(tl, slots):
            for k in range(TOP_K):
                _token_copy(o_hbm, slots[k], gbuf.at[step_slot], k * t + tl, sem.at[step_slot],
                            rpt).start(priority=k % 2)

        _for_each_token_slots(dsm.at[step_slot], t, gather)

    @pl.when(i == 0)
    def _():
        first = _slot_table_copy(dest_hbm, 0, dsm, 0, sem_idx)
        first.start()
        first.wait()
        issue(0)
        if n_steps > 1:
            _slot_table_copy(dest_hbm, 1, dsm, 1, sem_idx).start()

    @pl.when(i + 1 < n_steps)
    def _():
        _slot_table_copy(dest_hbm, i + 1, dsm, 1 - slot, sem_idx).wait()
        issue(1 - slot)

        @pl.when(i + 2 < n_steps)
        def _():
            _slot_table_copy(dest_hbm, i + 2, dsm, slot, sem_idx).start()

    gcur = gbuf.at[slot]
    pltpu.make_async_copy(o_hbm.at[pl.ds(0, TOP_K * t * rpt), :], gcur, sem.at[slot]).wait()

    acc_lo = [None] * rpt
    acc_hi = [None] * rpt
    for k in range(TOP_K):
        wk = wt_ref[:, k:k + 1]
        for j in range(rpt):
            words = gcur[_token_rows(k * t * rpt, t, rpt, j), :]
            lo = wk * pltpu.bitcast(words << 16, F32)
            hi = wk * pltpu.bitcast(words & U32(HI_MASK), F32)
            acc_lo[j] = lo if k == 0 else acc_lo[j] + lo
            acc_hi[j] = hi if k == 0 else acc_hi[j] + hi
    routed = jnp.concatenate(acc_lo + acc_hi, axis=1)

    v = DN_ALPHA * x1_ref[...] + (routed + sh_ref[...].astype(F32))
    mu = jnp.mean(v, axis=-1, keepdims=True)
    cen = v - mu
    var = jnp.mean(cen * cen, axis=-1, keepdims=True)
    y = cen * lax.rsqrt(var + LN_EPS) * g2_ref[...] + b2_ref[...]
    for s in range(tt):
        y_ref[:, s * d:(s + 1) * d] = y[s * SUBLANES:(s + 1) * SUBLANES, :]


def _combine(dest3, x1, wt, sh, o, g2, b2, n, d, t):
    tt = t // SUBLANES
    rpt = _rows_per_token(d)
    const = lambda shape: pl.BlockSpec(shape, lambda i: (0, 0), pipeline_mode=pl.Buffered(1))
    return pl.pallas_call(
        functools.partial(_combine_kernel, t=t, tt=tt, d=d, rpt=rpt, n_steps=n // t),
        out_shape=jax.ShapeDtypeStruct((SUBLANES, (n // SUBLANES) * d), F32),
        grid=(n // t,),
        in_specs=[
            pl.BlockSpec(memory_space=pl.ANY),
            pl.BlockSpec((t, d), lambda i: (i, 0)),
            pl.BlockSpec((t, TOP_K), lambda i: (i, 0)),
            pl.BlockSpec((t, d), lambda i: (i, 0)),
            pl.BlockSpec(memory_space=pl.ANY),
            const((1, d)), const((1, d)),
        ],
        out_specs=pl.BlockSpec((SUBLANES, tt * d), lambda i: (0, i)),
        scratch_shapes=[pltpu.SMEM((2, TOP_K * t // LANES, LANES), jnp.int32), pltpu.VMEM((2, TOP_K * t * rpt, LANES), U32),
                        pltpu.SemaphoreType.DMA((2,)), pltpu.SemaphoreType.DMA((2,))],
        compiler_params=_params(("arbitrary",)),
        name="moe_combine_ln2",
    )(dest3, x1, wt, sh, o, g2, b2)


def _work_items(counts, m, tm):
    e = counts.shape[0]
    end = jnp.cumsum(counts)
    start = end - counts
    nchunk = (counts + tm - 1) // tm
    cum = jnp.cumsum(nchunk)
    off = cum - nchunk
    n_work = m // tm + e
    w = jnp.arange(n_work, dtype=jnp.int32)
    we = jnp.minimum(jnp.sum(cum[None, :] <= w[:, None], axis=1), e - 1).astype(jnp.int32)
    row0 = (start[we] + (w - off[we]) * tm).astype(jnp.int32)
    nw = cum[-1].astype(jnp.int32)
    row0 = jnp.where(w < nw, row0, 0)
    ids = jnp.arange(e, dtype=jnp.int32)
    later = lax.cummin(jnp.where(counts > 0, ids, e)[::-1])[::-1]
    nxt_of = jnp.concatenate([later[1:], jnp.full((1,), e, jnp.int32)])
    nxt_of = jnp.where(nxt_of >= e, -1, nxt_of).astype(jnp.int32)
    ordinal = jnp.cumsum((counts > 0).astype(jnp.int32)) - 1
    work = (row0, we, nw.reshape(1), nxt_of[we], (ordinal[we] % 2).astype(jnp.int32))
    return work, start.astype(jnp.int32)


def kernel(x, w_in, b_in, conv_w, conv_b, lru_wa, lru_ba, lru_wx, lru_bx, lru_lambda, w_rnn_br, w_attn_br, w_out, ln1_g, ln1_b, w_router, router_bias, w_gate, w_up, w_down, ws_gate, ws_up, ws_down, ln2_g, ln2_b):
    bsz, s, d = x.shape
    assert bsz == SUBLANES, "time-major rows need batch == 8"
    assert DEPTH == 1 and w_in.shape[0] == 1
    n = bsz * s
    c = lru_lambda.shape[1]
    a = HEADS_PER_GROUP * len(ATTN_GROUPS) * HEAD_DIM
    assert c == d and w_in.shape[2] == 2 * c + 3 * a + 2 * d
    layer = 0

    gw = HEADS_PER_GROUP * HEAD_DIM
    src = (0, c, 2 * c, 2 * c + a, 2 * c + 2 * a, 2 * c + 3 * a, 2 * c + 3 * a + d, 2 * c + 3 * a + 2 * d)
    part = lambda v, p, lo=0, hi=None: v[..., src[p]:src[p + 1]][..., lo:hi]
    cat = lambda v, pieces: jnp.concatenate([part(v, *p) for p in pieces], axis=-1)
    main = ((0,), (1,), (5,), (6,))
    w_main = cat(w_in[layer], main).astype(BF16)
    b_main = cat(b_in[layer], main).reshape(1, -1)
    col_u, col_g, col_gr, col_ga = 0, c, 2 * c, 2 * c + d
    row = lambda v: v.reshape(1, -1)

    xv = x.reshape(bsz, s * d)
    z, xb = _in_proj(xv, w_main, b_main, n, d)

    yr = _lru(z, conv_w[layer], row(conv_b[layer]), lru_wa[layer].astype(BF16), row(lru_ba[layer]),
              lru_wx[layer].astype(BF16), row(lru_bx[layer]), row(lru_lambda[layer]), n, c, col_u, col_g)

    os_, sts = [], []
    for g in range(len(ATTN_GROUPS)):
        cols = tuple((p, g * gw, (g + 1) * gw) for p in (2, 3, 4))
        qkv = _qkv_proj(xb, cat(w_in[layer], cols).astype(BF16), cat(b_in[layer], cols).reshape(1, -1), g, n, d)
        o, st = _attn_group(qkv, g, n)
        os_.append(o)
        sts.append(st)

    x1, x1p = _merge(xv, yr, z, os_, sts, w_rnn_br[layer].astype(BF16), w_attn_br[layer].astype(BF16),
                     w_out[layer].astype(BF16), row(ln1_g[layer]), row(ln1_b[layer]), n, d, col_gr, col_ga)

    n_exp = w_router.shape[2]
    idx, wts, rank, cnt = _router(x1, w_router[layer].T.astype(BF16), router_bias[layer].reshape(n_exp, 1), n, d)

    tm = 256
    t_tok = _pick(n, (256, 128))
    m = n * TOP_K
    counts = cnt[:, 0].astype(jnp.int32)
    work, start = _work_items(counts, m, tm)
    dest3 = _dest(idx, rank, start.astype(F32).reshape(n_exp, 1), n, t_tok)
    dest3 = dest3.transpose(0, 2, 1).reshape(n // t_tok, t_tok * TOP_K // LANES, LANES)

    xs, shared = _dispatch(dest3, x1p, ws_gate[layer].astype(BF16), ws_up[layer].astype(BF16),
                           ws_down[layer].astype(BF16), n, d, t_tok, pad_tokens=tm)
    o = _experts(work, xs, w_gate[layer], w_up[layer], w_down[layer], tm)
    y = _combine(dest3, x1, wts.T, shared, o, row(ln2_g[layer]), row(ln2_b[layer]), n, d, t_tok)
    return y.reshape(bsz, s, d)
```

```python
import functools

import jax
import jax.numpy as jnp
from jax import lax
from jax.experimental import pallas as pl
from jax.experimental.pallas import tpu as pltpu

HEAD_DIM = 128
ATTN_GROUPS = ((128, 1), (512, 4), (2048, 16))
HEADS_PER_GROUP = 4
CONV_WIDTH = 4
LRU_C = 8.0
N_GROUP = 8
TOPK_GROUP = 4
TOP_K = 8
ROUTED_SCALE = 2.5
DEPTH = 1
DN_ALPHA = (2 * DEPTH) ** 0.25
LN_EPS = 1e-5

SUBLANES = 8
LANES = 128
V7X_VMEM_LIMIT_BYTES = 56 * 1024 * 1024

F32 = jnp.float32
BF16 = jnp.bfloat16


def _pick(n, cands):
    for c in cands:
        if n % c == 0:
            return c
    raise ValueError(f"no tile in {cands} divides {n}")


def _sigmoid(x):
    return 1.0 / (1.0 + jnp.exp(-x))


def _params(sem, vmem=V7X_VMEM_LIMIT_BYTES):
    return pltpu.CompilerParams(dimension_semantics=sem, vmem_limit_bytes=vmem)


def _time_major_rows(x_ref, tt, d):
    return jnp.concatenate([x_ref[:, t * d:(t + 1) * d] for t in range(tt)], axis=0)


U32 = jnp.uint32
HI_MASK = 0xFFFF0000


def _rows_per_token(d):
    assert d % (2 * LANES) == 0
    return d // (2 * LANES)


def _token_rows(base, m, rpt, j):
    return pl.ds(base + j, m, stride=rpt)


def _pack_words(x):
    half = x.shape[1] // 2
    bits = pltpu.bitcast(x.astype(BF16).astype(F32), U32)
    words = (bits[:, :half] >> 16) | (bits[:, half:] & U32(HI_MASK))
    return [words[:, j * LANES:(j + 1) * LANES] for j in range(half // LANES)]


def _store_packed(ref, base, x):
    chunks = _pack_words(x)
    for j, chunk in enumerate(chunks):
        ref[_token_rows(base, x.shape[0], len(chunks), j), :] = chunk


def _unpack_words(chunks):
    lo = [pltpu.bitcast(c << 16, F32) for c in chunks]
    hi = [pltpu.bitcast(c & U32(HI_MASK), F32) for c in chunks]
    return jnp.concatenate(lo + hi, axis=1)


def _in_proj_kernel(x_ref, w_ref, b_ref, z_ref, xb_ref, *, tt, d):
    @pl.when(pl.program_id(1) == 0)
    def _():
        for t in range(0, tt, 2):
            rows = jnp.concatenate([x_ref[:, t * d:(t + 1) * d], x_ref[:, (t + 1) * d:(t + 2) * d]], axis=0)
            xb_ref[t * SUBLANES:(t + 2) * SUBLANES, :] = rows.astype(BF16)

    z = jnp.dot(xb_ref[...], w_ref[...], preferred_element_type=F32) + b_ref[...]
    z_ref[...] = z.astype(z_ref.dtype)


def _in_proj(xv, w, b, n, d):
    d_out = w.shape[1]
    tm = _pick(n, (1024, 512, 256, 128, 64, 32, 16))
    tn = _pick(d_out, (1024, 512, 256, 128))
    tt = tm // SUBLANES
    return pl.pallas_call(
        functools.partial(_in_proj_kernel, tt=tt, d=d),
        out_shape=(jax.ShapeDtypeStruct((n, d_out), BF16), jax.ShapeDtypeStruct((n, d), BF16)),
        grid=(n // tm, d_out // tn),
        in_specs=[
            pl.BlockSpec((SUBLANES, tt * d), lambda i, j: (0, i)),
            pl.BlockSpec((d, tn), lambda i, j: (0, j)),
            pl.BlockSpec((1, tn), lambda i, j: (0, j)),
        ],
        out_specs=(pl.BlockSpec((tm, tn), lambda i, j: (i, j)),
                   pl.BlockSpec((tm, d), lambda i, j: (i, 0))),
        compiler_params=_params(("arbitrary", "arbitrary")),
        name="in_proj",
    )(xv, w, b)


BF16_SUBLANES = 16


def _qkv_kernel(xb_ref, w_ref, b_ref, p_ref, o_ref, *, stride):
    r = (jnp.dot(xb_ref[...], w_ref[...], preferred_element_type=F32) + b_ref[...]).astype(BF16)
    sb = p_ref.shape[0]
    for s in range(r.shape[0] // (2 * sb)):
        halves = [jnp.dot(p_ref[...], r[(2 * s + i) * sb:(2 * s + i + 1) * sb, :], preferred_element_type=F32)
                  for i in range(2)]
        for c in range(stride):
            tile = jnp.concatenate([hv[c * SUBLANES:(c + 1) * SUBLANES, :] for hv in halves], axis=0)
            o_ref[c, s * BF16_SUBLANES:(s + 1) * BF16_SUBLANES, :] = tile.astype(BF16)


def _qkv_proj(xb, w, b, g, n, d):
    _, dil = ATTN_GROUPS[g]
    stride = SUBLANES * dil
    gw = HEADS_PER_GROUP * HEAD_DIM
    sb = SUBLANES * stride
    tm = max(_pick(n, (1024, 512, 256)), 2 * sb)
    per = tm // stride
    new = jnp.arange(sb, dtype=jnp.int32)
    old = (new % SUBLANES) * stride + new // SUBLANES
    perm = (old[:, None] == jnp.arange(sb, dtype=jnp.int32)[None, :]).astype(BF16)
    return pl.pallas_call(
        functools.partial(_qkv_kernel, stride=stride),
        out_shape=jax.ShapeDtypeStruct((stride, n // stride, 3 * gw), BF16),
        grid=(n // tm, 3),
        in_specs=[
            pl.BlockSpec((tm, d), lambda i, j: (i, 0)),
            pl.BlockSpec((d, gw), lambda i, j: (0, j)),
            pl.BlockSpec((1, gw), lambda i, j: (0, j)),
            pl.BlockSpec((sb, sb), lambda i, j: (0, 0), pipeline_mode=pl.Buffered(1)),
        ],
        out_specs=pl.BlockSpec((stride, per, gw), lambda i, j: (0, i, j)),
        compiler_params=_params(("arbitrary", "arbitrary")),
        name=f"qkv_proj_g{g}",
    )(xb, w, b, perm)


def _lru_kernel(u_ref, g_ref, cw_ref, cb_ref, wa_ref, ba_ref, wx_ref, bx_ref, lam_ref, y_ref,
                carry_ref, h_ref, a_ref, b_ref, *, tt, nblk, bw):
    ti = pl.program_id(1)
    rows = tt * SUBLANES
    halo = SUBLANES * (CONV_WIDTH - 1)

    @pl.when(ti == 0)
    def _():
        carry_ref[...] = jnp.zeros_like(carry_ref)
        h_ref[...] = jnp.zeros_like(h_ref)

    u = u_ref[...].astype(F32)
    ext = jnp.concatenate([carry_ref[...], u], axis=0)
    uc = cb_ref[...] + cw_ref[0:1, :] * ext[0:rows, :]
    for j in range(1, CONV_WIDTH):
        uc = uc + cw_ref[j:j + 1, :] * ext[SUBLANES * j:SUBLANES * j + rows, :]
    carry_ref[...] = u[rows - halo:, :]

    ucb = uc.astype(BF16)
    r_parts, i_parts = [], []
    for kb in range(nblk):
        blk = ucb[:, kb * bw:(kb + 1) * bw]
        r_parts.append(jnp.dot(blk, wa_ref[kb], preferred_element_type=F32))
        i_parts.append(jnp.dot(blk, wx_ref[kb], preferred_element_type=F32))
    r = _sigmoid(jnp.concatenate(r_parts, axis=1) + ba_ref[...])
    ig = _sigmoid(jnp.concatenate(i_parts, axis=1) + bx_ref[...])

    nl = -lam_ref[...]
    softplus = jnp.maximum(nl, 0.0) + jnp.log(1.0 + jnp.exp(-jnp.abs(nl)))
    a = jnp.exp((-LRU_C) * r * softplus)
    gap = 1.0 - a * a
    mult = jnp.where(gap > 0.0, gap * lax.rsqrt(gap), 0.0)
    gated = uc * ig
    a_ref[...] = a
    b_ref[...] = gated * mult

    @pl.when(ti == 0)
    def _():
        b_ref[0:SUBLANES, :] = gated[0:SUBLANES, :]

    def step(t, h):
        r0 = pl.multiple_of(t * SUBLANES, SUBLANES)
        h = a_ref[pl.ds(r0, SUBLANES), :] * h + b_ref[pl.ds(r0, SUBLANES), :]
        b_ref[pl.ds(r0, SUBLANES), :] = h
        return h

    h_ref[...] = lax.fori_loop(0, tt, step, h_ref[...], unroll=8)

    g = g_ref[...].astype(F32)
    gelu = 0.5 * g * (1.0 + jnp.tanh(0.7978845608028654 * (g + 0.044715 * (g * g * g))))
    y_ref[...] = (b_ref[...] * gelu).astype(y_ref.dtype)


def _lru(z, conv_w, conv_b, wa, ba, wx, bx, lam, n, c, col_u, col_g):
    nblk_total, bw, _ = wa.shape
    ct = _pick(c, (512, 256, 128))
    tt = _pick(n // SUBLANES, (128, 64, 32, 16, 8))
    rows = tt * SUBLANES
    nblk = ct // bw
    vec = lambda: pl.BlockSpec((1, ct), lambda ci, ti: (0, ci))
    return pl.pallas_call(
        functools.partial(_lru_kernel, tt=tt, nblk=nblk, bw=bw),
        out_shape=jax.ShapeDtypeStruct((n, c), BF16),
        grid=(c // ct, n // rows),
        in_specs=[
            pl.BlockSpec((rows, ct), lambda ci, ti: (ti, col_u // ct + ci)),
            pl.BlockSpec((rows, ct), lambda ci, ti: (ti, col_g // ct + ci)),
            pl.BlockSpec((CONV_WIDTH, ct), lambda ci, ti: (0, ci)),
            vec(),
            pl.BlockSpec((nblk, bw, bw), lambda ci, ti: (ci, 0, 0)),
            vec(),
            pl.BlockSpec((nblk, bw, bw), lambda ci, ti: (ci, 0, 0)),
            vec(),
            vec(),
        ],
        out_specs=pl.BlockSpec((rows, ct), lambda ci, ti: (ti, ci)),
        scratch_shapes=[
            pltpu.VMEM((SUBLANES * (CONV_WIDTH - 1), ct), F32),
            pltpu.VMEM((SUBLANES, ct), F32),
            pltpu.VMEM((rows, ct), F32),
            pltpu.VMEM((rows, ct), F32),
        ],
        compiler_params=_params(("arbitrary", "arbitrary")),
        name="rg_lru",
    )(z, z, conv_w, conv_b, wa, ba, wx, bx, lam)


def _attn_kernel(q_ref, k_ref, v_ref, o_ref, st_ref, *, band, dil, slopes, qblocks):
    qi = lax.broadcasted_iota(jnp.int32, (band, 2 * band), 0)
    kj = lax.broadcasted_iota(jnp.int32, (band, 2 * band), 1)
    rel = qi - kj
    scale = HEAD_DIM ** -0.5
    nt = (((1,), (1,)), ((), ()))
    lane = lax.broadcasted_iota(jnp.int32, (band, LANES), 1)

    for qq in range(qblocks):
        n = pl.program_id(1) * qblocks + qq
        win0 = pl.multiple_of(jnp.maximum(n - 1, 0) * band, band)
        own = jnp.where(n > 0, band, 0)
        rows = slice(qq * band, (qq + 1) * band)
        q = q_ref[rows, :]
        kw = k_ref[pl.ds(win0, 2 * band), :]
        vw = v_ref[pl.ds(win0, 2 * band), :]
        dist_i = rel + own
        valid = (dist_i >= 0) & (dist_i <= band)
        dist = dist_i.astype(F32)

        outs = []
        stat = jnp.zeros((band, LANES), F32)
        for h in range(HEADS_PER_GROUP):
            sl = slice(h * HEAD_DIM, (h + 1) * HEAD_DIM)
            bias = slopes[h] * dil
            s = lax.dot_general(q[:, sl], kw[:, sl], nt, preferred_element_type=F32)
            s = jnp.where(valid, s * scale - bias * dist, -jnp.inf)
            m = jnp.max(s, axis=1, keepdims=True)
            p = jnp.exp(s - m)
            l = jnp.sum(p, axis=1, keepdims=True)
            acc = jnp.dot(p.astype(BF16), vw[:, sl], preferred_element_type=F32)
            outs.append(acc / l)
            stat = jnp.where(lane == h, m + jnp.log(l), stat)
        o_ref[rows, :] = jnp.concatenate(outs, axis=1).astype(o_ref.dtype)
        st_ref[rows, :] = stat


def _attn_group(qkv, g, n):
    window, dil = ATTN_GROUPS[g]
    band = window // dil
    stride = SUBLANES * dil
    lp = n // stride
    gw = HEADS_PER_GROUP * HEAD_DIM
    n_heads = HEADS_PER_GROUP * len(ATTN_GROUPS)
    slopes = tuple(2.0 ** (-8.0 * (g * HEADS_PER_GROUP + h + 1) / n_heads) for h in range(HEADS_PER_GROUP))
    qblocks = _pick(lp // band, (4, 2, 1))
    tq = qblocks * band
    o, st = pl.pallas_call(
        functools.partial(_attn_kernel, band=band, dil=float(dil), slopes=slopes, qblocks=qblocks),
        out_shape=(jax.ShapeDtypeStruct((lp, stride * gw), BF16),
                   jax.ShapeDtypeStruct((lp, stride * LANES), F32)),
        grid=(stride, lp // tq),
        in_specs=[
            pl.BlockSpec((None, tq, gw), lambda c, i: (c, i, 0)),
            pl.BlockSpec((None, lp, gw), lambda c, i: (c, 0, 1)),
            pl.BlockSpec((None, lp, gw), lambda c, i: (c, 0, 2)),
        ],
        out_specs=(pl.BlockSpec((tq, gw), lambda c, i: (i, c)),
                   pl.BlockSpec((tq, LANES), lambda c, i: (i, c))),
        compiler_params=_params(("arbitrary", "arbitrary")),
        name=f"dilated_attn_g{g}",
    )(qkv, qkv, qkv)
    return o.reshape(n, gw), st.reshape(n, LANES)


def _merge_kernel(x_ref, yr_ref, gr_ref, ga_ref, o0_ref, o1_ref, o2_ref, s0_ref, s1_ref, s2_ref,
                  wr_ref, wa_ref, wo_ref, g1_ref, b1_ref, x1_ref, x1p_ref, *, tt, d):
    t1 = jnp.dot(yr_ref[...], wr_ref[...], preferred_element_type=F32)
    merged = _sigmoid(gr_ref[...].astype(F32)) * t1

    stats = [s0_ref[...], s1_ref[...], s2_ref[...]]
    outs = [o0_ref, o1_ref, o2_ref]
    heads = []
    for h in range(HEADS_PER_GROUP):
        lse = [s[:, h:h + 1] for s in stats]
        mx = jnp.maximum(jnp.maximum(lse[0], lse[1]), lse[2])
        e = [jnp.exp(v - mx) for v in lse]
        tot = e[0] + e[1] + e[2]
        acc = None
        for gi in range(len(ATTN_GROUPS)):
            term = (e[gi] / tot) * outs[gi][:, h * HEAD_DIM:(h + 1) * HEAD_DIM].astype(F32)
            acc = term if acc is None else acc + term
        heads.append(acc)
    oa = jnp.concatenate(heads, axis=1).astype(BF16)
    t2 = jnp.dot(oa, wa_ref[...], preferred_element_type=F32)
    merged = merged + _sigmoid(ga_ref[...].astype(F32)) * t2
    mix = jnp.dot(merged.astype(BF16), wo_ref[...], preferred_element_type=F32)

    v = DN_ALPHA * _time_major_rows(x_ref, tt, d) + mix
    mu = jnp.mean(v, axis=-1, keepdims=True)
    cen = v - mu
    var = jnp.mean(cen * cen, axis=-1, keepdims=True)
    x1 = cen * lax.rsqrt(var + LN_EPS) * g1_ref[...] + b1_ref[...]
    x1_ref[...] = x1
    _store_packed(x1p_ref, 0, x1)


def _merge(xv, yr, z, os_, sts, wr, wa, wo, g1, b1, n, d, col_gr, col_ga):
    tm = _pick(n, (256, 128, 64))
    tt = tm // SUBLANES
    gw = HEADS_PER_GROUP * HEAD_DIM
    rpt = _rows_per_token(d)
    const = lambda shape: pl.BlockSpec(shape, lambda i: (0, 0), pipeline_mode=pl.Buffered(1))
    row = lambda w: pl.BlockSpec((tm, w), lambda i: (i, 0))
    return pl.pallas_call(
        functools.partial(_merge_kernel, tt=tt, d=d),
        out_shape=(jax.ShapeDtypeStruct((n, d), F32), jax.ShapeDtypeStruct((n * rpt, LANES), U32)),
        grid=(n // tm,),
        in_specs=[
            pl.BlockSpec((SUBLANES, tt * d), lambda i: (0, i)),
            row(yr.shape[1]),
            pl.BlockSpec((tm, d), lambda i: (i, col_gr // d)),
            pl.BlockSpec((tm, d), lambda i: (i, col_ga // d)),
            row(gw), row(gw), row(gw),
            row(LANES), row(LANES), row(LANES),
            const(wr.shape), const(wa.shape), const(wo.shape),
            const((1, d)), const((1, d)),
        ],
        out_specs=(row(d), pl.BlockSpec((tm * rpt, LANES), lambda i: (i, 0))),
        compiler_params=_params(("arbitrary",)),
        name="merge_ln1",
    )(xv, yr, z, z, *os_, *sts, wr, wa, wo, g1, b1)


def _router_kernel(x_ref, wr_ref, bias_ref, idx_ref, w_ref, rank_ref, cnt_ref, base_ref, tri_ref, *, t, e):
    i = pl.program_id(0)

    @pl.when(i == 0)
    def _():
        base_ref[...] = jnp.zeros_like(base_ref)
        rr = lax.broadcasted_iota(jnp.int32, (t, t), 0)
        cc = lax.broadcasted_iota(jnp.int32, (t, t), 1)
        tri_ref[...] = jnp.where(rr < cc, 1.0, 0.0).astype(BF16)

    nt = (((1,), (1,)), ((), ()))
    logits = lax.dot_general(wr_ref[...], x_ref[...].astype(BF16), nt, preferred_element_type=F32)
    scores = _sigmoid(logits)
    biased = scores + bias_ref[...]

    per = e // N_GROUP
    sub = lax.broadcasted_iota(jnp.int32, (per, t), 0).astype(F32)
    blocks, gscore = [], []
    for g in range(N_GROUP):
        blk = biased[g * per:(g + 1) * per, :]
        m1 = jnp.max(blk, axis=0, keepdims=True)
        first = jnp.min(jnp.where(blk == m1, sub, float(per)), axis=0, keepdims=True)
        m2 = jnp.max(jnp.where(sub == first, -jnp.inf, blk), axis=0, keepdims=True)
        blocks.append(blk)
        gscore.append(m1 + m2)
    masked_blocks = []
    for g in range(N_GROUP):
        beaten = jnp.zeros((1, t), F32)
        for g2 in range(N_GROUP):
            if g2 == g:
                continue
            wins = (gscore[g2] > gscore[g]) | ((gscore[g2] == gscore[g]) & (g2 < g))
            beaten = beaten + jnp.where(wins, 1.0, 0.0)
        keep = beaten < float(TOPK_GROUP)
        masked_blocks.append(jnp.where(keep, blocks[g], -jnp.inf))
    masked = jnp.concatenate(masked_blocks, axis=0)

    eidx = lax.broadcasted_iota(jnp.int32, (e, t), 0).astype(F32)
    member = jnp.zeros((e, t), F32)
    sels, ws = [], []
    for _ in range(TOP_K):
        m = jnp.max(masked, axis=0, keepdims=True)
        sel = jnp.min(jnp.where(masked == m, eidx, float(e)), axis=0, keepdims=True)
        hit = eidx == sel
        ws.append(jnp.sum(jnp.where(hit, scores, 0.0), axis=0, keepdims=True))
        masked = jnp.where(hit, -jnp.inf, masked)
        member = jnp.where(hit, 1.0, member)
        sels.append(sel)
    wsum = ws[0]
    for k in range(1, TOP_K):
        wsum = wsum + ws[k]

    before = jnp.dot(member.astype(BF16), tri_ref[...], preferred_element_type=F32) + base_ref[...]
    for k in range(TOP_K):
        idx_ref[k:k + 1, :] = sels[k].astype(jnp.int32)
        w_ref[k:k + 1, :] = ws[k] / wsum * ROUTED_SCALE
        rk = jnp.sum(jnp.where(eidx == sels[k], before, 0.0), axis=0, keepdims=True)
        rank_ref[k:k + 1, :] = rk.astype(jnp.int32)
    base_ref[...] = base_ref[...] + jnp.sum(member, axis=1, keepdims=True)
    cnt_ref[...] = jnp.broadcast_to(base_ref[...], cnt_ref.shape)


def _router(x1, wrt, bias, n, d):
    e = wrt.shape[0]
    t = _pick(n, (512, 256, 128))
    tok = lambda dt: jax.ShapeDtypeStruct((TOP_K, n), dt)
    return pl.pallas_call(
        functools.partial(_router_kernel, t=t, e=e),
        out_shape=(tok(jnp.int32), tok(F32), tok(jnp.int32), jax.ShapeDtypeStruct((e, LANES), F32)),
        grid=(n // t,),
        in_specs=[
            pl.BlockSpec((t, d), lambda i: (i, 0)),
            pl.BlockSpec((e, d), lambda i: (0, 0)),
            pl.BlockSpec((e, 1), lambda i: (0, 0)),
        ],
        out_specs=(pl.BlockSpec((TOP_K, t), lambda i: (0, i)),
                   pl.BlockSpec((TOP_K, t), lambda i: (0, i)),
                   pl.BlockSpec((TOP_K, t), lambda i: (0, i)),
                   pl.BlockSpec((e, LANES), lambda i: (0, 0))),
        scratch_shapes=[pltpu.VMEM((e, 1), F32), pltpu.VMEM((t, t), BF16)],
        compiler_params=_params(("arbitrary",)),
        name="router_topk",
    )(x1, wrt, bias)


def _dest_kernel(idx_ref, rank_ref, start_ref, dest_ref, *, t, e):
    eidx = lax.broadcasted_iota(jnp.int32, (e, t), 0)
    for k in range(TOP_K):
        hit = eidx == idx_ref[k:k + 1, :]
        st = jnp.sum(jnp.where(hit, start_ref[...], 0.0), axis=0, keepdims=True)
        dest_ref[0, k:k + 1, :] = st.astype(jnp.int32) + rank_ref[k:k + 1, :]


def _dest(idx, rank, start_col, n, t):
    e = start_col.shape[0]
    return pl.pallas_call(
        functools.partial(_dest_kernel, t=t, e=e),
        out_shape=jax.ShapeDtypeStruct((n // t, TOP_K, t), jnp.int32),
        grid=(n // t,),
        in_specs=[
            pl.BlockSpec((TOP_K, t), lambda i: (0, i)),
            pl.BlockSpec((TOP_K, t), lambda i: (0, i)),
            pl.BlockSpec((e, 1), lambda i: (0, 0)),
        ],
        out_specs=pl.BlockSpec((1, TOP_K, t), lambda i: (i, 0, 0)),
        compiler_params=_params(("arbitrary",)),
        name="moe_dest",
    )(idx, rank, start_col)


def _token_copy(src_ref, src_tok, dst_ref, dst_tok, sem, rpt):
    s0 = pl.multiple_of(src_tok * rpt, rpt)
    d0 = pl.multiple_of(dst_tok * rpt, rpt)
    return pltpu.make_async_copy(src_ref.at[pl.ds(s0, rpt), :], dst_ref.at[pl.ds(d0, rpt), :], sem)


TOKENS_PER_SLOT_ROW = LANES // TOP_K


def _for_each_token_slots(table, t, fn):
    def body(r, c):
        for pair in range(0, TOKENS_PER_SLOT_ROW, 2):
            slots = [[table[r, (pair + p) * TOP_K + k] for k in range(TOP_K)] for p in range(2)]
            for p in range(2):
                fn(r * TOKENS_PER_SLOT_ROW + pair + p, slots[p])
        return c

    lax.fori_loop(0, t // TOKENS_PER_SLOT_ROW, body, 0)


def _slot_table_copy(dest_hbm, step, dsm, slot, sem_idx):
    return pltpu.make_async_copy(dest_hbm.at[step], dsm.at[slot], sem_idx.at[slot])


def _dispatch_kernel(dest_hbm, x_ref, wsg_ref, wsu_ref, wsd_ref, xs_hbm, sh_ref, dsm, zero_ref, sem_idx, sem,
                     sem_pad, *, t, rpt, n_steps, pad_rows):
    i = pl.program_id(0)
    slot = i % 2

    @pl.when(i == 0)
    def _():
        _slot_table_copy(dest_hbm, 0, dsm, 0, sem_idx).start()
        zero_ref[...] = jnp.zeros_like(zero_ref)
        pad = pltpu.make_async_copy(zero_ref, xs_hbm.at[pl.ds(n_steps * t * TOP_K * rpt, pad_rows), :], sem_pad)
        pad.start()
        pad.wait()

    _slot_table_copy(dest_hbm, i, dsm, slot, sem_idx).wait()

    @pl.when(i + 1 < n_steps)
    def _():
        _slot_table_copy(dest_hbm, i + 1, dsm, 1 - slot, sem_idx).start()

    def issue(tl, slots):
        for k in range(TOP_K):
            _token_copy(x_ref, tl, xs_hbm, slots[k], sem, rpt).start(priority=k % 2)

    _for_each_token_slots(dsm.at[slot], t, issue)

    x = _unpack_words([x_ref[_token_rows(0, t, rpt, j), :] for j in range(rpt)]).astype(BF16)
    g = jnp.dot(x, wsg_ref[...], preferred_element_type=F32)
    u = jnp.dot(x, wsu_ref[...], preferred_element_type=F32)
    h = (g * _sigmoid(g)) * u
    sh_ref[...] = jnp.dot(h.astype(BF16), wsd_ref[...], preferred_element_type=F32).astype(sh_ref.dtype)

    for _ in range(TOP_K):
        pltpu.make_async_copy(x_ref, xs_hbm.at[pl.ds(0, t * rpt), :], sem).wait()


def _dispatch(dest3, x1p, wsg, wsu, wsd, n, d, t, pad_tokens):
    rpt = _rows_per_token(d)
    const = lambda shape: pl.BlockSpec(shape, lambda i: (0, 0), pipeline_mode=pl.Buffered(1))
    return pl.pallas_call(
        functools.partial(_dispatch_kernel, t=t, rpt=rpt, n_steps=n // t, pad_rows=pad_tokens * rpt),
        out_shape=(jax.ShapeDtypeStruct(((n * TOP_K + pad_tokens) * rpt, LANES), U32),
                   jax.ShapeDtypeStruct((n, d), BF16)),
        grid=(n // t,),
        in_specs=[pl.BlockSpec(memory_space=pl.ANY), pl.BlockSpec((t * rpt, LANES), lambda i: (i, 0)),
                  const(wsg.shape), const(wsu.shape), const(wsd.shape)],
        out_specs=(pl.BlockSpec(memory_space=pl.ANY), pl.BlockSpec((t, d), lambda i: (i, 0))),
        scratch_shapes=[pltpu.SMEM((2, TOP_K * t // LANES, LANES), jnp.int32),
                        pltpu.VMEM((pad_tokens * rpt, LANES), U32),
                        pltpu.SemaphoreType.DMA((2,)), pltpu.SemaphoreType.DMA, pltpu.SemaphoreType.DMA],
        compiler_params=_params(("arbitrary",)),
        name="moe_dispatch",
    )(dest3, x1p, wsg, wsu, wsd)


def _expert_kernel(r0_ref, we_ref, nw_ref, nx_ref, par_ref, xs_hbm, wg_hbm, wu_hbm, wd_hbm, o_hbm,
                   xin_ref, out_ref, wgf_ref, wuf_ref, wdf_ref, wgb_ref, wub_ref, wdb_ref,
                   wsem, isem, osem, *, tm, rpt):
    w = pl.program_id(0)
    nw = nw_ref[0]
    rows = tm * rpt

    def weight_copies(expert, buf):
        return [pltpu.make_async_copy(src.at[expert], dst.at[buf], wsem.at[buf])
                for src, dst in ((wg_hbm, wgf_ref), (wu_hbm, wuf_ref), (wd_hbm, wdf_ref))]

    def rows_in(item, buf):
        r0 = pl.multiple_of(r0_ref[item] * rpt, rpt)
        return pltpu.make_async_copy(xs_hbm.at[pl.ds(r0, rows), :], xin_ref.at[buf], isem.at[buf])

    def rows_out(item, buf):
        r0 = pl.multiple_of(r0_ref[item] * rpt, rpt)
        return pltpu.make_async_copy(out_ref.at[buf], o_hbm.at[pl.ds(r0, rows), :], osem.at[buf])

    @pl.when(w < nw)
    def _():
        e = we_ref[w]
        pw = jnp.maximum(w - 1, 0)
        buf_io = w % 2

        @pl.when(w == 0)
        def _():
            rows_in(0, 0).start()

        @pl.when(w + 1 < nw)
        def _():
            rows_in(w + 1, 1 - buf_io).start()

        @pl.when((w == 0) | (e != we_ref[pw]))
        def _():
            buf = par_ref[w]
            nxt = nx_ref[w]

            @pl.when(w == 0)
            def _():
                for c in weight_copies(e, buf):
                    c.start()

            @pl.when(nxt >= 0)
            def _():
                for c in weight_copies(nxt, 1 - buf):
                    c.start()

            for c in weight_copies(e, buf):
                c.wait()
            wgb_ref[...] = wgf_ref[buf].astype(BF16)
            wub_ref[...] = wuf_ref[buf].astype(BF16)
            wdb_ref[...] = wdf_ref[buf].astype(BF16)

        rows_in(w, buf_io).wait()
        xin = xin_ref.at[buf_io]
        x = _unpack_words([xin[_token_rows(0, tm, rpt, j), :] for j in range(rpt)]).astype(BF16)
        g = jnp.dot(x, wgb_ref[...], preferred_element_type=F32)
        u = jnp.dot(x, wub_ref[...], preferred_element_type=F32)
        h = (g * _sigmoid(g)) * u
        o = jnp.dot(h.astype(BF16), wdb_ref[...], preferred_element_type=F32)

        _store_packed(out_ref.at[buf_io], 0, o)

        @pl.when(w > 0)
        def _():
            rows_out(pw, 1 - buf_io).wait()

        rows_out(w, buf_io).start()

        @pl.when(w == nw - 1)
        def _():
            rows_out(w, buf_io).wait()


def _experts(work, xs, w_gate, w_up, w_down, tm):
    _, d, de = w_gate.shape
    rpt = _rows_per_token(d)
    n_work = work[0].shape[0]
    hbm = pl.BlockSpec(memory_space=pl.ANY)
    return pl.pallas_call(
        functools.partial(_expert_kernel, tm=tm, rpt=rpt),
        out_shape=jax.ShapeDtypeStruct(xs.shape, U32),
        grid_spec=pltpu.PrefetchScalarGridSpec(
            num_scalar_prefetch=len(work),
            grid=(n_work,),
            in_specs=[hbm, hbm, hbm, hbm],
            out_specs=hbm,
            scratch_shapes=[pltpu.VMEM((2, tm * rpt, LANES), U32), pltpu.VMEM((2, tm * rpt, LANES), U32),
                            pltpu.VMEM((2, d, de), F32), pltpu.VMEM((2, d, de), F32), pltpu.VMEM((2, de, d), F32),
                            pltpu.VMEM((d, de), BF16), pltpu.VMEM((d, de), BF16), pltpu.VMEM((de, d), BF16),
                            pltpu.SemaphoreType.DMA((2,)), pltpu.SemaphoreType.DMA((2,)),
                            pltpu.SemaphoreType.DMA((2,))],
        ),
        compiler_params=_params(("arbitrary",)),
        name="moe_experts",
    )(*work, xs, w_gate, w_up, w_down)


def _combine_kernel(dest_hbm, x1_ref, wt_ref, sh_ref, o_hbm, g2_ref, b2_ref, y_ref,
                    dsm, gbuf, sem_idx, sem, *, t, tt, d, rpt, n_steps):
    i = pl.program_id(0)
    slot = i % 2

    def issue(step_slot):
        def gather(tl, slots):
            for k in range(TOP_K):
                _token_copy(o_hbm, slots[k], gbuf.at[step_slot], k * t + tl, sem.at[step_slot],
                            rpt).start(priority=k % 2)

        _for_each_token_slots(dsm.at[step_slot], t, gather)

    @pl.when(i == 0)
    def _():
        first = _slot_table_copy(dest_hbm, 0, dsm, 0, sem_idx)
        first.start()
        first.wait()
        issue(0)
        if n_steps > 1:
            _slot_table_copy(dest_hbm, 1, dsm, 1, sem_idx).start()

    @pl.when(i + 1 < n_steps)
    def _():
        _slot_table_copy(dest_hbm, i + 1, dsm, 1 - slot, sem_idx).wait()
        issue(1 - slot)

        @pl.when(i + 2 < n_steps)
        def _():
            _slot_table_copy(dest_hbm, i + 2, dsm, slot, sem_idx).start()

    gcur = gbuf.at[slot]
    pltpu.make_async_copy(o_hbm.at[pl.ds(0, TOP_K * t * rpt), :], gcur, sem.at[slot]).wait()

    acc_lo = [None] * rpt
    acc_hi = [None] * rpt
    for k in range(TOP_K):
        wk = wt_ref[:, k:k + 1]
        for j in range(rpt):
            words = gcur[_token_rows(k * t * rpt, t, rpt, j), :]
            lo = wk * pltpu.bitcast(words << 16, F32)
            hi = wk * pltpu.bitcast(words & U32(HI_MASK), F32)
            acc_lo[j] = lo if k == 0 else acc_lo[j] + lo
            acc_hi[j] = hi if k == 0 else acc_hi[j] + hi
    routed = jnp.concatenate(acc_lo + acc_hi, axis=1)

    v = DN_ALPHA * x1_ref[...] + (routed + sh_ref[...].astype(F32))
    mu = jnp.mean(v, axis=-1, keepdims=True)
    cen = v - mu
    var = jnp.mean(cen * cen, axis=-1, keepdims=True)
    y = cen * lax.rsqrt(var + LN_EPS) * g2_ref[...] + b2_ref[...]
    for s in range(tt):
        y_ref[:, s * d:(s + 1) * d] = y[s * SUBLANES:(s + 1) * SUBLANES, :]


def _combine(dest3, x1, wt, sh, o, g2, b2, n, d, t):
    tt = t // SUBLANES
    rpt = _rows_per_token(d)
    const = lambda shape: pl.BlockSpec(shape, lambda i: (0, 0), pipeline_mode=pl.Buffered(1))
    return pl.pallas_call(
        functools.partial(_combine_kernel, t=t, tt=tt, d=d, rpt=rpt, n_steps=n // t),
        out_shape=jax.ShapeDtypeStruct((SUBLANES, (n // SUBLANES) * d), F32),
        grid=(n // t,),
        in_specs=[
            pl.BlockSpec(memory_space=pl.ANY),
            pl.BlockSpec((t, d), lambda i: (i, 0)),
            pl.BlockSpec((t, TOP_K), lambda i: (i, 0)),
            pl.BlockSpec((t, d), lambda i: (i, 0)),
            pl.BlockSpec(memory_space=pl.ANY),
            const((1, d)), const((1, d)),
        ],
        out_specs=pl.BlockSpec((SUBLANES, tt * d), lambda i: (0, i)),
        scratch_shapes=[pltpu.SMEM((2, TOP_K * t // LANES, LANES), jnp.int32), pltpu.VMEM((2, TOP_K * t * rpt, LANES), U32),
                        pltpu.SemaphoreType.DMA((2,)), pltpu.SemaphoreType.DMA((2,))],
        compiler_params=_params(("arbitrary",)),
        name="moe_combine_ln2",
    )(dest3, x1, wt, sh, o, g2, b2)


def _work_items(counts, m, tm):
    e = counts.shape[0]
    end = jnp.cumsum(counts)
    start = end - counts
    nchunk = (counts + tm - 1) // tm
    cum = jnp.cumsum(nchunk)
    off = cum - nchunk
    n_work = m // tm + e
    w = jnp.arange(n_work, dtype=jnp.int32)
    we = jnp.minimum(jnp.sum(cum[None, :] <= w[:, None], axis=1), e - 1).astype(jnp.int32)
    row0 = (start[we] + (w - off[we]) * tm).astype(jnp.int32)
    nw = cum[-1].astype(jnp.int32)
    row0 = jnp.where(w < nw, row0, 0)
    ids = jnp.arange(e, dtype=jnp.int32)
    later = lax.cummin(jnp.where(counts > 0, ids, e)[::-1])[::-1]
    nxt_of = jnp.concatenate([later[1:], jnp.full((1,), e, jnp.int32)])
    nxt_of = jnp.where(nxt_of >= e, -1, nxt_of).astype(jnp.int32)
    ordinal = jnp.cumsum((counts > 0).astype(jnp.int32)) - 1
    work = (row0, we, nw.reshape(1), nxt_of[we], (ordinal[we] % 2).astype(jnp.int32))
    return work, start.astype(jnp.int32)


def kernel(x, w_in, b_in, conv_w, conv_b, lru_wa, lru_ba, lru_wx, lru_bx, lru_lambda, w_rnn_br, w_attn_br, w_out, ln1_g, ln1_b, w_router, router_bias, w_gate, w_up, w_down, ws_gate, ws_up, ws_down, ln2_g, ln2_b):
    bsz, s, d = x.shape
    assert bsz == SUBLANES, "time-major rows need batch == 8"
    assert DEPTH == 1 and w_in.shape[0] == 1
    n = bsz * s
    c = lru_lambda.shape[1]
    a = HEADS_PER_GROUP * len(ATTN_GROUPS) * HEAD_DIM
    assert c == d and w_in.shape[2] == 2 * c + 3 * a + 2 * d
    layer = 0

    gw = HEADS_PER_GROUP * HEAD_DIM
    src = (0, c, 2 * c, 2 * c + a, 2 * c + 2 * a, 2 * c + 3 * a, 2 * c + 3 * a + d, 2 * c + 3 * a + 2 * d)
    part = lambda v, p, lo=0, hi=None: v[..., src[p]:src[p + 1]][..., lo:hi]
    cat = lambda v, pieces: jnp.concatenate([part(v, *p) for p in pieces], axis=-1)
    main = ((0,), (1,), (5,), (6,))
    w_main = cat(w_in[layer], main).astype(BF16)
    b_main = cat(b_in[layer], main).reshape(1, -1)
    col_u, col_g, col_gr, col_ga = 0, c, 2 * c, 2 * c + d
    row = lambda v: v.reshape(1, -1)

    xv = x.reshape(bsz, s * d)
    z, xb = _in_proj(xv, w_main, b_main, n, d)

    yr = _lru(z, conv_w[layer], row(conv_b[layer]), lru_wa[layer].astype(BF16), row(lru_ba[layer]),
              lru_wx[layer].astype(BF16), row(lru_bx[layer]), row(lru_lambda[layer]), n, c, col_u, col_g)

    os_, sts = [], []
    for g in range(len(ATTN_GROUPS)):
        cols = tuple((p, g * gw, (g + 1) * gw) for p in (2, 3, 4))
        qkv = _qkv_proj(xb, cat(w_in[layer], cols).astype(BF16), cat(b_in[layer], cols).reshape(1, -1), g, n, d)
        o, st = _attn_group(qkv, g, n)
        os_.append(o)
        sts.append(st)

    x1, x1p = _merge(xv, yr, z, os_, sts, w_rnn_br[layer].astype(BF16), w_attn_br[layer].astype(BF16),
                     w_out[layer].astype(BF16), row(ln1_g[layer]), row(ln1_b[layer]), n, d, col_gr, col_ga)

    n_exp = w_router.shape[2]
    idx, wts, rank, cnt = _router(x1, w_router[layer].T.astype(BF16), router_bias[layer].reshape(n_exp, 1), n, d)

    tm = 256
    t_tok = _pick(n, (256, 128))
    m = n * TOP_K
    counts = cnt[:, 0].astype(jnp.int32)
    work, start = _work_items(counts, m, tm)
    dest3 = _dest(idx, rank, start.astype(F32).reshape(n_exp, 1), n, t_tok)
    dest3 = dest3.transpose(0, 2, 1).reshape(n // t_tok, t_tok * TOP_K // LANES, LANES)

    xs, shared = _dispatch(dest3, x1p, ws_gate[layer].astype(BF16), ws_up[layer].astype(BF16),
                           ws_down[layer].astype(BF16), n, d, t_tok, pad_tokens=tm)
    o = _experts(work, xs, w_gate[layer], w_up[layer], w_down[layer], tm)
    y = _combine(dest3, x1, wts.T, shared, o, row(ln2_g[layer]), row(ln2_b[layer]), n, d, t_tok)
    return y.reshape(bsz, s, d)
```

```python
import functools

import jax
import jax.numpy as jnp
from jax import lax
from jax.experimental import pallas as pl
from jax.experimental.pallas import tpu as pltpu

HEAD_DIM = 128
ATTN_GROUPS = ((128, 1), (512, 4), (2048, 16))
HEADS_PER_GROUP = 4
CONV_WIDTH = 4
LRU_C = 8.0
N_GROUP = 8
TOPK_GROUP = 4
TOP_K = 8
ROUTED_SCALE = 2.5
DEPTH = 1
DN_ALPHA = (2 * DEPTH) ** 0.25
LN_EPS = 1e-5

SUBLANES = 8
LANES = 128
V7X_VMEM_LIMIT_BYTES = 56 * 1024 * 1024

F32 = jnp.float32
BF16 = jnp.bfloat16


def _pick(n, cands):
    for c in cands:
        if n % c == 0:
            return c
    raise ValueError(f"no tile in {cands} divides {n}")


def _sigmoid(x):
    return 1.0 / (1.0 + jnp.exp(-x))


def _params(sem, vmem=V7X_VMEM_LIMIT_BYTES):
    return pltpu.CompilerParams(dimension_semantics=sem, vmem_limit_bytes=vmem)


def _time_major_rows(x_ref, slab_ref, tt, d):
    for b in range(SUBLANES):
        for j in range(d // LANES):
            slab_ref[j, pl.ds(b, tt, stride=SUBLANES), :] = x_ref[b, :, j * LANES:(j + 1) * LANES]
    return jnp.concatenate([slab_ref[j] for j in range(d // LANES)], axis=1)


def _store_batch_major(y_ref, slab_ref, y, tt, d):
    for j in range(d // LANES):
        slab_ref[j] = y[:, j * LANES:(j + 1) * LANES]
    for b in range(SUBLANES):
        for j in range(d // LANES):
            y_ref[b, :, j * LANES:(j + 1) * LANES] = slab_ref[j, pl.ds(b, tt, stride=SUBLANES), :]


U32 = jnp.uint32
HI_MASK = 0xFFFF0000


def _rows_per_token(d):
    assert d % (2 * LANES) == 0
    return d // (2 * LANES)


def _token_rows(base, m, rpt, j):
    return pl.ds(base + j, m, stride=rpt)


def _pack_words(x):
    half = x.shape[1] // 2
    bits = pltpu.bitcast(x.astype(BF16).astype(F32), U32)
    words = (bits[:, :half] >> 16) | (bits[:, half:] & U32(HI_MASK))
    return [words[:, j * LANES:(j + 1) * LANES] for j in range(half // LANES)]


def _store_packed(ref, base, x):
    chunks = _pack_words(x)
    for j, chunk in enumerate(chunks):
        ref[_token_rows(base, x.shape[0], len(chunks), j), :] = chunk


def _unpack_words(chunks):
    lo = [pltpu.bitcast(c << 16, F32) for c in chunks]
    hi = [pltpu.bitcast(c & U32(HI_MASK), F32) for c in chunks]
    return jnp.concatenate(lo + hi, axis=1)


def _in_proj_kernel(x_ref, w_ref, b_ref, z_ref, xb_ref, slab_ref, *, tt, d):
    @pl.when(pl.program_id(1) == 0)
    def _():
        for b in range(SUBLANES):
            for j in range(d // LANES):
                slab_ref[j, pl.ds(b, tt, stride=SUBLANES), :] = x_ref[b, :, j * LANES:(j + 1) * LANES]
        for j in range(d // LANES):
            xb_ref[:, j * LANES:(j + 1) * LANES] = slab_ref[j].astype(BF16)

    z = jnp.dot(xb_ref[...], w_ref[...], preferred_element_type=F32) + b_ref[...]
    z_ref[...] = z.astype(z_ref.dtype)


def _in_proj(x, w, b, n, d):
    d_out = w.shape[1]
    tm = _pick(n, (1024, 512, 256, 128, 64, 32, 16))
    tn = _pick(d_out, (1024, 512, 256, 128))
    tt = tm // SUBLANES
    return pl.pallas_call(
        functools.partial(_in_proj_kernel, tt=tt, d=d),
        out_shape=(jax.ShapeDtypeStruct((n, d_out), BF16), jax.ShapeDtypeStruct((n, d), BF16)),
        grid=(n // tm, d_out // tn),
        in_specs=[
            pl.BlockSpec((SUBLANES, tt, d), lambda i, j: (0, i, 0)),
            pl.BlockSpec((d, tn), lambda i, j: (0, j)),
            pl.BlockSpec((1, tn), lambda i, j: (0, j)),
        ],
        out_specs=(pl.BlockSpec((tm, tn), lambda i, j: (i, j)),
                   pl.BlockSpec((tm, d), lambda i, j: (i, 0))),
        scratch_shapes=[pltpu.VMEM((d // LANES, tm, LANES), F32)],
        compiler_params=_params(("arbitrary", "arbitrary")),
        name="in_proj",
    )(x, w, b)


BF16_SUBLANES = 16


def _qkv_kernel(xb_ref, w_ref, b_ref, p_ref, o_ref, *, stride):
    r = (jnp.dot(xb_ref[...], w_ref[...], preferred_element_type=F32) + b_ref[...]).astype(BF16)
    sb = p_ref.shape[0]
    for s in range(r.shape[0] // (2 * sb)):
        halves = [jnp.dot(p_ref[...], r[(2 * s + i) * sb:(2 * s + i + 1) * sb, :], preferred_element_type=F32)
                  for i in range(2)]
        for c in range(stride):
            tile = jnp.concatenate([hv[c * SUBLANES:(c + 1) * SUBLANES, :] for hv in halves], axis=0)
            o_ref[c, s * BF16_SUBLANES:(s + 1) * BF16_SUBLANES, :] = tile.astype(BF16)


def _qkv_proj(xb, w, b, g, n, d):
    _, dil = ATTN_GROUPS[g]
    stride = SUBLANES * dil
    gw = HEADS_PER_GROUP * HEAD_DIM
    sb = SUBLANES * stride
    tm = max(_pick(n, (1024, 512, 256)), 2 * sb)
    per = tm // stride
    new = jnp.arange(sb, dtype=jnp.int32)
    old = (new % SUBLANES) * stride + new // SUBLANES
    perm = (old[:, None] == jnp.arange(sb, dtype=jnp.int32)[None, :]).astype(BF16)
    return pl.pallas_call(
        functools.partial(_qkv_kernel, stride=stride),
        out_shape=jax.ShapeDtypeStruct((stride, n // stride, 3 * gw), BF16),
        grid=(n // tm, 3),
        in_specs=[
            pl.BlockSpec((tm, d), lambda i, j: (i, 0)),
            pl.BlockSpec((d, gw), lambda i, j: (0, j)),
            pl.BlockSpec((1, gw), lambda i, j: (0, j)),
            pl.BlockSpec((sb, sb), lambda i, j: (0, 0), pipeline_mode=pl.Buffered(1)),
        ],
        out_specs=pl.BlockSpec((stride, per, gw), lambda i, j: (0, i, j)),
        compiler_params=_params(("arbitrary", "arbitrary")),
        name=f"qkv_proj_g{g}",
    )(xb, w, b, perm)


def _lru_kernel(u_ref, g_ref, cw_ref, cb_ref, wa_ref, ba_ref, wx_ref, bx_ref, lam_ref, y_ref,
                carry_ref, h_ref, a_ref, b_ref, *, tt, nblk, bw):
    ti = pl.program_id(1)
    rows = tt * SUBLANES
    halo = SUBLANES * (CONV_WIDTH - 1)

    @pl.when(ti == 0)
    def _():
        carry_ref[...] = jnp.zeros_like(carry_ref)
        h_ref[...] = jnp.zeros_like(h_ref)

    u = u_ref[...].astype(F32)
    ext = jnp.concatenate([carry_ref[...], u], axis=0)
    uc = cb_ref[...] + cw_ref[0:1, :] * ext[0:rows, :]
    for j in range(1, CONV_WIDTH):
        uc = uc + cw_ref[j:j + 1, :] * ext[SUBLANES * j:SUBLANES * j + rows, :]
    carry_ref[...] = u[rows - halo:, :]

    ucb = uc.astype(BF16)
    r_parts, i_parts = [], []
    for kb in range(nblk):
        blk = ucb[:, kb * bw:(kb + 1) * bw]
        r_parts.append(jnp.dot(blk, wa_ref[kb], preferred_element_type=F32))
        i_parts.append(jnp.dot(blk, wx_ref[kb], preferred_element_type=F32))
    r = _sigmoid(jnp.concatenate(r_parts, axis=1) + ba_ref[...])
    ig = _sigmoid(jnp.concatenate(i_parts, axis=1) + bx_ref[...])

    nl = -lam_ref[...]
    softplus = jnp.maximum(nl, 0.0) + jnp.log(1.0 + jnp.exp(-jnp.abs(nl)))
    a = jnp.exp((-LRU_C) * r * softplus)
    gap = 1.0 - a * a
    mult = jnp.where(gap > 0.0, gap * lax.rsqrt(gap), 0.0)
    gated = uc * ig
    a_ref[...] = a
    b_ref[...] = gated * mult

    @pl.when(ti == 0)
    def _():
        b_ref[0:SUBLANES, :] = gated[0:SUBLANES, :]

    def step(t, h):
        r0 = pl.multiple_of(t * SUBLANES, SUBLANES)
        h = a_ref[pl.ds(r0, SUBLANES), :] * h + b_ref[pl.ds(r0, SUBLANES), :]
        b_ref[pl.ds(r0, SUBLANES), :] = h
        return h

    h_ref[...] = lax.fori_loop(0, tt, step, h_ref[...], unroll=8)

    g = g_ref[...].astype(F32)
    gelu = 0.5 * g * (1.0 + jnp.tanh(0.7978845608028654 * (g + 0.044715 * (g * g * g))))
    y_ref[...] = (b_ref[...] * gelu).astype(y_ref.dtype)


def _lru(z, conv_w, conv_b, wa, ba, wx, bx, lam, n, c, col_u, col_g):
    nblk_total, bw, _ = wa.shape
    ct = _pick(c, (512, 256, 128))
    tt = _pick(n // SUBLANES, (128, 64, 32, 16, 8))
    rows = tt * SUBLANES
    nblk = ct // bw
    vec = lambda: pl.BlockSpec((1, ct), lambda ci, ti: (0, ci))
    return pl.pallas_call(
        functools.partial(_lru_kernel, tt=tt, nblk=nblk, bw=bw),
        out_shape=jax.ShapeDtypeStruct((n, c), BF16),
        grid=(c // ct, n // rows),
        in_specs=[
            pl.BlockSpec((rows, ct), lambda ci, ti: (ti, col_u // ct + ci)),
            pl.BlockSpec((rows, ct), lambda ci, ti: (ti, col_g // ct + ci)),
            pl.BlockSpec((CONV_WIDTH, ct), lambda ci, ti: (0, ci)),
            vec(),
            pl.BlockSpec((nblk, bw, bw), lambda ci, ti: (ci, 0, 0)),
            vec(),
            pl.BlockSpec((nblk, bw, bw), lambda ci, ti: (ci, 0, 0)),
            vec(),
            vec(),
        ],
        out_specs=pl.BlockSpec((rows, ct), lambda ci, ti: (ti, ci)),
        scratch_shapes=[
            pltpu.VMEM((SUBLANES * (CONV_WIDTH - 1), ct), F32),
            pltpu.VMEM((SUBLANES, ct), F32),
            pltpu.VMEM((rows, ct), F32),
            pltpu.VMEM((rows, ct), F32),
        ],
        compiler_params=_params(("arbitrary", "arbitrary")),
        name="rg_lru",
    )(z, z, conv_w, conv_b, wa, ba, wx, bx, lam)


def _attn_kernel(q_ref, k_ref, v_ref, o_ref, st_ref, *, band, dil, slopes, qblocks):
    qi = lax.broadcasted_iota(jnp.int32, (band, 2 * band), 0)
    kj = lax.broadcasted_iota(jnp.int32, (band, 2 * band), 1)
    rel = qi - kj
    scale = HEAD_DIM ** -0.5
    nt = (((1,), (1,)), ((), ()))
    lane = lax.broadcasted_iota(jnp.int32, (band, LANES), 1)

    for qq in range(qblocks):
        n = pl.program_id(1) * qblocks + qq
        win0 = pl.multiple_of(jnp.maximum(n - 1, 0) * band, band)
        own = jnp.where(n > 0, band, 0)
        rows = slice(qq * band, (qq + 1) * band)
        q = q_ref[rows, :]
        kw = k_ref[pl.ds(win0, 2 * band), :]
        vw = v_ref[pl.ds(win0, 2 * band), :]
        dist_i = rel + own
        valid = (dist_i >= 0) & (dist_i <= band)
        dist = dist_i.astype(F32)

        outs = []
        stat = jnp.zeros((band, LANES), F32)
        for h in range(HEADS_PER_GROUP):
            sl = slice(h * HEAD_DIM, (h + 1) * HEAD_DIM)
            bias = slopes[h] * dil
            s = lax.dot_general(q[:, sl], kw[:, sl], nt, preferred_element_type=F32)
            s = jnp.where(valid, s * scale - bias * dist, -jnp.inf)
            m = jnp.max(s, axis=1, keepdims=True)
            p = jnp.exp(s - m)
            l = jnp.sum(p, axis=1, keepdims=True)
            acc = jnp.dot(p.astype(BF16), vw[:, sl], preferred_element_type=F32)
            outs.append(acc / l)
            stat = jnp.where(lane == h, m + jnp.log(l), stat)
        o_ref[rows, :] = jnp.concatenate(outs, axis=1).astype(o_ref.dtype)
        st_ref[rows, :] = stat


def _attn_group(qkv, g, n):
    window, dil = ATTN_GROUPS[g]
    band = window // dil
    stride = SUBLANES * dil
    lp = n // stride
    gw = HEADS_PER_GROUP * HEAD_DIM
    n_heads = HEADS_PER_GROUP * len(ATTN_GROUPS)
    slopes = tuple(2.0 ** (-8.0 * (g * HEADS_PER_GROUP + h + 1) / n_heads) for h in range(HEADS_PER_GROUP))
    qblocks = _pick(lp // band, (4, 2, 1))
    tq = qblocks * band
    o, st = pl.pallas_call(
        functools.partial(_attn_kernel, band=band, dil=float(dil), slopes=slopes, qblocks=qblocks),
        out_shape=(jax.ShapeDtypeStruct((lp, stride * gw), BF16),
                   jax.ShapeDtypeStruct((lp, stride * LANES), F32)),
        grid=(stride, lp // tq),
        in_specs=[
            pl.BlockSpec((None, tq, gw), lambda c, i: (c, i, 0)),
            pl.BlockSpec((None, lp, gw), lambda c, i: (c, 0, 1)),
            pl.BlockSpec((None, lp, gw), lambda c, i: (c, 0, 2)),
        ],
        out_specs=(pl.BlockSpec((tq, gw), lambda c, i: (i, c)),
                   pl.BlockSpec((tq, LANES), lambda c, i: (i, c))),
        compiler_params=_params(("arbitrary", "arbitrary")),
        name=f"dilated_attn_g{g}",
    )(qkv, qkv, qkv)
    return o.reshape(n, gw), st.reshape(n, LANES)


def _merge_kernel(x_ref, yr_ref, gr_ref, ga_ref, o0_ref, o1_ref, o2_ref, s0_ref, s1_ref, s2_ref,
                  wr_ref, wa_ref, wo_ref, g1_ref, b1_ref, x1_ref, x1p_ref, slab_ref, *, tt, d):
    t1 = jnp.dot(yr_ref[...], wr_ref[...], preferred_element_type=F32)
    merged = _sigmoid(gr_ref[...].astype(F32)) * t1

    stats = [s0_ref[...], s1_ref[...], s2_ref[...]]
    outs = [o0_ref, o1_ref, o2_ref]
    heads = []
    for h in range(HEADS_PER_GROUP):
        lse = [s[:, h:h + 1] for s in stats]
        mx = jnp.maximum(jnp.maximum(lse[0], lse[1]), lse[2])
        e = [jnp.exp(v - mx) for v in lse]
        tot = e[0] + e[1] + e[2]
        acc = None
        for gi in range(len(ATTN_GROUPS)):
            term = (e[gi] / tot) * outs[gi][:, h * HEAD_DIM:(h + 1) * HEAD_DIM].astype(F32)
            acc = term if acc is None else acc + term
        heads.append(acc)
    oa = jnp.concatenate(heads, axis=1).astype(BF16)
    t2 = jnp.dot(oa, wa_ref[...], preferred_element_type=F32)
    merged = merged + _sigmoid(ga_ref[...].astype(F32)) * t2
    mix = jnp.dot(merged.astype(BF16), wo_ref[...], preferred_element_type=F32)

    v = DN_ALPHA * _time_major_rows(x_ref, slab_ref, tt, d) + mix
    mu = jnp.mean(v, axis=-1, keepdims=True)
    cen = v - mu
    var = jnp.mean(cen * cen, axis=-1, keepdims=True)
    x1 = cen * lax.rsqrt(var + LN_EPS) * g1_ref[...] + b1_ref[...]
    x1_ref[...] = x1
    _store_packed(x1p_ref, 0, x1)


def _merge(x, yr, z, os_, sts, wr, wa, wo, g1, b1, n, d, col_gr, col_ga):
    tm = _pick(n, (256, 128, 64))
    tt = tm // SUBLANES
    gw = HEADS_PER_GROUP * HEAD_DIM
    rpt = _rows_per_token(d)
    const = lambda shape: pl.BlockSpec(shape, lambda i: (0, 0), pipeline_mode=pl.Buffered(1))
    row = lambda w: pl.BlockSpec((tm, w), lambda i: (i, 0))
    return pl.pallas_call(
        functools.partial(_merge_kernel, tt=tt, d=d),
        out_shape=(jax.ShapeDtypeStruct((n, d), F32), jax.ShapeDtypeStruct((n * rpt, LANES), U32)),
        grid=(n // tm,),
        in_specs=[
            pl.BlockSpec((SUBLANES, tt, d), lambda i: (0, i, 0)),
            row(yr.shape[1]),
            pl.BlockSpec((tm, d), lambda i: (i, col_gr // d)),
            pl.BlockSpec((tm, d), lambda i: (i, col_ga // d)),
            row(gw), row(gw), row(gw),
            row(LANES), row(LANES), row(LANES),
            const(wr.shape), const(wa.shape), const(wo.shape),
            const((1, d)), const((1, d)),
        ],
        out_specs=(row(d), pl.BlockSpec((tm * rpt, LANES), lambda i: (i, 0))),
        scratch_shapes=[pltpu.VMEM((d // LANES, tm, LANES), F32)],
        compiler_params=_params(("arbitrary",)),
        name="merge_ln1",
    )(x, yr, z, z, *os_, *sts, wr, wa, wo, g1, b1)


def _router_kernel(x_ref, wr_ref, bias_ref, idx_ref, w_ref, rank_ref, cnt_ref, base_ref, tri_ref, *, t, e):
    i = pl.program_id(0)

    @pl.when(i == 0)
    def _():
        base_ref[...] = jnp.zeros_like(base_ref)
        rr = lax.broadcasted_iota(jnp.int32, (t, t), 0)
        cc = lax.broadcasted_iota(jnp.int32, (t, t), 1)
        tri_ref[...] = jnp.where(rr < cc, 1.0, 0.0).astype(BF16)

    nt = (((1,), (1,)), ((), ()))
    logits = lax.dot_general(wr_ref[...], x_ref[...].astype(BF16), nt, preferred_element_type=F32)
    scores = _sigmoid(logits)
    biased = scores + bias_ref[...]

    per = e // N_GROUP
    sub = lax.broadcasted_iota(jnp.int32, (per, t), 0).astype(F32)
    blocks, gscore = [], []
    for g in range(N_GROUP):
        blk = biased[g * per:(g + 1) * per, :]
        m1 = jnp.max(blk, axis=0, keepdims=True)
        first = jnp.min(jnp.where(blk == m1, sub, float(per)), axis=0, keepdims=True)
        m2 = jnp.max(jnp.where(sub == first, -jnp.inf, blk), axis=0, keepdims=True)
        blocks.append(blk)
        gscore.append(m1 + m2)
    masked_blocks = []
    for g in range(N_GROUP):
        beaten = jnp.zeros((1, t), F32)
        for g2 in range(N_GROUP):
            if g2 == g:
                continue
            wins = (gscore[g2] > gscore[g]) | ((gscore[g2] == gscore[g]) & (g2 < g))
            beaten = beaten + jnp.where(wins, 1.0, 0.0)
        keep = beaten < float(TOPK_GROUP)
        masked_blocks.append(jnp.where(keep, blocks[g], -jnp.inf))
    masked = jnp.concatenate(masked_blocks, axis=0)

    eidx = lax.broadcasted_iota(jnp.int32, (e, t), 0).astype(F32)
    member = jnp.zeros((e, t), F32)
    sels, ws = [], []
    for _ in range(TOP_K):
        m = jnp.max(masked, axis=0, keepdims=True)
        sel = jnp.min(jnp.where(masked == m, eidx, float(e)), axis=0, keepdims=True)
        hit = eidx == sel
        ws.append(jnp.sum(jnp.where(hit, scores, 0.0), axis=0, keepdims=True))
        masked = jnp.where(hit, -jnp.inf, masked)
        member = jnp.where(hit, 1.0, member)
        sels.append(sel)
    wsum = ws[0]
    for k in range(1, TOP_K):
        wsum = wsum + ws[k]

    before = jnp.dot(member.astype(BF16), tri_ref[...], preferred_element_type=F32) + base_ref[...]
    for k in range(TOP_K):
        idx_ref[k:k + 1, :] = sels[k].astype(jnp.int32)
        w_ref[k:k + 1, :] = ws[k] / wsum * ROUTED_SCALE
        rk = jnp.sum(jnp.where(eidx == sels[k], before, 0.0), axis=0, keepdims=True)
        rank_ref[k:k + 1, :] = rk.astype(jnp.int32)
    base_ref[...] = base_ref[...] + jnp.sum(member, axis=1, keepdims=True)
    cnt_ref[...] = jnp.broadcast_to(base_ref[...], cnt_ref.shape)


def _router(x1, wrt, bias, n, d):
    e = wrt.shape[0]
    t = _pick(n, (512, 256, 128))
    tok = lambda dt: jax.ShapeDtypeStruct((TOP_K, n), dt)
    return pl.pallas_call(
        functools.partial(_router_kernel, t=t, e=e),
        out_shape=(tok(jnp.int32), tok(F32), tok(jnp.int32), jax.ShapeDtypeStruct((e, LANES), F32)),
        grid=(n // t,),
        in_specs=[
            pl.BlockSpec((t, d), lambda i: (i, 0)),
            pl.BlockSpec((e, d), lambda i: (0, 0)),
            pl.BlockSpec((e, 1), lambda i: (0, 0)),
        ],
        out_specs=(pl.BlockSpec((TOP_K, t), lambda i: (0, i)),
                   pl.BlockSpec((TOP_K, t), lambda i: (0, i)),
                   pl.BlockSpec((TOP_K, t), lambda i: (0, i)),
                   pl.BlockSpec((e, LANES), lambda i: (0, 0))),
        scratch_shapes=[pltpu.VMEM((e, 1), F32), pltpu.VMEM((t, t), BF16)],
        compiler_params=_params(("arbitrary",)),
        name="router_topk",
    )(x1, wrt, bias)


def _dest_kernel(idx_ref, rank_ref, start_ref, dest_ref, *, t, e):
    eidx = lax.broadcasted_iota(jnp.int32, (e, t), 0)
    for k in range(TOP_K):
        hit = eidx == idx_ref[k:k + 1, :]
        st = jnp.sum(jnp.where(hit, start_ref[...], 0.0), axis=0, keepdims=True)
        dest_ref[0, k:k + 1, :] = st.astype(jnp.int32) + rank_ref[k:k + 1, :]


def _dest(idx, rank, start_col, n, t):
    e = start_col.shape[0]
    return pl.pallas_call(
        functools.partial(_dest_kernel, t=t, e=e),
        out_shape=jax.ShapeDtypeStruct((n // t, TOP_K, t), jnp.int32),
        grid=(n // t,),
        in_specs=[
            pl.BlockSpec((TOP_K, t), lambda i: (0, i)),
            pl.BlockSpec((TOP_K, t), lambda i: (0, i)),
            pl.BlockSpec((e, 1), lambda i: (0, 0)),
        ],
        out_specs=pl.BlockSpec((1, TOP_K, t), lambda i: (i, 0, 0)),
        compiler_params=_params(("arbitrary",)),
        name="moe_dest",
    )(idx, rank, start_col)


def _token_copy(src_ref, src_tok, dst_ref, dst_tok, sem, rpt):
    s0 = pl.multiple_of(src_tok * rpt, rpt)
    d0 = pl.multiple_of(dst_tok * rpt, rpt)
    return pltpu.make_async_copy(src_ref.at[pl.ds(s0, rpt), :], dst_ref.at[pl.ds(d0, rpt), :], sem)


TOKENS_PER_SLOT_ROW = LANES // TOP_K


def _for_each_token_slots(table, t, fn):
    def body(r, c):
        for pair in range(0, TOKENS_PER_SLOT_ROW, 2):
            slots = [[table[r, (pair + p) * TOP_K + k] for k in range(TOP_K)] for p in range(2)]
            for p in range(2):
                fn(r * TOKENS_PER_SLOT_ROW + pair + p, slots[p])
        return c

    lax.fori_loop(0, t // TOKENS_PER_SLOT_ROW, body, 0)


def _slot_table_copy(dest_hbm, step, dsm, slot, sem_idx):
    return pltpu.make_async_copy(dest_hbm.at[step], dsm.at[slot], sem_idx.at[slot])


def _dispatch_kernel(dest_hbm, x_ref, wsg_ref, wsu_ref, wsd_ref, xs_hbm, sh_ref, dsm, zero_ref, sem_idx, sem,
                     sem_pad, *, t, rpt, n_steps, pad_rows):
    i = pl.program_id(0)
    slot = i % 2

    @pl.when(i == 0)
    def _():
        _slot_table_copy(dest_hbm, 0, dsm, 0, sem_idx).start()
        zero_ref[...] = jnp.zeros_like(zero_ref)
        pad = pltpu.make_async_copy(zero_ref, xs_hbm.at[pl.ds(n_steps * t * TOP_K * rpt, pad_rows), :], sem_pad)
        pad.start()
        pad.wait()

    _slot_table_copy(dest_hbm, i, dsm, slot, sem_idx).wait()

    @pl.when(i + 1 < n_steps)
    def _():
        _slot_table_copy(dest_hbm, i + 1, dsm, 1 - slot, sem_idx).start()

    def issue(tl, slots):
        for k in range(TOP_K):
            _token_copy(x_ref, tl, xs_hbm, slots[k], sem, rpt).start(priority=k % 2)

    _for_each_token_slots(dsm.at[slot], t, issue)

    x = _unpack_words([x_ref[_token_rows(0, t, rpt, j), :] for j in range(rpt)]).astype(BF16)
    g = jnp.dot(x, wsg_ref[...], preferred_element_type=F32)
    u = jnp.dot(x, wsu_ref[...], preferred_element_type=F32)
    h = (g * _sigmoid(g)) * u
    sh_ref[...] = jnp.dot(h.astype(BF16), wsd_ref[...], preferred_element_type=F32).astype(sh_ref.dtype)

    for _ in range(TOP_K):
        pltpu.make_async_copy(x_ref, xs_hbm.at[pl.ds(0, t * rpt), :], sem).wait()


def _dispatch(dest3, x1p, wsg, wsu, wsd, n, d, t, pad_tokens):
    rpt = _rows_per_token(d)
    const = lambda shape: pl.BlockSpec(shape, lambda i: (0, 0), pipeline_mode=pl.Buffered(1))
    return pl.pallas_call(
        functools.partial(_dispatch_kernel, t=t, rpt=rpt, n_steps=n // t, pad_rows=pad_tokens * rpt),
        out_shape=(jax.ShapeDtypeStruct(((n * TOP_K + pad_tokens) * rpt, LANES), U32),
                   jax.ShapeDtypeStruct((n, d), BF16)),
        grid=(n // t,),
        in_specs=[pl.BlockSpec(memory_space=pl.ANY), pl.BlockSpec((t * rpt, LANES), lambda i: (i, 0)),
                  const(wsg.shape), const(wsu.shape), const(wsd.shape)],
        out_specs=(pl.BlockSpec(memory_space=pl.ANY), pl.BlockSpec((t, d), lambda i: (i, 0))),
        scratch_shapes=[pltpu.SMEM((2, TOP_K * t // LANES, LANES), jnp.int32),
                        pltpu.VMEM((pad_tokens * rpt, LANES), U32),
                        pltpu.SemaphoreType.DMA((2,)), pltpu.SemaphoreType.DMA, pltpu.SemaphoreType.DMA],
        compiler_params=_params(("arbitrary",)),
        name="moe_dispatch",
    )(dest3, x1p, wsg, wsu, wsd)


def _expert_kernel(r0_ref, we_ref, nw_ref, nx_ref, par_ref, xs_hbm, wg_hbm, wu_hbm, wd_hbm, o_hbm,
                   xin_ref, out_ref, wgf_ref, wuf_ref, wdf_ref, wgb_ref, wub_ref, wdb_ref,
                   wsem, isem, osem, *, tm, rpt):
    w = pl.program_id(0)
    nw = nw_ref[0]
    rows = tm * rpt

    def weight_copies(expert, buf):
        return [pltpu.make_async_copy(src.at[expert], dst.at[buf], wsem.at[buf])
                for src, dst in ((wg_hbm, wgf_ref), (wu_hbm, wuf_ref), (wd_hbm, wdf_ref))]

    def rows_in(item, buf):
        r0 = pl.multiple_of(r0_ref[item] * rpt, rpt)
        return pltpu.make_async_copy(xs_hbm.at[pl.ds(r0, rows), :], xin_ref.at[buf], isem.at[buf])

    def rows_out(item, buf):
        r0 = pl.multiple_of(r0_ref[item] * rpt, rpt)
        return pltpu.make_async_copy(out_ref.at[buf], o_hbm.at[pl.ds(r0, rows), :], osem.at[buf])

    @pl.when(w < nw)
    def _():
        e = we_ref[w]
        pw = jnp.maximum(w - 1, 0)
        buf_io = w % 2

        @pl.when(w == 0)
        def _():
            rows_in(0, 0).start()

        @pl.when(w + 1 < nw)
        def _():
            rows_in(w + 1, 1 - buf_io).start()

        @pl.when((w == 0) | (e != we_ref[pw]))
        def _():
            buf = par_ref[w]
            nxt = nx_ref[w]

            @pl.when(w == 0)
            def _():
                for c in weight_copies(e, buf):
                    c.start(priority=1)

            @pl.when(nxt >= 0)
            def _():
                for c in weight_copies(nxt, 1 - buf):
                    c.start(priority=1)

            for c in weight_copies(e, buf):
                c.wait()
            wgb_ref[...] = wgf_ref[buf].astype(BF16)
            wub_ref[...] = wuf_ref[buf].astype(BF16)
            wdb_ref[...] = wdf_ref[buf].astype(BF16)

        rows_in(w, buf_io).wait()
        xin = xin_ref.at[buf_io]
        x = _unpack_words([xin[_token_rows(0, tm, rpt, j), :] for j in range(rpt)]).astype(BF16)
        g = jnp.dot(x, wgb_ref[...], preferred_element_type=F32)
        u = jnp.dot(x, wub_ref[...], preferred_element_type=F32)
        h = (g * _sigmoid(g)) * u
        o = jnp.dot(h.astype(BF16), wdb_ref[...], preferred_element_type=F32)

        _store_packed(out_ref.at[buf_io], 0, o)

        @pl.when(w > 0)
        def _():
            rows_out(pw, 1 - buf_io).wait()

        rows_out(w, buf_io).start()

        @pl.when(w == nw - 1)
        def _():
            rows_out(w, buf_io).wait()


def _experts(work, xs, w_gate, w_up, w_down, tm):
    _, d, de = w_gate.shape
    rpt = _rows_per_token(d)
    n_work = work[0].shape[0]
    hbm = pl.BlockSpec(memory_space=pl.ANY)
    return pl.pallas_call(
        functools.partial(_expert_kernel, tm=tm, rpt=rpt),
        out_shape=jax.ShapeDtypeStruct(xs.shape, U32),
        grid_spec=pltpu.PrefetchScalarGridSpec(
            num_scalar_prefetch=len(work),
            grid=(n_work,),
            in_specs=[hbm, hbm, hbm, hbm],
            out_specs=hbm,
            scratch_shapes=[pltpu.VMEM((2, tm * rpt, LANES), U32), pltpu.VMEM((2, tm * rpt, LANES), U32),
                            pltpu.VMEM((2, d, de), F32), pltpu.VMEM((2, d, de), F32), pltpu.VMEM((2, de, d), F32),
                            pltpu.VMEM((d, de), BF16), pltpu.VMEM((d, de), BF16), pltpu.VMEM((de, d), BF16),
                            pltpu.SemaphoreType.DMA((2,)), pltpu.SemaphoreType.DMA((2,)),
                            pltpu.SemaphoreType.DMA((2,))],
        ),
        compiler_params=_params(("arbitrary",)),
        name="moe_experts",
    )(*work, xs, w_gate, w_up, w_down)


def _combine_kernel(dest_hbm, x1_ref, wt_ref, sh_ref, o_hbm, g2_ref, b2_ref, y_ref,
                    dsm, gbuf, slab_ref, sem_idx, sem, *, t, tt, d, rpt, n_steps):
    i = pl.program_id(0)
    slot = i % 2

    def issue(step_slot):
        def gather(tl, slots):
            for k in range(TOP_K):
                _token_copy(o_hbm, slots[k], gbuf.at[step_slot], k * t + tl, sem.at[step_slot],
                            rpt).start(priority=k % 2)

        _for_each_token_slots(dsm.at[step_slot], t, gather)

    @pl.when(i == 0)
    def _():
        first = _slot_table_copy(dest_hbm, 0, dsm, 0, sem_idx)
        first.start()
        first.wait()
        issue(0)
        if n_steps > 1:
            _slot_table_copy(dest_hbm, 1, dsm, 1, sem_idx).start()

    @pl.when(i + 1 < n_steps)
    def _():
        _slot_table_copy(dest_hbm, i + 1, dsm, 1 - slot, sem_idx).wait()
        issue(1 - slot)

        @pl.when(i + 2 < n_steps)
        def _():
            _slot_table_copy(dest_hbm, i + 2, dsm, slot, sem_idx).start()

    gcur = gbuf.at[slot]
    pltpu.make_async_copy(o_hbm.at[pl.ds(0, TOP_K * t * rpt), :], gcur, sem.at[slot]).wait()

    acc_lo = [None] * rpt
    acc_hi = [None] * rpt
    for k in range(TOP_K):
        wk = wt_ref[:, k:k + 1]
        for j in range(rpt):
            words = gcur[_token_rows(k * t * rpt, t, rpt, j), :]
            lo = wk * pltpu.bitcast(words << 16, F32)
            hi = wk * pltpu.bitcast(words & U32(HI_MASK), F32)
            acc_lo[j] = lo if k == 0 else acc_lo[j] + lo
            acc_hi[j] = hi if k == 0 else acc_hi[j] + hi
    routed = jnp.concatenate(acc_lo + acc_hi, axis=1)

    v = DN_ALPHA * x1_ref[...] + (routed + sh_ref[...].astype(F32))
    mu = jnp.mean(v, axis=-1, keepdims=True)
    cen = v - mu
    var = jnp.mean(cen * cen, axis=-1, keepdims=True)
    y = cen * lax.rsqrt(var + LN_EPS) * g2_ref[...] + b2_ref[...]
    _store_batch_major(y_ref, slab_ref, y, tt, d)


def _combine(dest3, x1, wt, sh, o, g2, b2, n, d, t):
    tt = t // SUBLANES
    rpt = _rows_per_token(d)
    const = lambda shape: pl.BlockSpec(shape, lambda i: (0, 0), pipeline_mode=pl.Buffered(1))
    return pl.pallas_call(
        functools.partial(_combine_kernel, t=t, tt=tt, d=d, rpt=rpt, n_steps=n // t),
        out_shape=jax.ShapeDtypeStruct((SUBLANES, n // SUBLANES, d), F32),
        grid=(n // t,),
        in_specs=[
            pl.BlockSpec(memory_space=pl.ANY),
            pl.BlockSpec((t, d), lambda i: (i, 0)),
            pl.BlockSpec((t, TOP_K), lambda i: (i, 0)),
            pl.BlockSpec((t, d), lambda i: (i, 0)),
            pl.BlockSpec(memory_space=pl.ANY),
            const((1, d)), const((1, d)),
        ],
        out_specs=pl.BlockSpec((SUBLANES, tt, d), lambda i: (0, i, 0)),
        scratch_shapes=[pltpu.SMEM((2, TOP_K * t // LANES, LANES), jnp.int32), pltpu.VMEM((2, TOP_K * t * rpt, LANES), U32),
                        pltpu.VMEM((d // LANES, t, LANES), F32),
                        pltpu.SemaphoreType.DMA((2,)), pltpu.SemaphoreType.DMA((2,))],
        compiler_params=_params(("arbitrary",)),
        name="moe_combine_ln2",
    )(dest3, x1, wt, sh, o, g2, b2)


def _work_items(counts, m, tm):
    e = counts.shape[0]
    end = jnp.cumsum(counts)
    start = end - counts
    nchunk = (counts + tm - 1) // tm
    cum = jnp.cumsum(nchunk)
    off = cum - nchunk
    n_work = m // tm + e
    w = jnp.arange(n_work, dtype=jnp.int32)
    we = jnp.minimum(jnp.sum(cum[None, :] <= w[:, None], axis=1), e - 1).astype(jnp.int32)
    row0 = (start[we] + (w - off[we]) * tm).astype(jnp.int32)
    nw = cum[-1].astype(jnp.int32)
    row0 = jnp.where(w < nw, row0, 0)
    ids = jnp.arange(e, dtype=jnp.int32)
    later = lax.cummin(jnp.where(counts > 0, ids, e)[::-1])[::-1]
    nxt_of = jnp.concatenate([later[1:], jnp.full((1,), e, jnp.int32)])
    nxt_of = jnp.where(nxt_of >= e, -1, nxt_of).astype(jnp.int32)
    ordinal = jnp.cumsum((counts > 0).astype(jnp.int32)) - 1
    work = (row0, we, nw.reshape(1), nxt_of[we], (ordinal[we] % 2).astype(jnp.int32))
    return work, start.astype(jnp.int32)


def kernel(x, w_in, b_in, conv_w, conv_b, lru_wa, lru_ba, lru_wx, lru_bx, lru_lambda, w_rnn_br, w_attn_br, w_out, ln1_g, ln1_b, w_router, router_bias, w_gate, w_up, w_down, ws_gate, ws_up, ws_down, ln2_g, ln2_b):
    bsz, s, d = x.shape
    assert bsz == SUBLANES, "time-major rows need batch == 8"
    assert DEPTH == 1 and w_in.shape[0] == 1
    n = bsz * s
    c = lru_lambda.shape[1]
    a = HEADS_PER_GROUP * len(ATTN_GROUPS) * HEAD_DIM
    assert c == d and w_in.shape[2] == 2 * c + 3 * a + 2 * d
    layer = 0

    gw = HEADS_PER_GROUP * HEAD_DIM
    src = (0, c, 2 * c, 2 * c + a, 2 * c + 2 * a, 2 * c + 3 * a, 2 * c + 3 * a + d, 2 * c + 3 * a + 2 * d)
    part = lambda v, p, lo=0, hi=None: v[..., src[p]:src[p + 1]][..., lo:hi]
    cat = lambda v, pieces: jnp.concatenate([part(v, *p) for p in pieces], axis=-1)
    main = ((0,), (1,), (5,), (6,))
    w_main = cat(w_in[layer], main).astype(BF16)
    b_main = cat(b_in[layer], main).reshape(1, -1)
    col_u, col_g, col_gr, col_ga = 0, c, 2 * c, 2 * c + d
    row = lambda v: v.reshape(1, -1)

    z, xb = _in_proj(x, w_main, b_main, n, d)

    yr = _lru(z, conv_w[layer], row(conv_b[layer]), lru_wa[layer].astype(BF16), row(lru_ba[layer]),
              lru_wx[layer].astype(BF16), row(lru_bx[layer]), row(lru_lambda[layer]), n, c, col_u, col_g)

    os_, sts = [], []
    for g in range(len(ATTN_GROUPS)):
        cols = tuple((p, g * gw, (g + 1) * gw) for p in (2, 3, 4))
        qkv = _qkv_proj(xb, cat(w_in[layer], cols).astype(BF16), cat(b_in[layer], cols).reshape(1, -1), g, n, d)
        o, st = _attn_group(qkv, g, n)
        os_.append(o)
        sts.append(st)

    x1, x1p = _merge(x, yr, z, os_, sts, w_rnn_br[layer].astype(BF16), w_attn_br[layer].astype(BF16),
                     w_out[layer].astype(BF16), row(ln1_g[layer]), row(ln1_b[layer]), n, d, col_gr, col_ga)

    n_exp = w_router.shape[2]
    idx, wts, rank, cnt = _router(x1, w_router[layer].T.astype(BF16), router_bias[layer].reshape(n_exp, 1), n, d)

    tm = 256
    t_tok = _pick(n, (256, 128))
    m = n * TOP_K
    counts = cnt[:, 0].astype(jnp.int32)
    work, start = _work_items(counts, m, tm)
    dest3 = _dest(idx, rank, start.astype(F32).reshape(n_exp, 1), n, t_tok)
    dest3 = dest3.transpose(0, 2, 1).reshape(n // t_tok, t_tok * TOP_K // LANES, LANES)

    xs, shared = _dispatch(dest3, x1p, ws_gate[layer].astype(BF16), ws_up[layer].astype(BF16),
                           ws_down[layer].astype(BF16), n, d, t_tok, pad_tokens=tm)
    o = _experts(work, xs, w_gate[layer], w_up[layer], w_down[layer], tm)
    y = _combine(dest3, x1, wts.T, shared, o, row(ln2_g[layer]), row(ln2_b[layer]), n, d, t_tok)
    return y
```

```python
import functools

import jax
import jax.numpy as jnp
from jax import lax
from jax.experimental import pallas as pl
from jax.experimental.pallas import tpu as pltpu

HEAD_DIM = 128
ATTN_GROUPS = ((128, 1), (512, 4), (2048, 16))
HEADS_PER_GROUP = 4
CONV_WIDTH = 4
LRU_C = 8.0
N_GROUP = 8
TOPK_GROUP = 4
TOP_K = 8
ROUTED_SCALE = 2.5
DEPTH = 1
DN_ALPHA = (2 * DEPTH) ** 0.25
LN_EPS = 1e-5

SUBLANES = 8
LANES = 128
V7X_VMEM_LIMIT_BYTES = 56 * 1024 * 1024

F32 = jnp.float32
BF16 = jnp.bfloat16


def _pick(n, cands):
    for c in cands:
        if n % c == 0:
            return c
    raise ValueError(f"no tile in {cands} divides {n}")


LOG2E = 1.4426950408889634


def _sigmoid(x):
    return 1.0 / (1.0 + jnp.exp2(x * (-LOG2E)))


def _params(sem, vmem=V7X_VMEM_LIMIT_BYTES):
    return pltpu.CompilerParams(dimension_semantics=sem, vmem_limit_bytes=vmem)


def _time_major_rows(x_ref, slab_ref, tt, d):
    for b in range(SUBLANES):
        for j in range(d // LANES):
            slab_ref[j, pl.ds(b, tt, stride=SUBLANES), :] = x_ref[b, :, j * LANES:(j + 1) * LANES]
    return jnp.concatenate([slab_ref[j] for j in range(d // LANES)], axis=1)


def _store_batch_major(y_ref, slab_ref, y, tt, d):
    for j in range(d // LANES):
        slab_ref[j] = y[:, j * LANES:(j + 1) * LANES]
    for b in range(SUBLANES):
        for j in range(d // LANES):
            y_ref[b, :, j * LANES:(j + 1) * LANES] = slab_ref[j, pl.ds(b, tt, stride=SUBLANES), :]


U32 = jnp.uint32
HI_MASK = 0xFFFF0000


def _rows_per_token(d):
    assert d % (2 * LANES) == 0
    return d // (2 * LANES)


def _token_rows(base, m, rpt, j):
    return pl.ds(base + j, m, stride=rpt)


def _pack_words(x):
    half = x.shape[1] // 2
    bits = pltpu.bitcast(x.astype(BF16).astype(F32), U32)
    words = (bits[:, :half] >> 16) | (bits[:, half:] & U32(HI_MASK))
    return [words[:, j * LANES:(j + 1) * LANES] for j in range(half // LANES)]


def _store_packed(ref, base, x):
    chunks = _pack_words(x)
    for j, chunk in enumerate(chunks):
        ref[_token_rows(base, x.shape[0], len(chunks), j), :] = chunk


def _unpack_words(chunks):
    lo = [pltpu.bitcast(c << 16, F32) for c in chunks]
    hi = [pltpu.bitcast(c & U32(HI_MASK), F32) for c in chunks]
    return jnp.concatenate(lo + hi, axis=1)


def _in_proj_kernel(x_ref, w_ref, b_ref, z_ref, xb_ref, slab_ref, *, tt, d):
    @pl.when(pl.program_id(1) == 0)
    def _():
        for b in range(SUBLANES):
            for j in range(d // LANES):
                slab_ref[j, pl.ds(b, tt, stride=SUBLANES), :] = x_ref[b, :, j * LANES:(j + 1) * LANES]
        for j in range(d // LANES):
            xb_ref[:, j * LANES:(j + 1) * LANES] = slab_ref[j].astype(BF16)

    z = jnp.dot(xb_ref[...], w_ref[...], preferred_element_type=F32) + b_ref[...]
    z_ref[...] = z.astype(z_ref.dtype)


def _in_proj(x, w, b, n, d):
    d_out = w.shape[1]
    tm = _pick(n, (1024, 512, 256, 128, 64, 32, 16))
    tn = _pick(d_out, (1024, 512, 256, 128))
    tt = tm // SUBLANES
    return pl.pallas_call(
        functools.partial(_in_proj_kernel, tt=tt, d=d),
        out_shape=(jax.ShapeDtypeStruct((n, d_out), BF16), jax.ShapeDtypeStruct((n, d), BF16)),
        grid=(n // tm, d_out // tn),
        in_specs=[
            pl.BlockSpec((SUBLANES, tt, d), lambda i, j: (0, i, 0)),
            pl.BlockSpec((d, tn), lambda i, j: (0, j)),
            pl.BlockSpec((1, tn), lambda i, j: (0, j)),
        ],
        out_specs=(pl.BlockSpec((tm, tn), lambda i, j: (i, j)),
                   pl.BlockSpec((tm, d), lambda i, j: (i, 0))),
        scratch_shapes=[pltpu.VMEM((d // LANES, tm, LANES), F32)],
        compiler_params=_params(("arbitrary", "arbitrary")),
        name="in_proj",
    )(x, w, b)


BF16_SUBLANES = 16


def _qkv_kernel(xb_ref, w_ref, b_ref, p_ref, o_ref, *, stride):
    r = (jnp.dot(xb_ref[...], w_ref[...], preferred_element_type=F32) + b_ref[...]).astype(BF16)
    sb = p_ref.shape[0]
    for s in range(r.shape[0] // (2 * sb)):
        halves = [jnp.dot(p_ref[...], r[(2 * s + i) * sb:(2 * s + i + 1) * sb, :], preferred_element_type=F32)
                  for i in range(2)]
        for c in range(stride):
            tile = jnp.concatenate([hv[c * SUBLANES:(c + 1) * SUBLANES, :] for hv in halves], axis=0)
            o_ref[c, s * BF16_SUBLANES:(s + 1) * BF16_SUBLANES, :] = tile.astype(BF16)


def _qkv_proj(xb, w, b, g, n, d):
    _, dil = ATTN_GROUPS[g]
    stride = SUBLANES * dil
    gw = HEADS_PER_GROUP * HEAD_DIM
    sb = SUBLANES * stride
    tm = max(_pick(n, (1024, 512, 256)), 2 * sb)
    per = tm // stride
    new = jnp.arange(sb, dtype=jnp.int32)
    old = (new % SUBLANES) * stride + new // SUBLANES
    perm = (old[:, None] == jnp.arange(sb, dtype=jnp.int32)[None, :]).astype(BF16)
    return pl.pallas_call(
        functools.partial(_qkv_kernel, stride=stride),
        out_shape=jax.ShapeDtypeStruct((stride, n // stride, 3 * gw), BF16),
        grid=(n // tm, 3),
        in_specs=[
            pl.BlockSpec((tm, d), lambda i, j: (i, 0)),
            pl.BlockSpec((d, gw), lambda i, j: (0, j)),
            pl.BlockSpec((1, gw), lambda i, j: (0, j)),
            pl.BlockSpec((sb, sb), lambda i, j: (0, 0), pipeline_mode=pl.Buffered(1)),
        ],
        out_specs=pl.BlockSpec((stride, per, gw), lambda i, j: (0, i, j)),
        compiler_params=_params(("arbitrary", "arbitrary")),
        name=f"qkv_proj_g{g}",
    )(xb, w, b, perm)


def _lru_kernel(u_ref, g_ref, cw_ref, cb_ref, wa_ref, ba_ref, wx_ref, bx_ref, lam_ref, y_ref,
                carry_ref, h_ref, a_ref, b_ref, *, tt, nblk, bw):
    ti = pl.program_id(1)
    rows = tt * SUBLANES
    halo = SUBLANES * (CONV_WIDTH - 1)

    @pl.when(ti == 0)
    def _():
        carry_ref[...] = jnp.zeros_like(carry_ref)
        h_ref[...] = jnp.zeros_like(h_ref)

    u = u_ref[...].astype(F32)
    ext = jnp.concatenate([carry_ref[...], u], axis=0)
    uc = cb_ref[...] + cw_ref[0:1, :] * ext[0:rows, :]
    for j in range(1, CONV_WIDTH):
        uc = uc + cw_ref[j:j + 1, :] * ext[SUBLANES * j:SUBLANES * j + rows, :]
    carry_ref[...] = u[rows - halo:, :]

    ucb = uc.astype(BF16)
    r_parts, i_parts = [], []
    for kb in range(nblk):
        blk = ucb[:, kb * bw:(kb + 1) * bw]
        r_parts.append(jnp.dot(blk, wa_ref[kb], preferred_element_type=F32))
        i_parts.append(jnp.dot(blk, wx_ref[kb], preferred_element_type=F32))
    r = _sigmoid(jnp.concatenate(r_parts, axis=1) + ba_ref[...])
    ig = _sigmoid(jnp.concatenate(i_parts, axis=1) + bx_ref[...])

    nl = -lam_ref[...]
    softplus = jnp.maximum(nl, 0.0) + jnp.log(1.0 + jnp.exp(-jnp.abs(nl)))
    a = jnp.exp2(r * ((-LRU_C * LOG2E) * softplus))
    gap = 1.0 - a * a
    mult = jnp.where(gap > 0.0, gap * lax.rsqrt(gap), 0.0)
    gated = uc * ig
    a_ref[...] = a
    b_ref[...] = gated * mult

    @pl.when(ti == 0)
    def _():
        b_ref[0:SUBLANES, :] = gated[0:SUBLANES, :]

    def step(t, h):
        r0 = pl.multiple_of(t * SUBLANES, SUBLANES)
        h = a_ref[pl.ds(r0, SUBLANES), :] * h + b_ref[pl.ds(r0, SUBLANES), :]
        b_ref[pl.ds(r0, SUBLANES), :] = h
        return h

    h_ref[...] = lax.fori_loop(0, tt, step, h_ref[...], unroll=8)

    g = g_ref[...].astype(F32)
    gelu = 0.5 * g * (1.0 + jnp.tanh(0.7978845608028654 * (g + 0.044715 * (g * g * g))))
    y_ref[...] = (b_ref[...] * gelu).astype(y_ref.dtype)


def _lru(z, conv_w, conv_b, wa, ba, wx, bx, lam, n, c, col_u, col_g):
    nblk_total, bw, _ = wa.shape
    ct = _pick(c, (512, 256, 128))
    tt = _pick(n // SUBLANES, (128, 64, 32, 16, 8))
    rows = tt * SUBLANES
    nblk = ct // bw
    vec = lambda: pl.BlockSpec((1, ct), lambda ci, ti: (0, ci))
    return pl.pallas_call(
        functools.partial(_lru_kernel, tt=tt, nblk=nblk, bw=bw),
        out_shape=jax.ShapeDtypeStruct((n, c), BF16),
        grid=(c // ct, n // rows),
        in_specs=[
            pl.BlockSpec((rows, ct), lambda ci, ti: (ti, col_u // ct + ci)),
            pl.BlockSpec((rows, ct), lambda ci, ti: (ti, col_g // ct + ci)),
            pl.BlockSpec((CONV_WIDTH, ct), lambda ci, ti: (0, ci)),
            vec(),
            pl.BlockSpec((nblk, bw, bw), lambda ci, ti: (ci, 0, 0)),
            vec(),
            pl.BlockSpec((nblk, bw, bw), lambda ci, ti: (ci, 0, 0)),
            vec(),
            vec(),
        ],
        out_specs=pl.BlockSpec((rows, ct), lambda ci, ti: (ti, ci)),
        scratch_shapes=[
            pltpu.VMEM((SUBLANES * (CONV_WIDTH - 1), ct), F32),
            pltpu.VMEM((SUBLANES, ct), F32),
            pltpu.VMEM((rows, ct), F32),
            pltpu.VMEM((rows, ct), F32),
        ],
        compiler_params=_params(("arbitrary", "arbitrary")),
        name="rg_lru",
    )(z, z, conv_w, conv_b, wa, ba, wx, bx, lam)


def _attn_kernel(q_ref, k_ref, v_ref, o_ref, st_ref, *, band, dil, slopes, qblocks):
    qi = lax.broadcasted_iota(jnp.int32, (band, 2 * band), 0)
    kj = lax.broadcasted_iota(jnp.int32, (band, 2 * band), 1)
    rel = qi - kj
    scale = HEAD_DIM ** -0.5
    nt = (((1,), (1,)), ((), ()))
    lane = lax.broadcasted_iota(jnp.int32, (band, LANES), 1)

    for qq in range(qblocks):
        n = pl.program_id(1) * qblocks + qq
        win0 = pl.multiple_of(jnp.maximum(n - 1, 0) * band, band)
        own = jnp.where(n > 0, band, 0)
        rows = slice(qq * band, (qq + 1) * band)
        q = q_ref[rows, :]
        kw = k_ref[pl.ds(win0, 2 * band), :]
        vw = v_ref[pl.ds(win0, 2 * band), :]
        dist_i = rel + own
        valid = (dist_i >= 0) & (dist_i <= band)
        dist = dist_i.astype(F32)

        outs = []
        stat = jnp.zeros((band, LANES), F32)
        for h in range(HEADS_PER_GROUP):
            sl = slice(h * HEAD_DIM, (h + 1) * HEAD_DIM)
            bias = slopes[h] * dil
            s = lax.dot_general(q[:, sl], kw[:, sl], nt, preferred_element_type=F32)
            s = jnp.where(valid, s * scale - bias * dist, -jnp.inf)
            m = jnp.max(s, axis=1, keepdims=True)
            p = jnp.exp(s - m)
            l = jnp.sum(p, axis=1, keepdims=True)
            acc = jnp.dot(p.astype(BF16), vw[:, sl], preferred_element_type=F32)
            outs.append(acc / l)
            stat = jnp.where(lane == h, m + jnp.log(l), stat)
        o_ref[rows, :] = jnp.concatenate(outs, axis=1).astype(o_ref.dtype)
        st_ref[rows, :] = stat


def _attn_group(qkv, g, n):
    window, dil = ATTN_GROUPS[g]
    band = window // dil
    stride = SUBLANES * dil
    lp = n // stride
    gw = HEADS_PER_GROUP * HEAD_DIM
    n_heads = HEADS_PER_GROUP * len(ATTN_GROUPS)
    slopes = tuple(2.0 ** (-8.0 * (g * HEADS_PER_GROUP + h + 1) / n_heads) for h in range(HEADS_PER_GROUP))
    qblocks = _pick(lp // band, (4, 2, 1))
    tq = qblocks * band
    o, st = pl.pallas_call(
        functools.partial(_attn_kernel, band=band, dil=float(dil), slopes=slopes, qblocks=qblocks),
        out_shape=(jax.ShapeDtypeStruct((lp, stride * gw), BF16),
                   jax.ShapeDtypeStruct((lp, stride * LANES), F32)),
        grid=(stride, lp // tq),
        in_specs=[
            pl.BlockSpec((None, tq, gw), lambda c, i: (c, i, 0)),
            pl.BlockSpec((None, lp, gw), lambda c, i: (c, 0, 1)),
            pl.BlockSpec((None, lp, gw), lambda c, i: (c, 0, 2)),
        ],
        out_specs=(pl.BlockSpec((tq, gw), lambda c, i: (i, c)),
                   pl.BlockSpec((tq, LANES), lambda c, i: (i, c))),
        compiler_params=_params(("arbitrary", "arbitrary")),
        name=f"dilated_attn_g{g}",
    )(qkv, qkv, qkv)
    return o.reshape(n, gw), st.reshape(n, LANES)


def _merge_kernel(x_ref, yr_ref, gr_ref, ga_ref, o0_ref, o1_ref, o2_ref, s0_ref, s1_ref, s2_ref,
                  wr_ref, wa_ref, wo_ref, g1_ref, b1_ref, x1_ref, x1p_ref, slab_ref, *, tt, d):
    t1 = jnp.dot(yr_ref[...], wr_ref[...], preferred_element_type=F32)
    merged = _sigmoid(gr_ref[...].astype(F32)) * t1

    stats = [s0_ref[...], s1_ref[...], s2_ref[...]]
    outs = [o0_ref, o1_ref, o2_ref]
    heads = []
    for h in range(HEADS_PER_GROUP):
        lse = [s[:, h:h + 1] for s in stats]
        mx = jnp.maximum(jnp.maximum(lse[0], lse[1]), lse[2])
        e = [jnp.exp(v - mx) for v in lse]
        tot = e[0] + e[1] + e[2]
        acc = None
        for gi in range(len(ATTN_GROUPS)):
            term = (e[gi] / tot) * outs[gi][:, h * HEAD_DIM:(h + 1) * HEAD_DIM].astype(F32)
            acc = term if acc is None else acc + term
        heads.append(acc)
    oa = jnp.concatenate(heads, axis=1).astype(BF16)
    t2 = jnp.dot(oa, wa_ref[...], preferred_element_type=F32)
    merged = merged + _sigmoid(ga_ref[...].astype(F32)) * t2
    mix = jnp.dot(merged.astype(BF16), wo_ref[...], preferred_element_type=F32)

    v = DN_ALPHA * _time_major_rows(x_ref, slab_ref, tt, d) + mix
    mu = jnp.mean(v, axis=-1, keepdims=True)
    cen = v - mu
    var = jnp.mean(cen * cen, axis=-1, keepdims=True)
    x1 = cen * lax.rsqrt(var + LN_EPS) * g1_ref[...] + b1_ref[...]
    x1_ref[...] = x1
    _store_packed(x1p_ref, 0, x1)


def _merge(x, yr, z, os_, sts, wr, wa, wo, g1, b1, n, d, col_gr, col_ga):
    tm = _pick(n, (256, 128, 64))
    tt = tm // SUBLANES
    gw = HEADS_PER_GROUP * HEAD_DIM
    rpt = _rows_per_token(d)
    const = lambda shape: pl.BlockSpec(shape, lambda i: (0, 0), pipeline_mode=pl.Buffered(1))
    row = lambda w: pl.BlockSpec((tm, w), lambda i: (i, 0))
    return pl.pallas_call(
        functools.partial(_merge_kernel, tt=tt, d=d),
        out_shape=(jax.ShapeDtypeStruct((n, d), F32), jax.ShapeDtypeStruct((n * rpt, LANES), U32)),
        grid=(n // tm,),
        in_specs=[
            pl.BlockSpec((SUBLANES, tt, d), lambda i: (0, i, 0)),
            row(yr.shape[1]),
            pl.BlockSpec((tm, d), lambda i: (i, col_gr // d)),
            pl.BlockSpec((tm, d), lambda i: (i, col_ga // d)),
            row(gw), row(gw), row(gw),
            row(LANES), row(LANES), row(LANES),
            const(wr.shape), const(wa.shape), const(wo.shape),
            const((1, d)), const((1, d)),
        ],
        out_specs=(row(d), pl.BlockSpec((tm * rpt, LANES), lambda i: (i, 0))),
        scratch_shapes=[pltpu.VMEM((d // LANES, tm, LANES), F32)],
        compiler_params=_params(("arbitrary",)),
        name="merge_ln1",
    )(x, yr, z, z, *os_, *sts, wr, wa, wo, g1, b1)


def _router_kernel(x_ref, wr_ref, bias_ref, idx_ref, w_ref, rank_ref, cnt_ref, base_ref, tri_ref, *, t, e):
    i = pl.program_id(0)

    @pl.when(i == 0)
    def _():
        base_ref[...] = jnp.zeros_like(base_ref)
        rr = lax.broadcasted_iota(jnp.int32, (t, t), 0)
        cc = lax.broadcasted_iota(jnp.int32, (t, t), 1)
        tri_ref[...] = jnp.where(rr < cc, 1.0, 0.0).astype(BF16)

    nt = (((1,), (1,)), ((), ()))
    logits = lax.dot_general(wr_ref[...], x_ref[...].astype(BF16), nt, preferred_element_type=F32)
    scores = _sigmoid(logits)
    biased = scores + bias_ref[...]

    per = e // N_GROUP
    sub = lax.broadcasted_iota(jnp.int32, (per, t), 0).astype(F32)
    blocks, gscore = [], []
    for g in range(N_GROUP):
        blk = biased[g * per:(g + 1) * per, :]
        m1 = jnp.max(blk, axis=0, keepdims=True)
        first = jnp.min(jnp.where(blk == m1, sub, float(per)), axis=0, keepdims=True)
        m2 = jnp.max(jnp.where(sub == first, -jnp.inf, blk), axis=0, keepdims=True)
        blocks.append(blk)
        gscore.append(m1 + m2)
    masked_blocks = []
    for g in range(N_GROUP):
        beaten = jnp.zeros((1, t), F32)
        for g2 in range(N_GROUP):
            if g2 == g:
                continue
            wins = (gscore[g2] > gscore[g]) | ((gscore[g2] == gscore[g]) & (g2 < g))
            beaten = beaten + jnp.where(wins, 1.0, 0.0)
        keep = beaten < float(TOPK_GROUP)
        masked_blocks.append(jnp.where(keep, blocks[g], -jnp.inf))
    masked = jnp.concatenate(masked_blocks, axis=0)

    eidx = lax.broadcasted_iota(jnp.int32, (e, t), 0).astype(F32)
    member = jnp.zeros((e, t), F32)
    sels, ws = [], []
    for _ in range(TOP_K):
        m = jnp.max(masked, axis=0, keepdims=True)
        sel = jnp.min(jnp.where(masked == m, eidx, float(e)), axis=0, keepdims=True)
        hit = eidx == sel
        ws.append(jnp.sum(jnp.where(hit, scores, 0.0), axis=0, keepdims=True))
        masked = jnp.where(hit, -jnp.inf, masked)
        member = jnp.where(hit, 1.0, member)
        sels.append(sel)
    wsum = ws[0]
    for k in range(1, TOP_K):
        wsum = wsum + ws[k]

    before = jnp.dot(member.astype(BF16), tri_ref[...], preferred_element_type=F32) + base_ref[...]
    for k in range(TOP_K):
        idx_ref[k:k + 1, :] = sels[k].astype(jnp.int32)
        w_ref[k:k + 1, :] = ws[k] / wsum * ROUTED_SCALE
        rk = jnp.sum(jnp.where(eidx == sels[k], before, 0.0), axis=0, keepdims=True)
        rank_ref[k:k + 1, :] = rk.astype(jnp.int32)
    base_ref[...] = base_ref[...] + jnp.sum(member, axis=1, keepdims=True)
    cnt_ref[...] = jnp.broadcast_to(base_ref[...], cnt_ref.shape)


def _router(x1, wrt, bias, n, d):
    e = wrt.shape[0]
    t = _pick(n, (512, 256, 128))
    tok = lambda dt: jax.ShapeDtypeStruct((TOP_K, n), dt)
    return pl.pallas_call(
        functools.partial(_router_kernel, t=t, e=e),
        out_shape=(tok(jnp.int32), tok(F32), tok(jnp.int32), jax.ShapeDtypeStruct((e, LANES), F32)),
        grid=(n // t,),
        in_specs=[
            pl.BlockSpec((t, d), lambda i: (i, 0)),
            pl.BlockSpec((e, d), lambda i: (0, 0)),
            pl.BlockSpec((e, 1), lambda i: (0, 0)),
        ],
        out_specs=(pl.BlockSpec((TOP_K, t), lambda i: (0, i)),
                   pl.BlockSpec((TOP_K, t), lambda i: (0, i)),
                   pl.BlockSpec((TOP_K, t), lambda i: (0, i)),
                   pl.BlockSpec((e, LANES), lambda i: (0, 0))),
        scratch_shapes=[pltpu.VMEM((e, 1), F32), pltpu.VMEM((t, t), BF16)],
        compiler_params=_params(("arbitrary",)),
        name="router_topk",
    )(x1, wrt, bias)


def _dest_kernel(idx_ref, rank_ref, start_ref, dest_ref, *, t, e):
    eidx = lax.broadcasted_iota(jnp.int32, (e, t), 0)
    for k in range(TOP_K):
        hit = eidx == idx_ref[k:k + 1, :]
        st = jnp.sum(jnp.where(hit, start_ref[...], 0.0), axis=0, keepdims=True)
        dest_ref[0, k:k + 1, :] = st.astype(jnp.int32) + rank_ref[k:k + 1, :]


def _dest(idx, rank, start_col, n, t):
    e = start_col.shape[0]
    return pl.pallas_call(
        functools.partial(_dest_kernel, t=t, e=e),
        out_shape=jax.ShapeDtypeStruct((n // t, TOP_K, t), jnp.int32),
        grid=(n // t,),
        in_specs=[
            pl.BlockSpec((TOP_K, t), lambda i: (0, i)),
            pl.BlockSpec((TOP_K, t), lambda i: (0, i)),
            pl.BlockSpec((e, 1), lambda i: (0, 0)),
        ],
        out_specs=pl.BlockSpec((1, TOP_K, t), lambda i: (i, 0, 0)),
        compiler_params=_params(("arbitrary",)),
        name="moe_dest",
    )(idx, rank, start_col)


def _token_copy(src_ref, src_tok, dst_ref, dst_tok, sem, rpt):
    s0 = pl.multiple_of(src_tok * rpt, rpt)
    d0 = pl.multiple_of(dst_tok * rpt, rpt)
    return pltpu.make_async_copy(src_ref.at[pl.ds(s0, rpt), :], dst_ref.at[pl.ds(d0, rpt), :], sem)


TOKENS_PER_SLOT_ROW = LANES // TOP_K


def _for_each_token_slots(table, t, fn):
    def body(r, c):
        for pair in range(0, TOKENS_PER_SLOT_ROW, 2):
            slots = [[table[r, (pair + p) * TOP_K + k] for k in range(TOP_K)] for p in range(2)]
            for p in range(2):
                fn(r * TOKENS_PER_SLOT_ROW + pair + p, slots[p])
        return c

    lax.fori_loop(0, t // TOKENS_PER_SLOT_ROW, body, 0)


def _slot_table_copy(dest_hbm, step, dsm, slot, sem_idx):
    return pltpu.make_async_copy(dest_hbm.at[step], dsm.at[slot], sem_idx.at[slot])


def _dispatch_kernel(dest_hbm, x_ref, wsg_ref, wsu_ref, wsd_ref, xs_hbm, sh_ref, dsm, zero_ref, sem_idx, sem,
                     sem_pad, *, t, rpt, n_steps, pad_rows):
    i = pl.program_id(0)
    slot = i % 2

    @pl.when(i == 0)
    def _():
        _slot_table_copy(dest_hbm, 0, dsm, 0, sem_idx).start()
        zero_ref[...] = jnp.zeros_like(zero_ref)
        pad = pltpu.make_async_copy(zero_ref, xs_hbm.at[pl.ds(n_steps * t * TOP_K * rpt, pad_rows), :], sem_pad)
        pad.start()
        pad.wait()

    _slot_table_copy(dest_hbm, i, dsm, slot, sem_idx).wait()

    @pl.when(i + 1 < n_steps)
    def _():
        _slot_table_copy(dest_hbm, i + 1, dsm, 1 - slot, sem_idx).start()

    def issue(tl, slots):
        for k in range(TOP_K):
            _token_copy(x_ref, tl, xs_hbm, slots[k], sem, rpt).start(priority=k % 2)

    _for_each_token_slots(dsm.at[slot], t, issue)

    x = _unpack_words([x_ref[_token_rows(0, t, rpt, j), :] for j in range(rpt)]).astype(BF16)
    g = jnp.dot(x, wsg_ref[...], preferred_element_type=F32)
    u = jnp.dot(x, wsu_ref[...], preferred_element_type=F32)
    h = (g * _sigmoid(g)) * u
    sh_ref[...] = jnp.dot(h.astype(BF16), wsd_ref[...], preferred_element_type=F32).astype(sh_ref.dtype)

    for _ in range(TOP_K):
        pltpu.make_async_copy(x_ref, xs_hbm.at[pl.ds(0, t * rpt), :], sem).wait()


def _dispatch(dest3, x1p, wsg, wsu, wsd, n, d, t, pad_tokens):
    rpt = _rows_per_token(d)
    const = lambda shape: pl.BlockSpec(shape, lambda i: (0, 0), pipeline_mode=pl.Buffered(1))
    return pl.pallas_call(
        functools.partial(_dispatch_kernel, t=t, rpt=rpt, n_steps=n // t, pad_rows=pad_tokens * rpt),
        out_shape=(jax.ShapeDtypeStruct(((n * TOP_K + pad_tokens) * rpt, LANES), U32),
                   jax.ShapeDtypeStruct((n, d), BF16)),
        grid=(n // t,),
        in_specs=[pl.BlockSpec(memory_space=pl.ANY), pl.BlockSpec((t * rpt, LANES), lambda i: (i, 0)),
                  const(wsg.shape), const(wsu.shape), const(wsd.shape)],
        out_specs=(pl.BlockSpec(memory_space=pl.ANY), pl.BlockSpec((t, d), lambda i: (i, 0))),
        scratch_shapes=[pltpu.SMEM((2, TOP_K * t // LANES, LANES), jnp.int32),
                        pltpu.VMEM((pad_tokens * rpt, LANES), U32),
                        pltpu.SemaphoreType.DMA((2,)), pltpu.SemaphoreType.DMA, pltpu.SemaphoreType.DMA],
        compiler_params=_params(("arbitrary",)),
        name="moe_dispatch",
    )(dest3, x1p, wsg, wsu, wsd)


def _expert_kernel(r0_ref, we_ref, nw_ref, nx_ref, par_ref, half_ref, xs_hbm, wg_hbm, wu_hbm, wd_hbm, o_hbm,
                   xin_ref, out_ref, wgf_ref, wuf_ref, wdf_ref, wgb_ref, wub_ref, wdb_ref,
                   wsem, isem, osem, *, tm, rpt):
    w = pl.program_id(0)
    nw = nw_ref[0]
    rows = tm * rpt

    def weight_copies(expert, buf):
        return [pltpu.make_async_copy(src.at[expert], dst.at[buf], wsem.at[buf])
                for src, dst in ((wg_hbm, wgf_ref), (wu_hbm, wuf_ref), (wd_hbm, wdf_ref))]

    def rows_in(item, buf):
        r0 = pl.multiple_of(r0_ref[item] * rpt, rpt)
        return pltpu.make_async_copy(xs_hbm.at[pl.ds(r0, rows), :], xin_ref.at[buf], isem.at[buf])

    def rows_out(item, buf):
        r0 = pl.multiple_of(r0_ref[item] * rpt, rpt)
        return pltpu.make_async_copy(out_ref.at[buf], o_hbm.at[pl.ds(r0, rows), :], osem.at[buf])

    @pl.when(w < nw)
    def _():
        e = we_ref[w]
        pw = jnp.maximum(w - 1, 0)
        buf_io = w % 2

        @pl.when(w == 0)
        def _():
            rows_in(0, 0).start()

        @pl.when(w + 1 < nw)
        def _():
            rows_in(w + 1, 1 - buf_io).start()

        @pl.when((w == 0) | (e != we_ref[pw]))
        def _():
            buf = par_ref[w]
            nxt = nx_ref[w]

            @pl.when(w == 0)
            def _():
                for c in weight_copies(e, buf):
                    c.start(priority=1)

            @pl.when(nxt >= 0)
            def _():
                for c in weight_copies(nxt, 1 - buf):
                    c.start(priority=1)

            for c in weight_copies(e, buf):
                c.wait()
            wgb_ref[...] = wgf_ref[buf].astype(BF16)
            wub_ref[...] = wuf_ref[buf].astype(BF16)
            wdb_ref[...] = wdf_ref[buf].astype(BF16)

        rows_in(w, buf_io).wait()
        xin = xin_ref.at[buf_io]
        out = out_ref.at[buf_io]

        def swiglu_rows(m):
            x = _unpack_words([xin[_token_rows(0, m, rpt, j), :] for j in range(rpt)]).astype(BF16)
            g = jnp.dot(x, wgb_ref[...], preferred_element_type=F32)
            u = jnp.dot(x, wub_ref[...], preferred_element_type=F32)
            h = (g * _sigmoid(g)) * u
            _store_packed(out, 0, jnp.dot(h.astype(BF16), wdb_ref[...], preferred_element_type=F32))

        @pl.when(half_ref[w] == 0)
        def _():
            swiglu_rows(tm)

        @pl.when(half_ref[w] != 0)
        def _():
            swiglu_rows(tm // 2)
            out[pl.ds(tm // 2 * rpt, tm // 2 * rpt), :] = jnp.zeros((tm // 2 * rpt, LANES), U32)

        @pl.when(w > 0)
        def _():
            rows_out(pw, 1 - buf_io).wait()

        rows_out(w, buf_io).start()

        @pl.when(w == nw - 1)
        def _():
            rows_out(w, buf_io).wait()


def _experts(work, xs, w_gate, w_up, w_down, tm):
    _, d, de = w_gate.shape
    rpt = _rows_per_token(d)
    n_work = work[0].shape[0]
    hbm = pl.BlockSpec(memory_space=pl.ANY)
    return pl.pallas_call(
        functools.partial(_expert_kernel, tm=tm, rpt=rpt),
        out_shape=jax.ShapeDtypeStruct(xs.shape, U32),
        grid_spec=pltpu.PrefetchScalarGridSpec(
            num_scalar_prefetch=len(work),
            grid=(n_work,),
            in_specs=[hbm, hbm, hbm, hbm],
            out_specs=hbm,
            scratch_shapes=[pltpu.VMEM((2, tm * rpt, LANES), U32), pltpu.VMEM((2, tm * rpt, LANES), U32),
                            pltpu.VMEM((2, d, de), F32), pltpu.VMEM((2, d, de), F32), pltpu.VMEM((2, de, d), F32),
                            pltpu.VMEM((d, de), BF16), pltpu.VMEM((d, de), BF16), pltpu.VMEM((de, d), BF16),
                            pltpu.SemaphoreType.DMA((2,)), pltpu.SemaphoreType.DMA((2,)),
                            pltpu.SemaphoreType.DMA((2,))],
        ),
        compiler_params=_params(("arbitrary",)),
        name="moe_experts",
    )(*work, xs, w_gate, w_up, w_down)


def _combine_kernel(dest_hbm, x1_ref, wt_ref, sh_ref, o_hbm, g2_ref, b2_ref, y_ref,
                    dsm, gbuf, slab_ref, sem_idx, sem, *, t, tt, d, rpt, n_steps):
    i = pl.program_id(0)
    slot = i % 2

    def issue(step_slot):
        def gather(tl, slots):
            for k in range(TOP_K):
                _token_copy(o_hbm, slots[k], gbuf.at[step_slot], k * t + tl, sem.at[step_slot],
                            rpt).start(priority=k % 2)

        _for_each_token_slots(dsm.at[step_slot], t, gather)

    @pl.when(i == 0)
    def _():
        first = _slot_table_copy(dest_hbm, 0, dsm, 0, sem_idx)
        first.start()
        first.wait()
        issue(0)
        if n_steps > 1:
            _slot_table_copy(dest_hbm, 1, dsm, 1, sem_idx).start()

    @pl.when(i + 1 < n_steps)
    def _():
        _slot_table_copy(dest_hbm, i + 1, dsm, 1 - slot, sem_idx).wait()
        issue(1 - slot)

        @pl.when(i + 2 < n_steps)
        def _():
            _slot_table_copy(dest_hbm, i + 2, dsm, slot, sem_idx).start()

    gcur = gbuf.at[slot]
    pltpu.make_async_copy(o_hbm.at[pl.ds(0, TOP_K * t * rpt), :], gcur, sem.at[slot]).wait()

    acc_lo = [None] * rpt
    acc_hi = [None] * rpt
    for k in range(TOP_K):
        wk = wt_ref[:, k:k + 1]
        for j in range(rpt):
            words = gcur[_token_rows(k * t * rpt, t, rpt, j), :]
            lo = wk * pltpu.bitcast(words << 16, F32)
            hi = wk * pltpu.bitcast(words & U32(HI_MASK), F32)
            acc_lo[j] = lo if k == 0 else acc_lo[j] + lo
            acc_hi[j] = hi if k == 0 else acc_hi[j] + hi
    routed = jnp.concatenate(acc_lo + acc_hi, axis=1)

    v = DN_ALPHA * x1_ref[...] + (routed + sh_ref[...].astype(F32))
    mu = jnp.mean(v, axis=-1, keepdims=True)
    cen = v - mu
    var = jnp.mean(cen * cen, axis=-1, keepdims=True)
    y = cen * lax.rsqrt(var + LN_EPS) * g2_ref[...] + b2_ref[...]
    _store_batch_major(y_ref, slab_ref, y, tt, d)


def _combine(dest3, x1, wt, sh, o, g2, b2, n, d, t):
    tt = t // SUBLANES
    rpt = _rows_per_token(d)
    const = lambda shape: pl.BlockSpec(shape, lambda i: (0, 0), pipeline_mode=pl.Buffered(1))
    return pl.pallas_call(
        functools.partial(_combine_kernel, t=t, tt=tt, d=d, rpt=rpt, n_steps=n // t),
        out_shape=jax.ShapeDtypeStruct((SUBLANES, n // SUBLANES, d), F32),
        grid=(n // t,),
        in_specs=[
            pl.BlockSpec(memory_space=pl.ANY),
            pl.BlockSpec((t, d), lambda i: (i, 0)),
            pl.BlockSpec((t, TOP_K), lambda i: (i, 0)),
            pl.BlockSpec((t, d), lambda i: (i, 0)),
            pl.BlockSpec(memory_space=pl.ANY),
            const((1, d)), const((1, d)),
        ],
        out_specs=pl.BlockSpec((SUBLANES, tt, d), lambda i: (0, i, 0)),
        scratch_shapes=[pltpu.SMEM((2, TOP_K * t // LANES, LANES), jnp.int32), pltpu.VMEM((2, TOP_K * t * rpt, LANES), U32),
                        pltpu.VMEM((d // LANES, t, LANES), F32),
                        pltpu.SemaphoreType.DMA((2,)), pltpu.SemaphoreType.DMA((2,))],
        compiler_params=_params(("arbitrary",)),
        name="moe_combine_ln2",
    )(dest3, x1, wt, sh, o, g2, b2)


def _work_items(counts, m, tm):
    e = counts.shape[0]
    end = jnp.cumsum(counts)
    start = end - counts
    nchunk = (counts + tm - 1) // tm
    cum = jnp.cumsum(nchunk)
    off = cum - nchunk
    n_work = m // tm + e
    w = jnp.arange(n_work, dtype=jnp.int32)
    we = jnp.minimum(jnp.sum(cum[None, :] <= w[:, None], axis=1), e - 1).astype(jnp.int32)
    chunk = w - off[we]
    row0 = (start[we] + chunk * tm).astype(jnp.int32)
    nw = cum[-1].astype(jnp.int32)
    row0 = jnp.where(w < nw, row0, 0)
    rem = counts - (nchunk - 1) * tm
    half = ((chunk == nchunk[we] - 1) & (rem[we] <= tm // 2)).astype(jnp.int32)
    ids = jnp.arange(e, dtype=jnp.int32)
    later = lax.cummin(jnp.where(counts > 0, ids, e)[::-1])[::-1]
    nxt_of = jnp.concatenate([later[1:], jnp.full((1,), e, jnp.int32)])
    nxt_of = jnp.where(nxt_of >= e, -1, nxt_of).astype(jnp.int32)
    ordinal = jnp.cumsum((counts > 0).astype(jnp.int32)) - 1
    work = (row0, we, nw.reshape(1), nxt_of[we], (ordinal[we] % 2).astype(jnp.int32), half)
    return work, start.astype(jnp.int32)


def kernel(x, w_in, b_in, conv_w, conv_b, lru_wa, lru_ba, lru_wx, lru_bx, lru_lambda, w_rnn_br, w_attn_br, w_out, ln1_g, ln1_b, w_router, router_bias, w_gate, w_up, w_down, ws_gate, ws_up, ws_down, ln2_g, ln2_b):
    bsz, s, d = x.shape
    assert bsz == SUBLANES, "time-major rows need batch == 8"
    assert DEPTH == 1 and w_in.shape[0] == 1
    n = bsz * s
    c = lru_lambda.shape[1]
    a = HEADS_PER_GROUP * len(ATTN_GROUPS) * HEAD_DIM
    assert c == d and w_in.shape[2] == 2 * c + 3 * a + 2 * d
    layer = 0

    gw = HEADS_PER_GROUP * HEAD_DIM
    src = (0, c, 2 * c, 2 * c + a, 2 * c + 2 * a, 2 * c + 3 * a, 2 * c + 3 * a + d, 2 * c + 3 * a + 2 * d)
    part = lambda v, p, lo=0, hi=None: v[..., src[p]:src[p + 1]][..., lo:hi]
    cat = lambda v, pieces: jnp.concatenate([part(v, *p) for p in pieces], axis=-1)
    main = ((0,), (1,), (5,), (6,))
    w_main = cat(w_in[layer], main).astype(BF16)
    b_main = cat(b_in[layer], main).reshape(1, -1)
    col_u, col_g, col_gr, col_ga = 0, c, 2 * c, 2 * c + d
    row = lambda v: v.reshape(1, -1)

    z, xb = _in_proj(x, w_main, b_main, n, d)

    yr = _lru(z, conv_w[layer], row(conv_b[layer]), lru_wa[layer].astype(BF16), row(lru_ba[layer]),
              lru_wx[layer].astype(BF16), row(lru_bx[layer]), row(lru_lambda[layer]), n, c, col_u, col_g)

    os_, sts = [], []
    for g in range(len(ATTN_GROUPS)):
        cols = tuple((p, g * gw, (g + 1) * gw) for p in (2, 3, 4))
        qkv = _qkv_proj(xb, cat(w_in[layer], cols).astype(BF16), cat(b_in[layer], cols).reshape(1, -1), g, n, d)
        o, st = _attn_group(qkv, g, n)
        os_.append(o)
        sts.append(st)

    x1, x1p = _merge(x, yr, z, os_, sts, w_rnn_br[layer].astype(BF16), w_attn_br[layer].astype(BF16),
                     w_out[layer].astype(BF16), row(ln1_g[layer]), row(ln1_b[layer]), n, d, col_gr, col_ga)

    n_exp = w_router.shape[2]
    idx, wts, rank, cnt = _router(x1, w_router[layer].T.astype(BF16), router_bias[layer].reshape(n_exp, 1), n, d)

    tm = 256
    t_tok = _pick(n, (256, 128))
    m = n * TOP_K
    counts = cnt[:, 0].astype(jnp.int32)
    work, start = _work_items(counts, m, tm)
    dest3 = _dest(idx, rank, start.astype(F32).reshape(n_exp, 1), n, t_tok)
    dest3 = dest3.transpose(0, 2, 1).reshape(n // t_tok, t_tok * TOP_K // LANES, LANES)

    xs, shared = _dispatch(dest3, x1p, ws_gate[layer].astype(BF16), ws_up[layer].astype(BF16),
                           ws_down[layer].astype(BF16), n, d, t_tok, pad_tokens=tm)
    o = _experts(work, xs, w_gate[layer], w_up[layer], w_down[layer], tm)
    y = _combine(dest3, x1, wts.T, shared, o, row(ln2_g[layer]), row(ln2_b[layer]), n, d, t_tok)
    return y
```

```python
import functools

import jax
import jax.numpy as jnp
from jax import lax
from jax.experimental import pallas as pl
from jax.experimental.pallas import tpu as pltpu

HEAD_DIM = 128
ATTN_GROUPS = ((128, 1), (512, 4), (2048, 16))
HEADS_PER_GROUP = 4
CONV_WIDTH = 4
LRU_C = 8.0
N_GROUP = 8
TOPK_GROUP = 4
TOP_K = 8
ROUTED_SCALE = 2.5
DEPTH = 1
DN_ALPHA = (2 * DEPTH) ** 0.25
LN_EPS = 1e-5

SUBLANES = 8
LANES = 128
V7X_VMEM_LIMIT_BYTES = 56 * 1024 * 1024

F32 = jnp.float32
BF16 = jnp.bfloat16


def _pick(n, cands):
    for c in cands:
        if n % c == 0:
            return c
    raise ValueError(f"no tile in {cands} divides {n}")


LOG2E = 1.4426950408889634


def _sigmoid(x):
    return 1.0 / (1.0 + jnp.exp2(x * (-LOG2E)))


def _params(sem, vmem=V7X_VMEM_LIMIT_BYTES):
    return pltpu.CompilerParams(dimension_semantics=sem, vmem_limit_bytes=vmem)


def _time_major_rows(x_ref, slab_ref, tt, d):
    for b in range(SUBLANES):
        for j in range(d // LANES):
            slab_ref[j, pl.ds(b, tt, stride=SUBLANES), :] = x_ref[b, :, j * LANES:(j + 1) * LANES]
    return jnp.concatenate([slab_ref[j] for j in range(d // LANES)], axis=1)


def _store_batch_major(y_ref, slab_ref, y, tt, d):
    for j in range(d // LANES):
        slab_ref[j] = y[:, j * LANES:(j + 1) * LANES]
    for b in range(SUBLANES):
        for j in range(d // LANES):
            y_ref[b, :, j * LANES:(j + 1) * LANES] = slab_ref[j, pl.ds(b, tt, stride=SUBLANES), :]


U32 = jnp.uint32
HI_MASK = 0xFFFF0000


def _rows_per_token(d):
    assert d % (2 * LANES) == 0
    return d // (2 * LANES)


def _token_rows(base, m, rpt, j):
    return pl.ds(base + j, m, stride=rpt)


def _pack_words(x):
    half = x.shape[1] // 2
    bits = pltpu.bitcast(x.astype(BF16).astype(F32), U32)
    words = (bits[:, :half] >> 16) | (bits[:, half:] & U32(HI_MASK))
    return [words[:, j * LANES:(j + 1) * LANES] for j in range(half // LANES)]


def _store_packed(ref, base, x):
    chunks = _pack_words(x)
    for j, chunk in enumerate(chunks):
        ref[_token_rows(base, x.shape[0], len(chunks), j), :] = chunk


def _unpack_words(chunks):
    lo = [pltpu.bitcast(c << 16, F32) for c in chunks]
    hi = [pltpu.bitcast(c & U32(HI_MASK), F32) for c in chunks]
    return jnp.concatenate(lo + hi, axis=1)


def _in_proj_kernel(x_ref, w_ref, b_ref, z_ref, xb_ref, slab_ref, *, tt, d):
    @pl.when(pl.program_id(1) == 0)
    def _():
        for b in range(SUBLANES):
            for j in range(d // LANES):
                slab_ref[j, pl.ds(b, tt, stride=SUBLANES), :] = x_ref[b, :, j * LANES:(j + 1) * LANES]
        for j in range(d // LANES):
            xb_ref[:, j * LANES:(j + 1) * LANES] = slab_ref[j].astype(BF16)

    z = jnp.dot(xb_ref[...], w_ref[...], preferred_element_type=F32) + b_ref[...]
    z_ref[...] = z.astype(z_ref.dtype)


def _in_proj(x, w, b, n, d):
    d_out = w.shape[1]
    tm = _pick(n, (1024, 512, 256, 128, 64, 32, 16))
    tn = _pick(d_out, (1024, 512, 256, 128))
    tt = tm // SUBLANES
    return pl.pallas_call(
        functools.partial(_in_proj_kernel, tt=tt, d=d),
        out_shape=(jax.ShapeDtypeStruct((n, d_out), BF16), jax.ShapeDtypeStruct((n, d), BF16)),
        grid=(n // tm, d_out // tn),
        in_specs=[
            pl.BlockSpec((SUBLANES, tt, d), lambda i, j: (0, i, 0)),
            pl.BlockSpec((d, tn), lambda i, j: (0, j)),
            pl.BlockSpec((1, tn), lambda i, j: (0, j)),
        ],
        out_specs=(pl.BlockSpec((tm, tn), lambda i, j: (i, j)),
                   pl.BlockSpec((tm, d), lambda i, j: (i, 0))),
        scratch_shapes=[pltpu.VMEM((d // LANES, tm, LANES), F32)],
        compiler_params=_params(("arbitrary", "arbitrary")),
        name="in_proj",
    )(x, w, b)


BF16_SUBLANES = 16


def _qkv_kernel(xb_ref, w_ref, b_ref, p_ref, o_ref, *, stride):
    r = (jnp.dot(xb_ref[...], w_ref[...], preferred_element_type=F32) + b_ref[...]).astype(BF16)
    sb = p_ref.shape[0]
    for s in range(r.shape[0] // (2 * sb)):
        halves = [jnp.dot(p_ref[...], r[(2 * s + i) * sb:(2 * s + i + 1) * sb, :], preferred_element_type=F32)
                  for i in range(2)]
        for c in range(stride):
            tile = jnp.concatenate([hv[c * SUBLANES:(c + 1) * SUBLANES, :] for hv in halves], axis=0)
            o_ref[c, s * BF16_SUBLANES:(s + 1) * BF16_SUBLANES, :] = tile.astype(BF16)


def _qkv_proj(xb, w, b, g, n, d):
    _, dil = ATTN_GROUPS[g]
    stride = SUBLANES * dil
    gw = HEADS_PER_GROUP * HEAD_DIM
    sb = SUBLANES * stride
    tm = max(_pick(n, (1024, 512, 256)), 2 * sb)
    per = tm // stride
    new = jnp.arange(sb, dtype=jnp.int32)
    old = (new % SUBLANES) * stride + new // SUBLANES
    perm = (old[:, None] == jnp.arange(sb, dtype=jnp.int32)[None, :]).astype(BF16)
    return pl.pallas_call(
        functools.partial(_qkv_kernel, stride=stride),
        out_shape=jax.ShapeDtypeStruct((stride, n // stride, 3 * gw), BF16),
        grid=(n // tm, 3),
        in_specs=[
            pl.BlockSpec((tm, d), lambda i, j: (i, 0)),
            pl.BlockSpec((d, gw), lambda i, j: (0, j)),
            pl.BlockSpec((1, gw), lambda i, j: (0, j)),
            pl.BlockSpec((sb, sb), lambda i, j: (0, 0), pipeline_mode=pl.Buffered(1)),
        ],
        out_specs=pl.BlockSpec((stride, per, gw), lambda i, j: (0, i, j)),
        compiler_params=_params(("arbitrary", "arbitrary")),
        name=f"qkv_proj_g{g}",
    )(xb, w, b, perm)


def _lru_kernel(u_ref, g_ref, cw_ref, cb_ref, wa_ref, ba_ref, wx_ref, bx_ref, lam_ref, y_ref,
                carry_ref, h_ref, a_ref, b_ref, *, tt, nblk, bw):
    ti = pl.program_id(1)
    rows = tt * SUBLANES
    halo = SUBLANES * (CONV_WIDTH - 1)

    @pl.when(ti == 0)
    def _():
        carry_ref[...] = jnp.zeros_like(carry_ref)
        h_ref[...] = jnp.zeros_like(h_ref)

    u = u_ref[...].astype(F32)
    ext = jnp.concatenate([carry_ref[...], u], axis=0)
    uc = cb_ref[...] + cw_ref[0:1, :] * ext[0:rows, :]
    for j in range(1, CONV_WIDTH):
        uc = uc + cw_ref[j:j + 1, :] * ext[SUBLANES * j:SUBLANES * j + rows, :]
    carry_ref[...] = u[rows - halo:, :]

    ucb = uc.astype(BF16)
    r_parts, i_parts = [], []
    for kb in range(nblk):
        blk = ucb[:, kb * bw:(kb + 1) * bw]
        r_parts.append(jnp.dot(blk, wa_ref[kb], preferred_element_type=F32))
        i_parts.append(jnp.dot(blk, wx_ref[kb], preferred_element_type=F32))
    r = _sigmoid(jnp.concatenate(r_parts, axis=1) + ba_ref[...])
    ig = _sigmoid(jnp.concatenate(i_parts, axis=1) + bx_ref[...])

    nl = -lam_ref[...]
    softplus = jnp.maximum(nl, 0.0) + jnp.log(1.0 + jnp.exp(-jnp.abs(nl)))
    a = jnp.exp2(r * ((-LRU_C * LOG2E) * softplus))
    gap = 1.0 - a * a
    mult = jnp.where(gap > 0.0, gap * lax.rsqrt(gap), 0.0)
    gated = uc * ig
    a_ref[...] = a
    b_ref[...] = gated * mult

    @pl.when(ti == 0)
    def _():
        b_ref[0:SUBLANES, :] = gated[0:SUBLANES, :]

    def step(t, h):
        r0 = pl.multiple_of(t * SUBLANES, SUBLANES)
        h = a_ref[pl.ds(r0, SUBLANES), :] * h + b_ref[pl.ds(r0, SUBLANES), :]
        b_ref[pl.ds(r0, SUBLANES), :] = h
        return h

    h_ref[...] = lax.fori_loop(0, tt, step, h_ref[...], unroll=8)

    g = g_ref[...].astype(F32)
    gelu = 0.5 * g * (1.0 + jnp.tanh(0.7978845608028654 * (g + 0.044715 * (g * g * g))))
    y_ref[...] = (b_ref[...] * gelu).astype(y_ref.dtype)


def _lru(z, conv_w, conv_b, wa, ba, wx, bx, lam, n, c, col_u, col_g):
    nblk_total, bw, _ = wa.shape
    ct = _pick(c, (512, 256, 128))
    tt = _pick(n // SUBLANES, (128, 64, 32, 16, 8))
    rows = tt * SUBLANES
    nblk = ct // bw
    vec = lambda: pl.BlockSpec((1, ct), lambda ci, ti: (0, ci))
    return pl.pallas_call(
        functools.partial(_lru_kernel, tt=tt, nblk=nblk, bw=bw),
        out_shape=jax.ShapeDtypeStruct((n, c), BF16),
        grid=(c // ct, n // rows),
        in_specs=[
            pl.BlockSpec((rows, ct), lambda ci, ti: (ti, col_u // ct + ci)),
            pl.BlockSpec((rows, ct), lambda ci, ti: (ti, col_g // ct + ci)),
            pl.BlockSpec((CONV_WIDTH, ct), lambda ci, ti: (0, ci)),
            vec(),
            pl.BlockSpec((nblk, bw, bw), lambda ci, ti: (ci, 0, 0)),
            vec(),
            pl.BlockSpec((nblk, bw, bw), lambda ci, ti: (ci, 0, 0)),
            vec(),
            vec(),
        ],
        out_specs=pl.BlockSpec((rows, ct), lambda ci, ti: (ti, ci)),
        scratch_shapes=[
            pltpu.VMEM((SUBLANES * (CONV_WIDTH - 1), ct), F32),
            pltpu.VMEM((SUBLANES, ct), F32),
            pltpu.VMEM((rows, ct), F32),
            pltpu.VMEM((rows, ct), F32),
        ],
        compiler_params=_params(("arbitrary", "arbitrary")),
        name="rg_lru",
    )(z, z, conv_w, conv_b, wa, ba, wx, bx, lam)


def _attn_kernel(q_ref, k_ref, v_ref, o_ref, st_ref, *, band, dil, slopes, qblocks):
    qi = lax.broadcasted_iota(jnp.int32, (band, 2 * band), 0)
    kj = lax.broadcasted_iota(jnp.int32, (band, 2 * band), 1)
    rel = qi - kj
    scale = HEAD_DIM ** -0.5
    nt = (((1,), (1,)), ((), ()))
    lane = lax.broadcasted_iota(jnp.int32, (band, LANES), 1)

    for qq in range(qblocks):
        n = pl.program_id(1) * qblocks + qq
        win0 = pl.multiple_of(jnp.maximum(n - 1, 0) * band, band)
        own = jnp.where(n > 0, band, 0)
        rows = slice(qq * band, (qq + 1) * band)
        q = q_ref[rows, :]
        kw = k_ref[pl.ds(win0, 2 * band), :]
        vw = v_ref[pl.ds(win0, 2 * band), :]
        dist_i = rel + own
        valid = (dist_i >= 0) & (dist_i <= band)
        dist = dist_i.astype(F32)

        outs = []
        stat = jnp.zeros((band, LANES), F32)
        for h in range(HEADS_PER_GROUP):
            sl = slice(h * HEAD_DIM, (h + 1) * HEAD_DIM)
            bias = slopes[h] * dil
            s = lax.dot_general(q[:, sl], kw[:, sl], nt, preferred_element_type=F32)
            s = jnp.where(valid, s * scale - bias * dist, -jnp.inf)
            m = jnp.max(s, axis=1, keepdims=True)
            p = jnp.exp(s - m)
            l = jnp.sum(p, axis=1, keepdims=True)
            acc = jnp.dot(p.astype(BF16), vw[:, sl], preferred_element_type=F32)
            outs.append(acc / l)
            stat = jnp.where(lane == h, m + jnp.log(l), stat)
        o_ref[rows, :] = jnp.concatenate(outs, axis=1).astype(o_ref.dtype)
        st_ref[rows, :] = stat


def _attn_group(qkv, g, n):
    window, dil = ATTN_GROUPS[g]
    band = window // dil
    stride = SUBLANES * dil
    lp = n // stride
    gw = HEADS_PER_GROUP * HEAD_DIM
    n_heads = HEADS_PER_GROUP * len(ATTN_GROUPS)
    slopes = tuple(2.0 ** (-8.0 * (g * HEADS_PER_GROUP + h + 1) / n_heads) for h in range(HEADS_PER_GROUP))
    qblocks = _pick(lp // band, (4, 2, 1))
    tq = qblocks * band
    o, st = pl.pallas_call(
        functools.partial(_attn_kernel, band=band, dil=float(dil), slopes=slopes, qblocks=qblocks),
        out_shape=(jax.ShapeDtypeStruct((lp, stride * gw), BF16),
                   jax.ShapeDtypeStruct((lp, stride * LANES), F32)),
        grid=(stride, lp // tq),
        in_specs=[
            pl.BlockSpec((None, tq, gw), lambda c, i: (c, i, 0)),
            pl.BlockSpec((None, lp, gw), lambda c, i: (c, 0, 1)),
            pl.BlockSpec((None, lp, gw), lambda c, i: (c, 0, 2)),
        ],
        out_specs=(pl.BlockSpec((tq, gw), lambda c, i: (i, c)),
                   pl.BlockSpec((tq, LANES), lambda c, i: (i, c))),
        compiler_params=_params(("arbitrary", "arbitrary")),
        name=f"dilated_attn_g{g}",
    )(qkv, qkv, qkv)
    return o.reshape(n, gw), st.reshape(n, LANES)


def _merge_kernel(x_ref, yr_ref, gr_ref, ga_ref, o0_ref, o1_ref, o2_ref, s0_ref, s1_ref, s2_ref,
                  wr_ref, wa_ref, wo_ref, g1_ref, b1_ref, x1_ref, x1p_ref, slab_ref, *, tt, d):
    t1 = jnp.dot(yr_ref[...], wr_ref[...], preferred_element_type=F32)
    merged = _sigmoid(gr_ref[...].astype(F32)) * t1

    stats = [s0_ref[...], s1_ref[...], s2_ref[...]]
    outs = [o0_ref, o1_ref, o2_ref]
    heads = []
    for h in range(HEADS_PER_GROUP):
        lse = [s[:, h:h + 1] for s in stats]
        mx = jnp.maximum(jnp.maximum(lse[0], lse[1]), lse[2])
        e = [jnp.exp(v - mx) for v in lse]
        tot = e[0] + e[1] + e[2]
        acc = None
        for gi in range(len(ATTN_GROUPS)):
            term = (e[gi] / tot) * outs[gi][:, h * HEAD_DIM:(h + 1) * HEAD_DIM].astype(F32)
            acc = term if acc is None else acc + term
        heads.append(acc)
    oa = jnp.concatenate(heads, axis=1).astype(BF16)
    t2 = jnp.dot(oa, wa_ref[...], preferred_element_type=F32)
    merged = merged + _sigmoid(ga_ref[...].astype(F32)) * t2
    mix = jnp.dot(merged.astype(BF16), wo_ref[...], preferred_element_type=F32)

    v = DN_ALPHA * _time_major_rows(x_ref, slab_ref, tt, d) + mix
    mu = jnp.mean(v, axis=-1, keepdims=True)
    cen = v - mu
    var = jnp.mean(cen * cen, axis=-1, keepdims=True)
    x1 = cen * lax.rsqrt(var + LN_EPS) * g1_ref[...] + b1_ref[...]
    x1_ref[...] = x1
    _store_packed(x1p_ref, 0, x1)


def _merge(x, yr, z, os_, sts, wr, wa, wo, g1, b1, n, d, col_gr, col_ga):
    tm = _pick(n, (256, 128, 64))
    tt = tm // SUBLANES
    gw = HEADS_PER_GROUP * HEAD_DIM
    rpt = _rows_per_token(d)
    const = lambda shape: pl.BlockSpec(shape, lambda i: (0, 0), pipeline_mode=pl.Buffered(1))
    row = lambda w: pl.BlockSpec((tm, w), lambda i: (i, 0))
    return pl.pallas_call(
        functools.partial(_merge_kernel, tt=tt, d=d),
        out_shape=(jax.ShapeDtypeStruct((n, d), F32), jax.ShapeDtypeStruct((n * rpt, LANES), U32)),
        grid=(n // tm,),
        in_specs=[
            pl.BlockSpec((SUBLANES, tt, d), lambda i: (0, i, 0)),
            row(yr.shape[1]),
            pl.BlockSpec((tm, d), lambda i: (i, col_gr // d)),
            pl.BlockSpec((tm, d), lambda i: (i, col_ga // d)),
            row(gw), row(gw), row(gw),
            row(LANES), row(LANES), row(LANES),
            const(wr.shape), const(wa.shape), const(wo.shape),
            const((1, d)), const((1, d)),
        ],
        out_specs=(row(d), pl.BlockSpec((tm * rpt, LANES), lambda i: (i, 0))),
        scratch_shapes=[pltpu.VMEM((d // LANES, tm, LANES), F32)],
        compiler_params=_params(("arbitrary",)),
        name="merge_ln1",
    )(x, yr, z, z, *os_, *sts, wr, wa, wo, g1, b1)


def _router_kernel(x_ref, wr_ref, bias_ref, idx_ref, w_ref, rank_ref, cnt_ref, base_ref, tri_ref, *, t, e):
    i = pl.program_id(0)

    @pl.when(i == 0)
    def _():
        base_ref[...] = jnp.zeros_like(base_ref)
        rr = lax.broadcasted_iota(jnp.int32, (t, t), 0)
        cc = lax.broadcasted_iota(jnp.int32, (t, t), 1)
        tri_ref[...] = jnp.where(rr < cc, 1.0, 0.0).astype(BF16)

    nt = (((1,), (1,)), ((), ()))
    logits = lax.dot_general(wr_ref[...], x_ref[...].astype(BF16), nt, preferred_element_type=F32)
    scores = _sigmoid(logits)
    biased = scores + bias_ref[...]

    per = e // N_GROUP
    sub = lax.broadcasted_iota(jnp.int32, (per, t), 0).astype(F32)
    blocks, gscore = [], []
    for g in range(N_GROUP):
        blk = biased[g * per:(g + 1) * per, :]
        m1 = jnp.max(blk, axis=0, keepdims=True)
        first = jnp.min(jnp.where(blk == m1, sub, float(per)), axis=0, keepdims=True)
        m2 = jnp.max(jnp.where(sub == first, -jnp.inf, blk), axis=0, keepdims=True)
        blocks.append(blk)
        gscore.append(m1 + m2)
    masked_blocks = []
    for g in range(N_GROUP):
        beaten = jnp.zeros((1, t), F32)
        for g2 in range(N_GROUP):
            if g2 == g:
                continue
            wins = (gscore[g2] > gscore[g]) | ((gscore[g2] == gscore[g]) & (g2 < g))
            beaten = beaten + jnp.where(wins, 1.0, 0.0)
        keep = beaten < float(TOPK_GROUP)
        masked_blocks.append(jnp.where(keep, blocks[g], -jnp.inf))
    masked = jnp.concatenate(masked_blocks, axis=0)

    eidx = lax.broadcasted_iota(jnp.int32, (e, t), 0).astype(F32)
    member = jnp.zeros((e, t), F32)
    sels, ws = [], []
    for _ in range(TOP_K):
        m = jnp.max(masked, axis=0, keepdims=True)
        sel = jnp.min(jnp.where(masked == m, eidx, float(e)), axis=0, keepdims=True)
        hit = eidx == sel
        ws.append(jnp.sum(jnp.where(hit, scores, 0.0), axis=0, keepdims=True))
        masked = jnp.where(hit, -jnp.inf, masked)
        member = jnp.where(hit, 1.0, member)
        sels.append(sel)
    wsum = ws[0]
    for k in range(1, TOP_K):
        wsum = wsum + ws[k]

    before = jnp.dot(member.astype(BF16), tri_ref[...], preferred_element_type=F32) + base_ref[...]
    for k in range(TOP_K):
        idx_ref[k:k + 1, :] = sels[k].astype(jnp.int32)
        w_ref[k:k + 1, :] = ws[k] / wsum * ROUTED_SCALE
        rk = jnp.sum(jnp.where(eidx == sels[k], before, 0.0), axis=0, keepdims=True)
        rank_ref[k:k + 1, :] = rk.astype(jnp.int32)
    base_ref[...] = base_ref[...] + jnp.sum(member, axis=1, keepdims=True)
    cnt_ref[...] = jnp.broadcast_to(base_ref[...], cnt_ref.shape)


def _router(x1, wrt, bias, n, d):
    e = wrt.shape[0]
    t = _pick(n, (512, 256, 128))
    tok = lambda dt: jax.ShapeDtypeStruct((TOP_K, n), dt)
    return pl.pallas_call(
        functools.partial(_router_kernel, t=t, e=e),
        out_shape=(tok(jnp.int32), tok(F32), tok(jnp.int32), jax.ShapeDtypeStruct((e, LANES), F32)),
        grid=(n // t,),
        in_specs=[
            pl.BlockSpec((t, d), lambda i: (i, 0)),
            pl.BlockSpec((e, d), lambda i: (0, 0)),
            pl.BlockSpec((e, 1), lambda i: (0, 0)),
        ],
        out_specs=(pl.BlockSpec((TOP_K, t), lambda i: (0, i)),
                   pl.BlockSpec((TOP_K, t), lambda i: (0, i)),
                   pl.BlockSpec((TOP_K, t), lambda i: (0, i)),
                   pl.BlockSpec((e, LANES), lambda i: (0, 0))),
        scratch_shapes=[pltpu.VMEM((e, 1), F32), pltpu.VMEM((t, t), BF16)],
        compiler_params=_params(("arbitrary",)),
        name="router_topk",
    )(x1, wrt, bias)


def _dest_kernel(idx_ref, rank_ref, start_ref, dest_ref, *, t, e):
    eidx = lax.broadcasted_iota(jnp.int32, (e, t), 0)
    for k in range(TOP_K):
        hit = eidx == idx_ref[k:k + 1, :]
        st = jnp.sum(jnp.where(hit, start_ref[...], 0.0), axis=0, keepdims=True)
        dest_ref[0, k:k + 1, :] = st.astype(jnp.int32) + rank_ref[k:k + 1, :]


def _dest(idx, rank, start_col, n, t):
    e = start_col.shape[0]
    return pl.pallas_call(
        functools.partial(_dest_kernel, t=t, e=e),
        out_shape=jax.ShapeDtypeStruct((n // t, TOP_K, t), jnp.int32),
        grid=(n // t,),
        in_specs=[
            pl.BlockSpec((TOP_K, t), lambda i: (0, i)),
            pl.BlockSpec((TOP_K, t), lambda i: (0, i)),
            pl.BlockSpec((e, 1), lambda i: (0, 0)),
        ],
        out_specs=pl.BlockSpec((1, TOP_K, t), lambda i: (i, 0, 0)),
        compiler_params=_params(("arbitrary",)),
        name="moe_dest",
    )(idx, rank, start_col)


def _token_copy(src_ref, src_tok, dst_ref, dst_tok, sem, rpt):
    s0 = pl.multiple_of(src_tok * rpt, rpt)
    d0 = pl.multiple_of(dst_tok * rpt, rpt)
    return pltpu.make_async_copy(src_ref.at[pl.ds(s0, rpt), :], dst_ref.at[pl.ds(d0, rpt), :], sem)


TOKENS_PER_SLOT_ROW = LANES // TOP_K


def _for_each_token_slots(table, t, fn):
    def body(r, c):
        for pair in range(0, TOKENS_PER_SLOT_ROW, 2):
            slots = [[table[r, (pair + p) * TOP_K + k] for k in range(TOP_K)] for p in range(2)]
            for p in range(2):
                fn(r * TOKENS_PER_SLOT_ROW + pair + p, slots[p])
        return c

    lax.fori_loop(0, t // TOKENS_PER_SLOT_ROW, body, 0)


def _slot_table_copy(dest_hbm, step, dsm, slot, sem_idx):
    return pltpu.make_async_copy(dest_hbm.at[step], dsm.at[slot], sem_idx.at[slot])


def _dispatch_kernel(dest_hbm, x_hbm, wsg_ref, wsu_ref, wsd_ref, xs_hbm, sh_ref, dsm, xring, zero_ref, sem_idx,
                     sem_in, sem, sem_pad, *, t, rpt, n_steps, pad_rows):
    i = pl.program_id(0)
    slot = i % 2
    rows = t * rpt

    def block_in(step, buf):
        r0 = pl.multiple_of(step * rows, rows)
        return pltpu.make_async_copy(x_hbm.at[pl.ds(r0, rows), :], xring.at[buf], sem_in.at[buf])

    @pl.when(i == 0)
    def _():
        _slot_table_copy(dest_hbm, 0, dsm, 0, sem_idx).start()
        block_in(0, 0).start()
        zero_ref[...] = jnp.zeros_like(zero_ref)
        pad = pltpu.make_async_copy(zero_ref, xs_hbm.at[pl.ds(n_steps * t * TOP_K * rpt, pad_rows), :], sem_pad)
        pad.start()
        pad.wait()

    _slot_table_copy(dest_hbm, i, dsm, slot, sem_idx).wait()
    block_in(i, i % 3).wait()

    @pl.when(i + 1 < n_steps)
    def _():
        _slot_table_copy(dest_hbm, i + 1, dsm, 1 - slot, sem_idx).start()
        block_in(i + 1, (i + 1) % 3).start()

    x_ref = xring.at[i % 3]

    def issue(tl, slots):
        for k in range(TOP_K):
            _token_copy(x_ref, tl, xs_hbm, slots[k], sem.at[slot], rpt).start(priority=k % 2)

    _for_each_token_slots(dsm.at[slot], t, issue)

    x = _unpack_words([x_ref[_token_rows(0, t, rpt, j), :] for j in range(rpt)]).astype(BF16)
    g = jnp.dot(x, wsg_ref[...], preferred_element_type=F32)
    u = jnp.dot(x, wsu_ref[...], preferred_element_type=F32)
    h = (g * _sigmoid(g)) * u
    sh_ref[...] = jnp.dot(h.astype(BF16), wsd_ref[...], preferred_element_type=F32).astype(sh_ref.dtype)

    def drain(parity):
        for _ in range(TOP_K):
            pltpu.make_async_copy(xring.at[0], xs_hbm.at[pl.ds(0, rows), :], sem.at[parity]).wait()

    @pl.when(i > 0)
    def _():
        drain(1 - slot)

    @pl.when(i == n_steps - 1)
    def _():
        drain(slot)


def _dispatch(dest3, x1p, wsg, wsu, wsd, n, d, t, pad_tokens):
    rpt = _rows_per_token(d)
    const = lambda shape: pl.BlockSpec(shape, lambda i: (0, 0), pipeline_mode=pl.Buffered(1))
    return pl.pallas_call(
        functools.partial(_dispatch_kernel, t=t, rpt=rpt, n_steps=n // t, pad_rows=pad_tokens * rpt),
        out_shape=(jax.ShapeDtypeStruct(((n * TOP_K + pad_tokens) * rpt, LANES), U32),
                   jax.ShapeDtypeStruct((n, d), BF16)),
        grid=(n // t,),
        in_specs=[pl.BlockSpec(memory_space=pl.ANY), pl.BlockSpec(memory_space=pl.ANY),
                  const(wsg.shape), const(wsu.shape), const(wsd.shape)],
        out_specs=(pl.BlockSpec(memory_space=pl.ANY), pl.BlockSpec((t, d), lambda i: (i, 0))),
        scratch_shapes=[pltpu.SMEM((2, TOP_K * t // LANES, LANES), jnp.int32),
                        pltpu.VMEM((3, t * rpt, LANES), U32),
                        pltpu.VMEM((pad_tokens * rpt, LANES), U32),
                        pltpu.SemaphoreType.DMA((2,)), pltpu.SemaphoreType.DMA((3,)),
                        pltpu.SemaphoreType.DMA((2,)), pltpu.SemaphoreType.DMA],
        compiler_params=_params(("arbitrary",)),
        name="moe_dispatch",
    )(dest3, x1p, wsg, wsu, wsd)


def _expert_kernel(r0_ref, we_ref, nw_ref, nx_ref, par_ref, half_ref, xs_hbm, wg_hbm, wu_hbm, wd_hbm, o_hbm,
                   xin_ref, out_ref, wgf_ref, wuf_ref, wdf_ref, wgb_ref, wub_ref, wdb_ref,
                   wsem, isem, osem, *, tm, rpt):
    w = pl.program_id(0)
    nw = nw_ref[0]
    rows = tm * rpt

    def weight_copies(expert, buf):
        return [pltpu.make_async_copy(src.at[expert], dst.at[buf], wsem.at[buf])
                for src, dst in ((wg_hbm, wgf_ref), (wu_hbm, wuf_ref), (wd_hbm, wdf_ref))]

    def rows_in(item, buf):
        r0 = pl.multiple_of(r0_ref[item] * rpt, rpt)
        return pltpu.make_async_copy(xs_hbm.at[pl.ds(r0, rows), :], xin_ref.at[buf], isem.at[buf])

    def rows_out(item, buf):
        r0 = pl.multiple_of(r0_ref[item] * rpt, rpt)
        return pltpu.make_async_copy(out_ref.at[buf], o_hbm.at[pl.ds(r0, rows), :], osem.at[buf])

    @pl.when(w < nw)
    def _():
        e = we_ref[w]
        pw = jnp.maximum(w - 1, 0)
        buf_io = w % 2

        @pl.when(w == 0)
        def _():
            rows_in(0, 0).start()

        @pl.when(w + 1 < nw)
        def _():
            rows_in(w + 1, 1 - buf_io).start()

        @pl.when((w == 0) | (e != we_ref[pw]))
        def _():
            buf = par_ref[w]
            nxt = nx_ref[w]

            @pl.when(w == 0)
            def _():
                for c in weight_copies(e, buf):
                    c.start(priority=1)

            @pl.when(nxt >= 0)
            def _():
                for c in weight_copies(nxt, 1 - buf):
                    c.start(priority=1)

            for c in weight_copies(e, buf):
                c.wait()
            wgb_ref[...] = wgf_ref[buf].astype(BF16)
            wub_ref[...] = wuf_ref[buf].astype(BF16)
            wdb_ref[...] = wdf_ref[buf].astype(BF16)

        rows_in(w, buf_io).wait()
        xin = xin_ref.at[buf_io]
        out = out_ref.at[buf_io]

        def swiglu_rows(m):
            x = _unpack_words([xin[_token_rows(0, m, rpt, j), :] for j in range(rpt)]).astype(BF16)
            g = jnp.dot(x, wgb_ref[...], preferred_element_type=F32)
            u = jnp.dot(x, wub_ref[...], preferred_element_type=F32)
            h = (g * _sigmoid(g)) * u
            _store_packed(out, 0, jnp.dot(h.astype(BF16), wdb_ref[...], preferred_element_type=F32))

        @pl.when(half_ref[w] == 0)
        def _():
            swiglu_rows(tm)

        @pl.when(half_ref[w] != 0)
        def _():
            swiglu_rows(tm // 2)
            out[pl.ds(tm // 2 * rpt, tm // 2 * rpt), :] = jnp.zeros((tm // 2 * rpt, LANES), U32)

        @pl.when(w > 0)
        def _():
            rows_out(pw, 1 - buf_io).wait()

        rows_out(w, buf_io).start()

        @pl.when(w == nw - 1)
        def _():
            rows_out(w, buf_io).wait()


def _experts(work, xs, w_gate, w_up, w_down, tm):
    _, d, de = w_gate.shape
    rpt = _rows_per_token(d)
    n_work = work[0].shape[0]
    hbm = pl.BlockSpec(memory_space=pl.ANY)
    return pl.pallas_call(
        functools.partial(_expert_kernel, tm=tm, rpt=rpt),
        out_shape=jax.ShapeDtypeStruct(xs.shape, U32),
        grid_spec=pltpu.PrefetchScalarGridSpec(
            num_scalar_prefetch=len(work),
            grid=(n_work,),
            in_specs=[hbm, hbm, hbm, hbm],
            out_specs=hbm,
            scratch_shapes=[pltpu.VMEM((2, tm * rpt, LANES), U32), pltpu.VMEM((2, tm * rpt, LANES), U32),
                            pltpu.VMEM((2, d, de), F32), pltpu.VMEM((2, d, de), F32), pltpu.VMEM((2, de, d), F32),
                            pltpu.VMEM((d, de), BF16), pltpu.VMEM((d, de), BF16), pltpu.VMEM((de, d), BF16),
                            pltpu.SemaphoreType.DMA((2,)), pltpu.SemaphoreType.DMA((2,)),
                            pltpu.SemaphoreType.DMA((2,))],
        ),
        compiler_params=_params(("arbitrary",)),
        name="moe_experts",
    )(*work, xs, w_gate, w_up, w_down)


def _combine_kernel(dest_hbm, x1_ref, wt_ref, sh_ref, o_hbm, g2_ref, b2_ref, y_ref,
                    dsm, gbuf, slab_ref, sem_idx, sem, *, t, tt, d, rpt, n_steps):
    i = pl.program_id(0)
    slot = i % 2

    def issue(step_slot):
        def gather(tl, slots):
            for k in range(TOP_K):
                _token_copy(o_hbm, slots[k], gbuf.at[step_slot], k * t + tl, sem.at[step_slot],
                            rpt).start(priority=k % 2)

        _for_each_token_slots(dsm.at[step_slot], t, gather)

    @pl.when(i == 0)
    def _():
        first = _slot_table_copy(dest_hbm, 0, dsm, 0, sem_idx)
        first.start()
        first.wait()
        issue(0)
        if n_steps > 1:
            _slot_table_copy(dest_hbm, 1, dsm, 1, sem_idx).start()

    @pl.when(i + 1 < n_steps)
    def _():
        _slot_table_copy(dest_hbm, i + 1, dsm, 1 - slot, sem_idx).wait()
        issue(1 - slot)

        @pl.when(i + 2 < n_steps)
        def _():
            _slot_table_copy(dest_hbm, i + 2, dsm, slot, sem_idx).start()

    gcur = gbuf.at[slot]
    pltpu.make_async_copy(o_hbm.at[pl.ds(0, TOP_K * t * rpt), :], gcur, sem.at[slot]).wait()

    acc_lo = [None] * rpt
    acc_hi = [None] * rpt
    for k in range(TOP_K):
        wk = wt_ref[:, k:k + 1]
        for j in range(rpt):
            words = gcur[_token_rows(k * t * rpt, t, rpt, j), :]
            lo = wk * pltpu.bitcast(words << 16, F32)
            hi = wk * pltpu.bitcast(words & U32(HI_MASK), F32)
            acc_lo[j] = lo if k == 0 else acc_lo[j] + lo
            acc_hi[j] = hi if k == 0 else acc_hi[j] + hi
    routed = jnp.concatenate(acc_lo + acc_hi, axis=1)

    v = DN_ALPHA * x1_ref[...] + (routed + sh_ref[...].astype(F32))
    mu = jnp.mean(v, axis=-1, keepdims=True)
    cen = v - mu
    var = jnp.mean(cen * cen, axis=-1, keepdims=True)
    y = cen * lax.rsqrt(var + LN_EPS) * g2_ref[...] + b2_ref[...]
    _store_batch_major(y_ref, slab_ref, y, tt, d)


def _combine(dest3, x1, wt, sh, o, g2, b2, n, d, t):
    tt = t // SUBLANES
    rpt = _rows_per_token(d)
    const = lambda shape: pl.BlockSpec(shape, lambda i: (0, 0), pipeline_mode=pl.Buffered(1))
    return pl.pallas_call(
        functools.partial(_combine_kernel, t=t, tt=tt, d=d, rpt=rpt, n_steps=n // t),
        out_shape=jax.ShapeDtypeStruct((SUBLANES, n // SUBLANES, d), F32),
        grid=(n // t,),
        in_specs=[
            pl.BlockSpec(memory_space=pl.ANY),
            pl.BlockSpec((t, d), lambda i: (i, 0)),
            pl.BlockSpec((t, TOP_K), lambda i: (i, 0)),
            pl.BlockSpec((t, d), lambda i: (i, 0)),
            pl.BlockSpec(memory_space=pl.ANY),
            const((1, d)), const((1, d)),
        ],
        out_specs=pl.BlockSpec((SUBLANES, tt, d), lambda i: (0, i, 0)),
        scratch_shapes=[pltpu.SMEM((2, TOP_K * t // LANES, LANES), jnp.int32), pltpu.VMEM((2, TOP_K * t * rpt, LANES), U32),
                        pltpu.VMEM((d // LANES, t, LANES), F32),
                        pltpu.SemaphoreType.DMA((2,)), pltpu.SemaphoreType.DMA((2,))],
        compiler_params=_params(("arbitrary",)),
        name="moe_combine_ln2",
    )(dest3, x1, wt, sh, o, g2, b2)


def _work_items(counts, m, tm):
    e = counts.shape[0]
    end = jnp.cumsum(counts)
    start = end - counts
    nchunk = (counts + tm - 1) // tm
    cum = jnp.cumsum(nchunk)
    off = cum - nchunk
    n_work = m // tm + e
    w = jnp.arange(n_work, dtype=jnp.int32)
    we = jnp.minimum(jnp.sum(cum[None, :] <= w[:, None], axis=1), e - 1).astype(jnp.int32)
    chunk = w - off[we]
    row0 = (start[we] + chunk * tm).astype(jnp.int32)
    nw = cum[-1].astype(jnp.int32)
    row0 = jnp.where(w < nw, row0, 0)
    rem = counts - (nchunk - 1) * tm
    half = ((chunk == nchunk[we] - 1) & (rem[we] <= tm // 2)).astype(jnp.int32)
    ids = jnp.arange(e, dtype=jnp.int32)
    later = lax.cummin(jnp.where(counts > 0, ids, e)[::-1])[::-1]
    nxt_of = jnp.concatenate([later[1:], jnp.full((1,), e, jnp.int32)])
    nxt_of = jnp.where(nxt_of >= e, -1, nxt_of).astype(jnp.int32)
    ordinal = jnp.cumsum((counts > 0).astype(jnp.int32)) - 1
    work = (row0, we, nw.reshape(1), nxt_of[we], (ordinal[we] % 2).astype(jnp.int32), half)
    return work, start.astype(jnp.int32)


def kernel(x, w_in, b_in, conv_w, conv_b, lru_wa, lru_ba, lru_wx, lru_bx, lru_lambda, w_rnn_br, w_attn_br, w_out, ln1_g, ln1_b, w_router, router_bias, w_gate, w_up, w_down, ws_gate, ws_up, ws_down, ln2_g, ln2_b):
    bsz, s, d = x.shape
    assert bsz == SUBLANES, "time-major rows need batch == 8"
    assert DEPTH == 1 and w_in.shape[0] == 1
    n = bsz * s
    c = lru_lambda.shape[1]
    a = HEADS_PER_GROUP * len(ATTN_GROUPS) * HEAD_DIM
    assert c == d and w_in.shape[2] == 2 * c + 3 * a + 2 * d
    layer = 0

    gw = HEADS_PER_GROUP * HEAD_DIM
    src = (0, c, 2 * c, 2 * c + a, 2 * c + 2 * a, 2 * c + 3 * a, 2 * c + 3 * a + d, 2 * c + 3 * a + 2 * d)
    part = lambda v, p, lo=0, hi=None: v[..., src[p]:src[p + 1]][..., lo:hi]
    cat = lambda v, pieces: jnp.concatenate([part(v, *p) for p in pieces], axis=-1)
    main = ((0,), (1,), (5,), (6,))
    w_main = cat(w_in[layer], main).astype(BF16)
    b_main = cat(b_in[layer], main).reshape(1, -1)
    col_u, col_g, col_gr, col_ga = 0, c, 2 * c, 2 * c + d
    row = lambda v: v.reshape(1, -1)

    z, xb = _in_proj(x, w_main, b_main, n, d)

    yr = _lru(z, conv_w[layer], row(conv_b[layer]), lru_wa[layer].astype(BF16), row(lru_ba[layer]),
              lru_wx[layer].astype(BF16), row(lru_bx[layer]), row(lru_lambda[layer]), n, c, col_u, col_g)

    os_, sts = [], []
    for g in range(len(ATTN_GROUPS)):
        cols = tuple((p, g * gw, (g + 1) * gw) for p in (2, 3, 4))
        qkv = _qkv_proj(xb, cat(w_in[layer], cols).astype(BF16), cat(b_in[layer], cols).reshape(1, -1), g, n, d)
        o, st = _attn_group(qkv, g, n)
        os_.append(o)
        sts.append(st)

    x1, x1p = _merge(x, yr, z, os_, sts, w_rnn_br[layer].astype(BF16), w_attn_br[layer].astype(BF16),
                     w_out[layer].astype(BF16), row(ln1_g[layer]), row(ln1_b[layer]), n, d, col_gr, col_ga)

    n_exp = w_router.shape[2]
    idx, wts, rank, cnt = _router(x1, w_router[layer].T.astype(BF16), router_bias[layer].reshape(n_exp, 1), n, d)

    tm = 256
    t_tok = _pick(n, (256, 128))
    m = n * TOP_K
    counts = cnt[:, 0].astype(jnp.int32)
    work, start = _work_items(counts, m, tm)
    dest3 = _dest(idx, rank, start.astype(F32).reshape(n_exp, 1), n, t_tok)
    dest3 = dest3.transpose(0, 2, 1).reshape(n // t_tok, t_tok * TOP_K // LANES, LANES)

    xs, shared = _dispatch(dest3, x1p, ws_gate[layer].astype(BF16), ws_up[layer].astype(BF16),
                           ws_down[layer].astype(BF16), n, d, t_tok, pad_tokens=tm)
    o = _experts(work, xs, w_gate[layer], w_up[layer], w_down[layer], tm)
    y = _combine(dest3, x1, wts.T, shared, o, row(ln2_g[layer]), row(ln2_b[layer]), n, d, t_tok)
    return y
```

```python
import functools

import jax
import jax.numpy as jnp
from jax import lax
from jax.experimental import pallas as pl
from jax.experimental.pallas import tpu as pltpu

HEAD_DIM = 128
ATTN_GROUPS = ((128, 1), (512, 4), (2048, 16))
HEADS_PER_GROUP = 4
CONV_WIDTH = 4
LRU_C = 8.0
N_GROUP = 8
TOPK_GROUP = 4
TOP_K = 8
ROUTED_SCALE = 2.5
DEPTH = 1
DN_ALPHA = (2 * DEPTH) ** 0.25
LN_EPS = 1e-5

SUBLANES = 8
LANES = 128
V7X_VMEM_LIMIT_BYTES = 56 * 1024 * 1024

F32 = jnp.float32
BF16 = jnp.bfloat16


def _pick(n, cands):
    for c in cands:
        if n % c == 0:
            return c
    raise ValueError(f"no tile in {cands} divides {n}")


LOG2E = 1.4426950408889634


def _sigmoid(x):
    return 1.0 / (1.0 + jnp.exp2(x * (-LOG2E)))


def _params(sem, vmem=V7X_VMEM_LIMIT_BYTES):
    return pltpu.CompilerParams(dimension_semantics=sem, vmem_limit_bytes=vmem)


def _time_major_rows(x_ref, slab_ref, tt, d):
    for b in range(SUBLANES):
        for j in range(d // LANES):
            slab_ref[j, pl.ds(b, tt, stride=SUBLANES), :] = x_ref[b, :, j * LANES:(j + 1) * LANES]
    return jnp.concatenate([slab_ref[j] for j in range(d // LANES)], axis=1)


def _store_batch_major(y_ref, slab_ref, y, tt, d):
    for j in range(d // LANES):
        slab_ref[j] = y[:, j * LANES:(j + 1) * LANES]
    for b in range(SUBLANES):
        for j in range(d // LANES):
            y_ref[b, :, j * LANES:(j + 1) * LANES] = slab_ref[j, pl.ds(b, tt, stride=SUBLANES), :]


U32 = jnp.uint32
HI_MASK = 0xFFFF0000


def _rows_per_token(d):
    assert d % (2 * LANES) == 0
    return d // (2 * LANES)


def _token_rows(base, m, rpt, j):
    return pl.ds(base + j, m, stride=rpt)


def _pack_words(x):
    half = x.shape[1] // 2
    bits = pltpu.bitcast(x.astype(BF16).astype(F32), U32)
    words = (bits[:, :half] >> 16) | (bits[:, half:] & U32(HI_MASK))
    return [words[:, j * LANES:(j + 1) * LANES] for j in range(half // LANES)]


def _store_packed(ref, base, x):
    chunks = _pack_words(x)
    for j, chunk in enumerate(chunks):
        ref[_token_rows(base, x.shape[0], len(chunks), j), :] = chunk


def _unpack_words(chunks):
    lo = [pltpu.bitcast(c << 16, F32) for c in chunks]
    hi = [pltpu.bitcast(c & U32(HI_MASK), F32) for c in chunks]
    return jnp.concatenate(lo + hi, axis=1)


def _in_proj_kernel(x_ref, w_ref, b_ref, z_ref, xb_ref, slab_ref, *, tt, d):
    @pl.when(pl.program_id(1) == 0)
    def _():
        for b in range(SUBLANES):
            for j in range(d // LANES):
                slab_ref[j, pl.ds(b, tt, stride=SUBLANES), :] = x_ref[b, :, j * LANES:(j + 1) * LANES]
        for j in range(d // LANES):
            xb_ref[:, j * LANES:(j + 1) * LANES] = slab_ref[j].astype(BF16)

    z = jnp.dot(xb_ref[...], w_ref[...], preferred_element_type=F32) + b_ref[...]
    z_ref[...] = z.astype(z_ref.dtype)


def _in_proj(x, w, b, n, d):
    d_out = w.shape[1]
    tm = _pick(n, (1024, 512, 256, 128, 64, 32, 16))
    tn = _pick(d_out, (1024, 512, 256, 128))
    tt = tm // SUBLANES
    return pl.pallas_call(
        functools.partial(_in_proj_kernel, tt=tt, d=d),
        out_shape=(jax.ShapeDtypeStruct((n, d_out), BF16), jax.ShapeDtypeStruct((n, d), BF16)),
        grid=(n // tm, d_out // tn),
        in_specs=[
            pl.BlockSpec((SUBLANES, tt, d), lambda i, j: (0, i, 0)),
            pl.BlockSpec((d, tn), lambda i, j: (0, j)),
            pl.BlockSpec((1, tn), lambda i, j: (0, j)),
        ],
        out_specs=(pl.BlockSpec((tm, tn), lambda i, j: (i, j)),
                   pl.BlockSpec((tm, d), lambda i, j: (i, 0))),
        scratch_shapes=[pltpu.VMEM((d // LANES, tm, LANES), F32)],
        compiler_params=_params(("arbitrary", "arbitrary")),
        name="in_proj",
    )(x, w, b)


BF16_SUBLANES = 16


def _qkv_kernel(xb_ref, w_ref, b_ref, p_ref, o_ref, *, stride):
    r = (jnp.dot(xb_ref[...], w_ref[...], preferred_element_type=F32) + b_ref[...]).astype(BF16)
    sb = p_ref.shape[0]
    for s in range(r.shape[0] // (2 * sb)):
        halves = [jnp.dot(p_ref[...], r[(2 * s + i) * sb:(2 * s + i + 1) * sb, :], preferred_element_type=F32)
                  for i in range(2)]
        for c in range(stride):
            tile = jnp.concatenate([hv[c * SUBLANES:(c + 1) * SUBLANES, :] for hv in halves], axis=0)
            o_ref[c, s * BF16_SUBLANES:(s + 1) * BF16_SUBLANES, :] = tile.astype(BF16)


def _qkv_proj(xb, w, b, g, n, d):
    _, dil = ATTN_GROUPS[g]
    stride = SUBLANES * dil
    gw = HEADS_PER_GROUP * HEAD_DIM
    sb = SUBLANES * stride
    tm = max(_pick(n, (1024, 512, 256)), 2 * sb)
    per = tm // stride
    new = jnp.arange(sb, dtype=jnp.int32)
    old = (new % SUBLANES) * stride + new // SUBLANES
    perm = (old[:, None] == jnp.arange(sb, dtype=jnp.int32)[None, :]).astype(BF16)
    return pl.pallas_call(
        functools.partial(_qkv_kernel, stride=stride),
        out_shape=jax.ShapeDtypeStruct((stride, n // stride, 3 * gw), BF16),
        grid=(n // tm, 3),
        in_specs=[
            pl.BlockSpec((tm, d), lambda i, j: (i, 0)),
            pl.BlockSpec((d, gw), lambda i, j: (0, j)),
            pl.BlockSpec((1, gw), lambda i, j: (0, j)),
            pl.BlockSpec((sb, sb), lambda i, j: (0, 0), pipeline_mode=pl.Buffered(1)),
        ],
        out_specs=pl.BlockSpec((stride, per, gw), lambda i, j: (0, i, j)),
        compiler_params=_params(("arbitrary", "arbitrary")),
        name=f"qkv_proj_g{g}",
    )(xb, w, b, perm)


def _lru_kernel(u_ref, g_ref, cw_ref, cb_ref, wa_ref, ba_ref, wx_ref, bx_ref, lam_ref, y_ref,
                carry_ref, h_ref, a_ref, b_ref, *, tt, nblk, bw):
    ti = pl.program_id(1)
    rows = tt * SUBLANES
    halo = SUBLANES * (CONV_WIDTH - 1)

    @pl.when(ti == 0)
    def _():
        carry_ref[...] = jnp.zeros_like(carry_ref)
        h_ref[...] = jnp.zeros_like(h_ref)

    u = u_ref[...].astype(F32)
    ext = jnp.concatenate([carry_ref[...], u], axis=0)
    uc = cb_ref[...] + cw_ref[0:1, :] * ext[0:rows, :]
    for j in range(1, CONV_WIDTH):
        uc = uc + cw_ref[j:j + 1, :] * ext[SUBLANES * j:SUBLANES * j + rows, :]
    carry_ref[...] = u[rows - halo:, :]

    ucb = uc.astype(BF16)
    r_parts, i_parts = [], []
    for kb in range(nblk):
        blk = ucb[:, kb * bw:(kb + 1) * bw]
        r_parts.append(jnp.dot(blk, wa_ref[kb], preferred_element_type=F32))
        i_parts.append(jnp.dot(blk, wx_ref[kb], preferred_element_type=F32))
    r = _sigmoid(jnp.concatenate(r_parts, axis=1) + ba_ref[...])
    ig = _sigmoid(jnp.concatenate(i_parts, axis=1) + bx_ref[...])

    nl = -lam_ref[...]
    softplus = jnp.maximum(nl, 0.0) + jnp.log(1.0 + jnp.exp(-jnp.abs(nl)))
    a = jnp.exp2(r * ((-LRU_C * LOG2E) * softplus))
    gap = 1.0 - a * a
    mult = jnp.where(gap > 0.0, gap * lax.rsqrt(gap), 0.0)
    gated = uc * ig
    a_ref[...] = a
    b_ref[...] = gated * mult

    @pl.when(ti == 0)
    def _():
        b_ref[0:SUBLANES, :] = gated[0:SUBLANES, :]

    def step(t, h):
        r0 = pl.multiple_of(t * SUBLANES, SUBLANES)
        h = a_ref[pl.ds(r0, SUBLANES), :] * h + b_ref[pl.ds(r0, SUBLANES), :]
        b_ref[pl.ds(r0, SUBLANES), :] = h
        return h

    h_ref[...] = lax.fori_loop(0, tt, step, h_ref[...], unroll=8)

    g = g_ref[...].astype(F32)
    gelu = 0.5 * g * (1.0 + jnp.tanh(0.7978845608028654 * (g + 0.044715 * (g * g * g))))
    y_ref[...] = (b_ref[...] * gelu).astype(y_ref.dtype)


def _lru(z, conv_w, conv_b, wa, ba, wx, bx, lam, n, c, col_u, col_g):
    nblk_total, bw, _ = wa.shape
    ct = _pick(c, (512, 256, 128))
    tt = _pick(n // SUBLANES, (128, 64, 32, 16, 8))
    rows = tt * SUBLANES
    nblk = ct // bw
    vec = lambda: pl.BlockSpec((1, ct), lambda ci, ti: (0, ci))
    return pl.pallas_call(
        functools.partial(_lru_kernel, tt=tt, nblk=nblk, bw=bw),
        out_shape=jax.ShapeDtypeStruct((n, c), BF16),
        grid=(c // ct, n // rows),
        in_specs=[
            pl.BlockSpec((rows, ct), lambda ci, ti: (ti, col_u // ct + ci)),
            pl.BlockSpec((rows, ct), lambda ci, ti: (ti, col_g // ct + ci)),
            pl.BlockSpec((CONV_WIDTH, ct), lambda ci, ti: (0, ci)),
            vec(),
            pl.BlockSpec((nblk, bw, bw), lambda ci, ti: (ci, 0, 0)),
            vec(),
            pl.BlockSpec((nblk, bw, bw), lambda ci, ti: (ci, 0, 0)),
            vec(),
            vec(),
        ],
        out_specs=pl.BlockSpec((rows, ct), lambda ci, ti: (ti, ci)),
        scratch_shapes=[
            pltpu.VMEM((SUBLANES * (CONV_WIDTH - 1), ct), F32),
            pltpu.VMEM((SUBLANES, ct), F32),
            pltpu.VMEM((rows, ct), F32),
            pltpu.VMEM((rows, ct), F32),
        ],
        compiler_params=_params(("arbitrary", "arbitrary")),
        name="rg_lru",
    )(z, z, conv_w, conv_b, wa, ba, wx, bx, lam)


def _attn_kernel(q_ref, k_ref, v_ref, o_ref, st_ref, *, band, dil, slopes, qblocks):
    qi = lax.broadcasted_iota(jnp.int32, (band, 2 * band), 0)
    kj = lax.broadcasted_iota(jnp.int32, (band, 2 * band), 1)
    rel = qi - kj
    scale = HEAD_DIM ** -0.5
    nt = (((1,), (1,)), ((), ()))
    lane = lax.broadcasted_iota(jnp.int32, (band, LANES), 1)

    for qq in range(qblocks):
        n = pl.program_id(1) * qblocks + qq
        win0 = pl.multiple_of(jnp.maximum(n - 1, 0) * band, band)
        own = jnp.where(n > 0, band, 0)
        rows = slice(qq * band, (qq + 1) * band)
        q = q_ref[rows, :]
        kw = k_ref[pl.ds(win0, 2 * band), :]
        vw = v_ref[pl.ds(win0, 2 * band), :]
        dist_i = rel + own
        valid = (dist_i >= 0) & (dist_i <= band)
        dist = dist_i.astype(F32)

        outs = []
        stat = jnp.zeros((band, LANES), F32)
        for h in range(HEADS_PER_GROUP):
            sl = slice(h * HEAD_DIM, (h + 1) * HEAD_DIM)
            bias = slopes[h] * dil
            s = lax.dot_general(q[:, sl], kw[:, sl], nt, preferred_element_type=F32)
            s = jnp.where(valid, s * scale - bias * dist, -jnp.inf)
            m = jnp.max(s, axis=1, keepdims=True)
            p = jnp.exp(s - m)
            l = jnp.sum(p, axis=1, keepdims=True)
            acc = jnp.dot(p.astype(BF16), vw[:, sl], preferred_element_type=F32)
            outs.append(acc / l)
            stat = jnp.where(lane == h, m + jnp.log(l), stat)
        o_ref[rows, :] = jnp.concatenate(outs, axis=1).astype(o_ref.dtype)
        st_ref[rows, :] = stat


def _attn_group(qkv, g, n):
    window, dil = ATTN_GROUPS[g]
    band = window // dil
    stride = SUBLANES * dil
    lp = n // stride
    gw = HEADS_PER_GROUP * HEAD_DIM
    n_heads = HEADS_PER_GROUP * len(ATTN_GROUPS)
    slopes = tuple(2.0 ** (-8.0 * (g * HEADS_PER_GROUP + h + 1) / n_heads) for h in range(HEADS_PER_GROUP))
    qblocks = _pick(lp // band, (4, 2, 1))
    tq = qblocks * band
    o, st = pl.pallas_call(
        functools.partial(_attn_kernel, band=band, dil=float(dil), slopes=slopes, qblocks=qblocks),
        out_shape=(jax.ShapeDtypeStruct((lp, stride * gw), BF16),
                   jax.ShapeDtypeStruct((lp, stride * LANES), F32)),
        grid=(stride, lp // tq),
        in_specs=[
            pl.BlockSpec((None, tq, gw), lambda c, i: (c, i, 0)),
            pl.BlockSpec((None, lp, gw), lambda c, i: (c, 0, 1)),
            pl.BlockSpec((None, lp, gw), lambda c, i: (c, 0, 2)),
        ],
        out_specs=(pl.BlockSpec((tq, gw), lambda c, i: (i, c)),
                   pl.BlockSpec((tq, LANES), lambda c, i: (i, c))),
        compiler_params=_params(("arbitrary", "arbitrary")),
        name=f"dilated_attn_g{g}",
    )(qkv, qkv, qkv)
    return o.reshape(n, gw), st.reshape(n, LANES)


def _merge_kernel(x_ref, yr_ref, gr_ref, ga_ref, o0_ref, o1_ref, o2_ref, s0_ref, s1_ref, s2_ref,
                  wr_ref, wa_ref, wo_ref, g1_ref, b1_ref, x1_ref, x1p_ref, slab_ref, *, tt, d):
    t1 = jnp.dot(yr_ref[...], wr_ref[...], preferred_element_type=F32)
    merged = _sigmoid(gr_ref[...].astype(F32)) * t1

    stats = [s0_ref[...], s1_ref[...], s2_ref[...]]
    outs = [o0_ref, o1_ref, o2_ref]
    heads = []
    for h in range(HEADS_PER_GROUP):
        lse = [s[:, h:h + 1] for s in stats]
        mx = jnp.maximum(jnp.maximum(lse[0], lse[1]), lse[2])
        e = [jnp.exp(v - mx) for v in lse]
        tot = e[0] + e[1] + e[2]
        acc = None
        for gi in range(len(ATTN_GROUPS)):
            term = (e[gi] / tot) * outs[gi][:, h * HEAD_DIM:(h + 1) * HEAD_DIM].astype(F32)
            acc = term if acc is None else acc + term
        heads.append(acc)
    oa = jnp.concatenate(heads, axis=1).astype(BF16)
    t2 = jnp.dot(oa, wa_ref[...], preferred_element_type=F32)
    merged = merged + _sigmoid(ga_ref[...].astype(F32)) * t2
    mix = jnp.dot(merged.astype(BF16), wo_ref[...], preferred_element_type=F32)

    v = DN_ALPHA * _time_major_rows(x_ref, slab_ref, tt, d) + mix
    mu = jnp.mean(v, axis=-1, keepdims=True)
    cen = v - mu
    var = jnp.mean(cen * cen, axis=-1, keepdims=True)
    x1 = cen * lax.rsqrt(var + LN_EPS) * g1_ref[...] + b1_ref[...]
    x1_ref[...] = x1
    _store_packed(x1p_ref, 0, x1)


def _merge(x, yr, z, os_, sts, wr, wa, wo, g1, b1, n, d, col_gr, col_ga):
    tm = _pick(n, (256, 128, 64))
    tt = tm // SUBLANES
    gw = HEADS_PER_GROUP * HEAD_DIM
    rpt = _rows_per_token(d)
    const = lambda shape: pl.BlockSpec(shape, lambda i: (0, 0), pipeline_mode=pl.Buffered(1))
    row = lambda w: pl.BlockSpec((tm, w), lambda i: (i, 0))
    return pl.pallas_call(
        functools.partial(_merge_kernel, tt=tt, d=d),
        out_shape=(jax.ShapeDtypeStruct((n, d), F32), jax.ShapeDtypeStruct((n * rpt, LANES), U32)),
        grid=(n // tm,),
        in_specs=[
            pl.BlockSpec((SUBLANES, tt, d), lambda i: (0, i, 0)),
            row(yr.shape[1]),
            pl.BlockSpec((tm, d), lambda i: (i, col_gr // d)),
            pl.BlockSpec((tm, d), lambda i: (i, col_ga // d)),
            row(gw), row(gw), row(gw),
            row(LANES), row(LANES), row(LANES),
            const(wr.shape), const(wa.shape), const(wo.shape),
            const((1, d)), const((1, d)),
        ],
        out_specs=(row(d), pl.BlockSpec((tm * rpt, LANES), lambda i: (i, 0))),
        scratch_shapes=[pltpu.VMEM((d // LANES, tm, LANES), F32)],
        compiler_params=_params(("arbitrary",)),
        name="merge_ln1",
    )(x, yr, z, z, *os_, *sts, wr, wa, wo, g1, b1)


def _router_kernel(x_ref, wr_ref, bias_ref, idx_ref, w_ref, rank_ref, cnt_ref, base_ref, tri_ref, *, t, e):
    i = pl.program_id(0)

    @pl.when(i == 0)
    def _():
        base_ref[...] = jnp.zeros_like(base_ref)
        rr = lax.broadcasted_iota(jnp.int32, (t, t), 0)
        cc = lax.broadcasted_iota(jnp.int32, (t, t), 1)
        tri_ref[...] = jnp.where(rr < cc, 1.0, 0.0).astype(BF16)

    nt = (((1,), (1,)), ((), ()))
    logits = lax.dot_general(wr_ref[...], x_ref[...].astype(BF16), nt, preferred_element_type=F32)
    scores = _sigmoid(logits)
    biased = scores + bias_ref[...]

    per = e // N_GROUP
    sub = lax.broadcasted_iota(jnp.int32, (per, t), 0).astype(F32)
    blocks, gscore = [], []
    for g in range(N_GROUP):
        blk = biased[g * per:(g + 1) * per, :]
        m1 = jnp.max(blk, axis=0, keepdims=True)
        first = jnp.min(jnp.where(blk == m1, sub, float(per)), axis=0, keepdims=True)
        m2 = jnp.max(jnp.where(sub == first, -jnp.inf, blk), axis=0, keepdims=True)
        blocks.append(blk)
        gscore.append(m1 + m2)
    masked_blocks = []
    for g in range(N_GROUP):
        beaten = jnp.zeros((1, t), F32)
        for g2 in range(N_GROUP):
            if g2 == g:
                continue
            wins = (gscore[g2] > gscore[g]) | ((gscore[g2] == gscore[g]) & (g2 < g))
            beaten = beaten + jnp.where(wins, 1.0, 0.0)
        keep = beaten < float(TOPK_GROUP)
        masked_blocks.append(jnp.where(keep, blocks[g], -jnp.inf))
    masked = jnp.concatenate(masked_blocks, axis=0)

    eidx = lax.broadcasted_iota(jnp.int32, (e, t), 0).astype(F32)
    member = jnp.zeros((e, t), F32)
    sels, ws = [], []
    for _ in range(TOP_K):
        m = jnp.max(masked, axis=0, keepdims=True)
        sel = jnp.min(jnp.where(masked == m, eidx, float(e)), axis=0, keepdims=True)
        hit = eidx == sel
        ws.append(jnp.sum(jnp.where(hit, scores, 0.0), axis=0, keepdims=True))
        masked = jnp.where(hit, -jnp.inf, masked)
        member = jnp.where(hit, 1.0, member)
        sels.append(sel)
    wsum = ws[0]
    for k in range(1, TOP_K):
        wsum = wsum + ws[k]

    before = jnp.dot(member.astype(BF16), tri_ref[...], preferred_element_type=F32) + base_ref[...]
    for k in range(TOP_K):
        idx_ref[k:k + 1, :] = sels[k].astype(jnp.int32)
        w_ref[k:k + 1, :] = ws[k] / wsum * ROUTED_SCALE
        rk = jnp.sum(jnp.where(eidx == sels[k], before, 0.0), axis=0, keepdims=True)
        rank_ref[k:k + 1, :] = rk.astype(jnp.int32)
    base_ref[...] = base_ref[...] + jnp.sum(member, axis=1, keepdims=True)
    cnt_ref[...] = jnp.broadcast_to(base_ref[...], cnt_ref.shape)


def _router(x1, wrt, bias, n, d):
    e = wrt.shape[0]
    t = _pick(n, (512, 256, 128))
    tok = lambda dt: jax.ShapeDtypeStruct((TOP_K, n), dt)
    return pl.pallas_call(
        functools.partial(_router_kernel, t=t, e=e),
        out_shape=(tok(jnp.int32), tok(F32), tok(jnp.int32), jax.ShapeDtypeStruct((e, LANES), F32)),
        grid=(n // t,),
        in_specs=[
            pl.BlockSpec((t, d), lambda i: (i, 0)),
            pl.BlockSpec((e, d), lambda i: (0, 0)),
            pl.BlockSpec((e, 1), lambda i: (0, 0)),
        ],
        out_specs=(pl.BlockSpec((TOP_K, t), lambda i: (0, i)),
                   pl.BlockSpec((TOP_K, t), lambda i: (0, i)),
                   pl.BlockSpec((TOP_K, t), lambda i: (0, i)),
                   pl.BlockSpec((e, LANES), lambda i: (0, 0))),
        scratch_shapes=[pltpu.VMEM((e, 1), F32), pltpu.VMEM((t, t), BF16)],
        compiler_params=_params(("arbitrary",)),
        name="router_topk",
    )(x1, wrt, bias)


def _dest_kernel(idx_ref, rank_ref, start_ref, dest_ref, *, t, e):
    eidx = lax.broadcasted_iota(jnp.int32, (e, t), 0)
    for k in range(TOP_K):
        hit = eidx == idx_ref[k:k + 1, :]
        st = jnp.sum(jnp.where(hit, start_ref[...], 0.0), axis=0, keepdims=True)
        dest_ref[0, k:k + 1, :] = st.astype(jnp.int32) + rank_ref[k:k + 1, :]


def _dest(idx, rank, start_col, n, t):
    e = start_col.shape[0]
    return pl.pallas_call(
        functools.partial(_dest_kernel, t=t, e=e),
        out_shape=jax.ShapeDtypeStruct((n // t, TOP_K, t), jnp.int32),
        grid=(n // t,),
        in_specs=[
            pl.BlockSpec((TOP_K, t), lambda i: (0, i)),
            pl.BlockSpec((TOP_K, t), lambda i: (0, i)),
            pl.BlockSpec((e, 1), lambda i: (0, 0)),
        ],
        out_specs=pl.BlockSpec((1, TOP_K, t), lambda i: (i, 0, 0)),
        compiler_params=_params(("arbitrary",)),
        name="moe_dest",
    )(idx, rank, start_col)


def _token_copy(src_ref, src_tok, dst_ref, dst_tok, sem, rpt):
    s0 = pl.multiple_of(src_tok * rpt, rpt)
    d0 = pl.multiple_of(dst_tok * rpt, rpt)
    return pltpu.make_async_copy(src_ref.at[pl.ds(s0, rpt), :], dst_ref.at[pl.ds(d0, rpt), :], sem)


TOKENS_PER_SLOT_ROW = LANES // TOP_K


def _for_each_token_slots(table, t, fn):
    def body(r, c):
        for pair in range(0, TOKENS_PER_SLOT_ROW, 2):
            slots = [[table[r, (pair + p) * TOP_K + k] for k in range(TOP_K)] for p in range(2)]
            for p in range(2):
                fn(r * TOKENS_PER_SLOT_ROW + pair + p, slots[p])
        return c

    lax.fori_loop(0, t // TOKENS_PER_SLOT_ROW, body, 0)


def _slot_table_copy(dest_hbm, step, dsm, slot, sem_idx):
    return pltpu.make_async_copy(dest_hbm.at[step], dsm.at[slot], sem_idx.at[slot])


def _dispatch_kernel(dest_hbm, x_hbm, wsg_ref, wsu_ref, wsd_ref, xs_hbm, sh_ref, dsm, xring, zero_ref, sem_idx,
                     sem_in, sem, sem_pad, *, t, rpt, n_steps, pad_rows):
    i = pl.program_id(0)
    slot = i % 2
    rows = t * rpt

    def block_in(step, buf):
        r0 = pl.multiple_of(step * rows, rows)
        return pltpu.make_async_copy(x_hbm.at[pl.ds(r0, rows), :], xring.at[buf], sem_in.at[buf])

    @pl.when(i == 0)
    def _():
        _slot_table_copy(dest_hbm, 0, dsm, 0, sem_idx).start()
        block_in(0, 0).start()
        zero_ref[...] = jnp.zeros_like(zero_ref)
        pad = pltpu.make_async_copy(zero_ref, xs_hbm.at[pl.ds(n_steps * t * TOP_K * rpt, pad_rows), :], sem_pad)
        pad.start()
        pad.wait()

    _slot_table_copy(dest_hbm, i, dsm, slot, sem_idx).wait()
    block_in(i, i % 3).wait()

    @pl.when(i + 1 < n_steps)
    def _():
        _slot_table_copy(dest_hbm, i + 1, dsm, 1 - slot, sem_idx).start()
        block_in(i + 1, (i + 1) % 3).start()

    x_ref = xring.at[i % 3]

    def issue(tl, slots):
        for k in range(TOP_K):
            _token_copy(x_ref, tl, xs_hbm, slots[k], sem.at[slot], rpt).start(priority=k % 2)

    _for_each_token_slots(dsm.at[slot], t, issue)

    x = _unpack_words([x_ref[_token_rows(0, t, rpt, j), :] for j in range(rpt)]).astype(BF16)
    g = jnp.dot(x, wsg_ref[...], preferred_element_type=F32)
    u = jnp.dot(x, wsu_ref[...], preferred_element_type=F32)
    h = (g * _sigmoid(g)) * u
    sh_ref[...] = jnp.dot(h.astype(BF16), wsd_ref[...], preferred_element_type=F32).astype(sh_ref.dtype)

    def drain(parity):
        for _ in range(TOP_K):
            pltpu.make_async_copy(xring.at[0], xs_hbm.at[pl.ds(0, rows), :], sem.at[parity]).wait()

    @pl.when(i > 0)
    def _():
        drain(1 - slot)

    @pl.when(i == n_steps - 1)
    def _():
        drain(slot)


def _dispatch(dest3, x1p, wsg, wsu, wsd, n, d, t, pad_tokens):
    rpt = _rows_per_token(d)
    const = lambda shape: pl.BlockSpec(shape, lambda i: (0, 0), pipeline_mode=pl.Buffered(1))
    return pl.pallas_call(
        functools.partial(_dispatch_kernel, t=t, rpt=rpt, n_steps=n // t, pad_rows=pad_tokens * rpt),
        out_shape=(jax.ShapeDtypeStruct(((n * TOP_K + pad_tokens) * rpt, LANES), U32),
                   jax.ShapeDtypeStruct((n, d), BF16)),
        grid=(n // t,),
        in_specs=[pl.BlockSpec(memory_space=pl.ANY), pl.BlockSpec(memory_space=pl.ANY),
                  const(wsg.shape), const(wsu.shape), const(wsd.shape)],
        out_specs=(pl.BlockSpec(memory_space=pl.ANY), pl.BlockSpec((t, d), lambda i: (i, 0))),
        scratch_shapes=[pltpu.SMEM((2, TOP_K * t // LANES, LANES), jnp.int32),
                        pltpu.VMEM((3, t * rpt, LANES), U32),
                        pltpu.VMEM((pad_tokens * rpt, LANES), U32),
                        pltpu.SemaphoreType.DMA((2,)), pltpu.SemaphoreType.DMA((3,)),
                        pltpu.SemaphoreType.DMA((2,)), pltpu.SemaphoreType.DMA],
        compiler_params=_params(("arbitrary",)),
        name="moe_dispatch",
    )(dest3, x1p, wsg, wsu, wsd)


CHUNK_PARTS = 4


def _expert_kernel(r0_ref, we_ref, nw_ref, nx_ref, par_ref, parts_ref, xs_hbm, wg_hbm, wu_hbm, wd_hbm, o_hbm,
                   xin_ref, out_ref, wgf_ref, wuf_ref, wdf_ref, wgb_ref, wub_ref, wdb_ref,
                   wsem, isem, osem, *, tm, rpt):
    w = pl.program_id(0)
    nw = nw_ref[0]
    rows = tm * rpt

    def weight_copies(expert, buf):
        return [pltpu.make_async_copy(src.at[expert], dst.at[buf], wsem.at[buf])
                for src, dst in ((wg_hbm, wgf_ref), (wu_hbm, wuf_ref), (wd_hbm, wdf_ref))]

    def rows_in(item, buf):
        r0 = pl.multiple_of(r0_ref[item] * rpt, rpt)
        return pltpu.make_async_copy(xs_hbm.at[pl.ds(r0, rows), :], xin_ref.at[buf], isem.at[buf])

    def rows_out(item, buf):
        r0 = pl.multiple_of(r0_ref[item] * rpt, rpt)
        return pltpu.make_async_copy(out_ref.at[buf], o_hbm.at[pl.ds(r0, rows), :], osem.at[buf])

    @pl.when(w < nw)
    def _():
        e = we_ref[w]
        pw = jnp.maximum(w - 1, 0)
        buf_io = w % 2

        @pl.when(w == 0)
        def _():
            rows_in(0, 0).start()

        @pl.when(w + 1 < nw)
        def _():
            rows_in(w + 1, 1 - buf_io).start()

        @pl.when((w == 0) | (e != we_ref[pw]))
        def _():
            buf = par_ref[w]
            nxt = nx_ref[w]

            @pl.when(w == 0)
            def _():
                for c in weight_copies(e, buf):
                    c.start(priority=1)

            @pl.when(nxt >= 0)
            def _():
                for c in weight_copies(nxt, 1 - buf):
                    c.start(priority=1)

            for c in weight_copies(e, buf):
                c.wait()
            wgb_ref[...] = wgf_ref[buf].astype(BF16)
            wub_ref[...] = wuf_ref[buf].astype(BF16)
            wdb_ref[...] = wdf_ref[buf].astype(BF16)

        rows_in(w, buf_io).wait()
        xin = xin_ref.at[buf_io]
        out = out_ref.at[buf_io]

        def swiglu_rows(m):
            x = _unpack_words([xin[_token_rows(0, m, rpt, j), :] for j in range(rpt)]).astype(BF16)
            g = jnp.dot(x, wgb_ref[...], preferred_element_type=F32)
            u = jnp.dot(x, wub_ref[...], preferred_element_type=F32)
            h = (g * _sigmoid(g)) * u
            _store_packed(out, 0, jnp.dot(h.astype(BF16), wdb_ref[...], preferred_element_type=F32))

        quarter = tm // CHUNK_PARTS
        for parts in range(1, CHUNK_PARTS + 1):
            @pl.when(parts_ref[w] == parts)
            def _(parts=parts):
                swiglu_rows(parts * quarter)
                if parts < CHUNK_PARTS:
                    rest = (CHUNK_PARTS - parts) * quarter * rpt
                    out[pl.ds(parts * quarter * rpt, rest), :] = jnp.zeros((rest, LANES), U32)

        @pl.when(w > 0)
        def _():
            rows_out(pw, 1 - buf_io).wait()

        rows_out(w, buf_io).start()

        @pl.when(w == nw - 1)
        def _():
            rows_out(w, buf_io).wait()


def _experts(work, xs, w_gate, w_up, w_down, tm):
    _, d, de = w_gate.shape
    rpt = _rows_per_token(d)
    n_work = work[0].shape[0]
    hbm = pl.BlockSpec(memory_space=pl.ANY)
    return pl.pallas_call(
        functools.partial(_expert_kernel, tm=tm, rpt=rpt),
        out_shape=jax.ShapeDtypeStruct(xs.shape, U32),
        grid_spec=pltpu.PrefetchScalarGridSpec(
            num_scalar_prefetch=len(work),
            grid=(n_work,),
            in_specs=[hbm, hbm, hbm, hbm],
            out_specs=hbm,
            scratch_shapes=[pltpu.VMEM((2, tm * rpt, LANES), U32), pltpu.VMEM((2, tm * rpt, LANES), U32),
                            pltpu.VMEM((2, d, de), F32), pltpu.VMEM((2, d, de), F32), pltpu.VMEM((2, de, d), F32),
                            pltpu.VMEM((d, de), BF16), pltpu.VMEM((d, de), BF16), pltpu.VMEM((de, d), BF16),
                            pltpu.SemaphoreType.DMA((2,)), pltpu.SemaphoreType.DMA((2,)),
                            pltpu.SemaphoreType.DMA((2,))],
        ),
        compiler_params=_params(("arbitrary",)),
        name="moe_experts",
    )(*work, xs, w_gate, w_up, w_down)


def _combine_kernel(dest_hbm, x1_ref, wt_ref, sh_ref, o_hbm, g2_ref, b2_ref, y_ref,
                    dsm, gbuf, slab_ref, sem_idx, sem, *, t, tt, d, rpt, n_steps):
    i = pl.program_id(0)
    slot = i % 2

    def issue(step_slot):
        def gather(tl, slots):
            for k in range(TOP_K):
                _token_copy(o_hbm, slots[k], gbuf.at[step_slot], k * t + tl, sem.at[step_slot],
                            rpt).start(priority=k % 2)

        _for_each_token_slots(dsm.at[step_slot], t, gather)

    @pl.when(i == 0)
    def _():
        first = _slot_table_copy(dest_hbm, 0, dsm, 0, sem_idx)
        first.start()
        first.wait()
        issue(0)
        if n_steps > 1:
            _slot_table_copy(dest_hbm, 1, dsm, 1, sem_idx).start()

    @pl.when(i + 1 < n_steps)
    def _():
        _slot_table_copy(dest_hbm, i + 1, dsm, 1 - slot, sem_idx).wait()
        issue(1 - slot)

        @pl.when(i + 2 < n_steps)
        def _():
            _slot_table_copy(dest_hbm, i + 2, dsm, slot, sem_idx).start()

    gcur = gbuf.at[slot]
    pltpu.make_async_copy(o_hbm.at[pl.ds(0, TOP_K * t * rpt), :], gcur, sem.at[slot]).wait()

    acc_lo = [None] * rpt
    acc_hi = [None] * rpt
    for k in range(TOP_K):
        wk = wt_ref[:, k:k + 1]
        for j in range(rpt):
            words = gcur[_token_rows(k * t * rpt, t, rpt, j), :]
            lo = wk * pltpu.bitcast(words << 16, F32)
            hi = wk * pltpu.bitcast(words & U32(HI_MASK), F32)
            acc_lo[j] = lo if k == 0 else acc_lo[j] + lo
            acc_hi[j] = hi if k == 0 else acc_hi[j] + hi
    routed = jnp.concatenate(acc_lo + acc_hi, axis=1)

    v = DN_ALPHA * x1_ref[...] + (routed + sh_ref[...].astype(F32))
    mu = jnp.mean(v, axis=-1, keepdims=True)
    cen = v - mu
    var = jnp.mean(cen * cen, axis=-1, keepdims=True)
    y = cen * lax.rsqrt(var + LN_EPS) * g2_ref[...] + b2_ref[...]
    _store_batch_major(y_ref, slab_ref, y, tt, d)


def _combine(dest3, x1, wt, sh, o, g2, b2, n, d, t):
    tt = t // SUBLANES
    rpt = _rows_per_token(d)
    const = lambda shape: pl.BlockSpec(shape, lambda i: (0, 0), pipeline_mode=pl.Buffered(1))
    return pl.pallas_call(
        functools.partial(_combine_kernel, t=t, tt=tt, d=d, rpt=rpt, n_steps=n // t),
        out_shape=jax.ShapeDtypeStruct((SUBLANES, n // SUBLANES, d), F32),
        grid=(n // t,),
        in_specs=[
            pl.BlockSpec(memory_space=pl.ANY),
            pl.BlockSpec((t, d), lambda i: (i, 0)),
            pl.BlockSpec((t, TOP_K), lambda i: (i, 0)),
            pl.BlockSpec((t, d), lambda i: (i, 0)),
            pl.BlockSpec(memory_space=pl.ANY),
            const((1, d)), const((1, d)),
        ],
        out_specs=pl.BlockSpec((SUBLANES, tt, d), lambda i: (0, i, 0)),
        scratch_shapes=[pltpu.SMEM((2, TOP_K * t // LANES, LANES), jnp.int32), pltpu.VMEM((2, TOP_K * t * rpt, LANES), U32),
                        pltpu.VMEM((d // LANES, t, LANES), F32),
                        pltpu.SemaphoreType.DMA((2,)), pltpu.SemaphoreType.DMA((2,))],
        compiler_params=_params(("arbitrary",)),
        name="moe_combine_ln2",
    )(dest3, x1, wt, sh, o, g2, b2)


def _work_items(counts, m, tm):
    e = counts.shape[0]
    end = jnp.cumsum(counts)
    start = end - counts
    nchunk = (counts + tm - 1) // tm
    cum = jnp.cumsum(nchunk)
    off = cum - nchunk
    n_work = m // tm + e
    w = jnp.arange(n_work, dtype=jnp.int32)
    we = jnp.minimum(jnp.sum(cum[None, :] <= w[:, None], axis=1), e - 1).astype(jnp.int32)
    chunk = w - off[we]
    row0 = (start[we] + chunk * tm).astype(jnp.int32)
    nw = cum[-1].astype(jnp.int32)
    row0 = jnp.where(w < nw, row0, 0)
    rem = counts - (nchunk - 1) * tm
    quarter = tm // CHUNK_PARTS
    parts = jnp.where(chunk == nchunk[we] - 1, (rem[we] + quarter - 1) // quarter, CHUNK_PARTS).astype(jnp.int32)
    parts = jnp.clip(parts, 1, CHUNK_PARTS)
    ids = jnp.arange(e, dtype=jnp.int32)
    later = lax.cummin(jnp.where(counts > 0, ids, e)[::-1])[::-1]
    nxt_of = jnp.concatenate([later[1:], jnp.full((1,), e, jnp.int32)])
    nxt_of = jnp.where(nxt_of >= e, -1, nxt_of).astype(jnp.int32)
    ordinal = jnp.cumsum((counts > 0).astype(jnp.int32)) - 1
    work = (row0, we, nw.reshape(1), nxt_of[we], (ordinal[we] % 2).astype(jnp.int32), parts)
    return work, start.astype(jnp.int32)


def kernel(x, w_in, b_in, conv_w, conv_b, lru_wa, lru_ba, lru_wx, lru_bx, lru_lambda, w_rnn_br, w_attn_br, w_out, ln1_g, ln1_b, w_router, router_bias, w_gate, w_up, w_down, ws_gate, ws_up, ws_down, ln2_g, ln2_b):
    bsz, s, d = x.shape
    assert bsz == SUBLANES, "time-major rows need batch == 8"
    assert DEPTH == 1 and w_in.shape[0] == 1
    n = bsz * s
    c = lru_lambda.shape[1]
    a = HEADS_PER_GROUP * len(ATTN_GROUPS) * HEAD_DIM
    assert c == d and w_in.shape[2] == 2 * c + 3 * a + 2 * d
    layer = 0

    gw = HEADS_PER_GROUP * HEAD_DIM
    src = (0, c, 2 * c, 2 * c + a, 2 * c + 2 * a, 2 * c + 3 * a, 2 * c + 3 * a + d, 2 * c + 3 * a + 2 * d)
    part = lambda v, p, lo=0, hi=None: v[..., src[p]:src[p + 1]][..., lo:hi]
    cat = lambda v, pieces: jnp.concatenate([part(v, *p) for p in pieces], axis=-1)
    main = ((0,), (1,), (5,), (6,))
    w_main = cat(w_in[layer], main).astype(BF16)
    b_main = cat(b_in[layer], main).reshape(1, -1)
    col_u, col_g, col_gr, col_ga = 0, c, 2 * c, 2 * c + d
    row = lambda v: v.reshape(1, -1)

    z, xb = _in_proj(x, w_main, b_main, n, d)

    yr = _lru(z, conv_w[layer], row(conv_b[layer]), lru_wa[layer].astype(BF16), row(lru_ba[layer]),
              lru_wx[layer].astype(BF16), row(lru_bx[layer]), row(lru_lambda[layer]), n, c, col_u, col_g)

    os_, sts = [], []
    for g in range(len(ATTN_GROUPS)):
        cols = tuple((p, g * gw, (g + 1) * gw) for p in (2, 3, 4))
        qkv = _qkv_proj(xb, cat(w_in[layer], cols).astype(BF16), cat(b_in[layer], cols).reshape(1, -1), g, n, d)
        o, st = _attn_group(qkv, g, n)
        os_.append(o)
        sts.append(st)

    x1, x1p = _merge(x, yr, z, os_, sts, w_rnn_br[layer].astype(BF16), w_attn_br[layer].astype(BF16),
                     w_out[layer].astype(BF16), row(ln1_g[layer]), row(ln1_b[layer]), n, d, col_gr, col_ga)

    n_exp = w_router.shape[2]
    idx, wts, rank, cnt = _router(x1, w_router[layer].T.astype(BF16), router_bias[layer].reshape(n_exp, 1), n, d)

    tm = 512
    t_tok = _pick(n, (256, 128))
    m = n * TOP_K
    counts = cnt[:, 0].astype(jnp.int32)
    work, start = _work_items(counts, m, tm)
    dest3 = _dest(idx, rank, start.astype(F32).reshape(n_exp, 1), n, t_tok)
    dest3 = dest3.transpose(0, 2, 1).reshape(n // t_tok, t_tok * TOP_K // LANES, LANES)

    xs, shared = _dispatch(dest3, x1p, ws_gate[layer].astype(BF16), ws_up[layer].astype(BF16),
                           ws_down[layer].astype(BF16), n, d, t_tok, pad_tokens=tm)
    o = _experts(work, xs, w_gate[layer], w_up[layer], w_down[layer], tm)
    y = _combine(dest3, x1, wts.T, shared, o, row(ln2_g[layer]), row(ln2_b[layer]), n, d, t_tok)
    return y
```

```python
import functools

import jax
import jax.numpy as jnp
from jax import lax
from jax.experimental import pallas as pl
from jax.experimental.pallas import tpu as pltpu

HEAD_DIM = 128
ATTN_GROUPS = ((128, 1), (512, 4), (2048, 16))
HEADS_PER_GROUP = 4
CONV_WIDTH = 4
LRU_C = 8.0
N_GROUP = 8
TOPK_GROUP = 4
TOP_K = 8
ROUTED_SCALE = 2.5
DEPTH = 1
DN_ALPHA = (2 * DEPTH) ** 0.25
LN_EPS = 1e-5

SUBLANES = 8
LANES = 128
V7X_VMEM_LIMIT_BYTES = 56 * 1024 * 1024

F32 = jnp.float32
BF16 = jnp.bfloat16


def _pick(n, cands):
    for c in cands:
        if n % c == 0:
            return c
    raise ValueError(f"no tile in {cands} divides {n}")


LOG2E = 1.4426950408889634


def _sigmoid(x):
    return 1.0 / (1.0 + jnp.exp2(x * (-LOG2E)))


def _params(sem, vmem=V7X_VMEM_LIMIT_BYTES):
    return pltpu.CompilerParams(dimension_semantics=sem, vmem_limit_bytes=vmem)


def _time_major_rows(x_ref, slab_ref, tt, d):
    for b in range(SUBLANES):
        for j in range(d // LANES):
            slab_ref[j, pl.ds(b, tt, stride=SUBLANES), :] = x_ref[b, :, j * LANES:(j + 1) * LANES]
    return jnp.concatenate([slab_ref[j] for j in range(d // LANES)], axis=1)


def _store_batch_major(y_ref, slab_ref, y, tt, d):
    for j in range(d // LANES):
        slab_ref[j] = y[:, j * LANES:(j + 1) * LANES]
    for b in range(SUBLANES):
        for j in range(d // LANES):
            y_ref[b, :, j * LANES:(j + 1) * LANES] = slab_ref[j, pl.ds(b, tt, stride=SUBLANES), :]


U32 = jnp.uint32
HI_MASK = 0xFFFF0000


def _rows_per_token(d):
    assert d % (2 * LANES) == 0
    return d // (2 * LANES)


def _token_rows(base, m, rpt, j):
    return pl.ds(base + j, m, stride=rpt)


def _pack_words(x):
    half = x.shape[1] // 2
    bits = pltpu.bitcast(x.astype(BF16).astype(F32), U32)
    words = (bits[:, :half] >> 16) | (bits[:, half:] & U32(HI_MASK))
    return [words[:, j * LANES:(j + 1) * LANES] for j in range(half // LANES)]


def _store_packed(ref, base, x):
    chunks = _pack_words(x)
    for j, chunk in enumerate(chunks):
        ref[_token_rows(base, x.shape[0], len(chunks), j), :] = chunk


def _unpack_words(chunks):
    lo = [pltpu.bitcast(c << 16, F32) for c in chunks]
    hi = [pltpu.bitcast(c & U32(HI_MASK), F32) for c in chunks]
    return jnp.concatenate(lo + hi, axis=1)


def _in_proj_kernel(x_ref, w_ref, b_ref, z_ref, xb_ref, slab_ref, *, tt, d):
    @pl.when(pl.program_id(1) == 0)
    def _():
        for b in range(SUBLANES):
            for j in range(d // LANES):
                slab_ref[j, pl.ds(b, tt, stride=SUBLANES), :] = x_ref[b, :, j * LANES:(j + 1) * LANES]
        for j in range(d // LANES):
            xb_ref[:, j * LANES:(j + 1) * LANES] = slab_ref[j].astype(BF16)

    z = jnp.dot(xb_ref[...], w_ref[...], preferred_element_type=F32) + b_ref[...]
    z_ref[...] = z.astype(z_ref.dtype)


def _in_proj(x, w, b, n, d):
    d_out = w.shape[1]
    tm = _pick(n, (1024, 512, 256, 128, 64, 32, 16))
    tn = _pick(d_out, (1024, 512, 256, 128))
    tt = tm // SUBLANES
    return pl.pallas_call(
        functools.partial(_in_proj_kernel, tt=tt, d=d),
        out_shape=(jax.ShapeDtypeStruct((n, d_out), BF16), jax.ShapeDtypeStruct((n, d), BF16)),
        grid=(n // tm, d_out // tn),
        in_specs=[
            pl.BlockSpec((SUBLANES, tt, d), lambda i, j: (0, i, 0)),
            pl.BlockSpec((d, tn), lambda i, j: (0, j)),
            pl.BlockSpec((1, tn), lambda i, j: (0, j)),
        ],
        out_specs=(pl.BlockSpec((tm, tn), lambda i, j: (i, j)),
                   pl.BlockSpec((tm, d), lambda i, j: (i, 0))),
        scratch_shapes=[pltpu.VMEM((d // LANES, tm, LANES), F32)],
        compiler_params=_params(("arbitrary", "arbitrary")),
        name="in_proj",
    )(x, w, b)


BF16_SUBLANES = 16


def _qkv_kernel(xb_ref, w_ref, b_ref, p_ref, o_ref, *, stride):
    r = (jnp.dot(xb_ref[...], w_ref[...], preferred_element_type=F32) + b_ref[...]).astype(BF16)
    sb = p_ref.shape[0]
    for s in range(r.shape[0] // (2 * sb)):
        halves = [jnp.dot(p_ref[...], r[(2 * s + i) * sb:(2 * s + i + 1) * sb, :], preferred_element_type=F32)
                  for i in range(2)]
        for c in range(stride):
            tile = jnp.concatenate([hv[c * SUBLANES:(c + 1) * SUBLANES, :] for hv in halves], axis=0)
            o_ref[c, s * BF16_SUBLANES:(s + 1) * BF16_SUBLANES, :] = tile.astype(BF16)


def _qkv_proj(xb, w, b, g, n, d):
    _, dil = ATTN_GROUPS[g]
    stride = SUBLANES * dil
    gw = HEADS_PER_GROUP * HEAD_DIM
    sb = SUBLANES * stride
    tm = max(_pick(n, (1024, 512, 256)), 2 * sb)
    per = tm // stride
    new = jnp.arange(sb, dtype=jnp.int32)
    old = (new % SUBLANES) * stride + new // SUBLANES
    perm = (old[:, None] == jnp.arange(sb, dtype=jnp.int32)[None, :]).astype(BF16)
    return pl.pallas_call(
        functools.partial(_qkv_kernel, stride=stride),
        out_shape=jax.ShapeDtypeStruct((stride, n // stride, 3 * gw), BF16),
        grid=(n // tm, 3),
        in_specs=[
            pl.BlockSpec((tm, d), lambda i, j: (i, 0)),
            pl.BlockSpec((d, gw), lambda i, j: (0, j)),
            pl.BlockSpec((1, gw), lambda i, j: (0, j)),
            pl.BlockSpec((sb, sb), lambda i, j: (0, 0), pipeline_mode=pl.Buffered(1)),
        ],
        out_specs=pl.BlockSpec((stride, per, gw), lambda i, j: (0, i, j)),
        compiler_params=_params(("arbitrary", "arbitrary")),
        name=f"qkv_proj_g{g}",
    )(xb, w, b, perm)


def _lru_kernel(u_ref, g_ref, cw_ref, cb_ref, wa_ref, ba_ref, wx_ref, bx_ref, lam_ref, y_ref,
                carry_ref, h_ref, a_ref, b_ref, *, tt, nblk, bw):
    ti = pl.program_id(1)
    rows = tt * SUBLANES
    halo = SUBLANES * (CONV_WIDTH - 1)

    @pl.when(ti == 0)
    def _():
        carry_ref[...] = jnp.zeros_like(carry_ref)
        h_ref[...] = jnp.zeros_like(h_ref)

    u = u_ref[...].astype(F32)
    ext = jnp.concatenate([carry_ref[...], u], axis=0)
    uc = cb_ref[...] + cw_ref[0:1, :] * ext[0:rows, :]
    for j in range(1, CONV_WIDTH):
        uc = uc + cw_ref[j:j + 1, :] * ext[SUBLANES * j:SUBLANES * j + rows, :]
    carry_ref[...] = u[rows - halo:, :]

    ucb = uc.astype(BF16)
    r_parts, i_parts = [], []
    for kb in range(nblk):
        blk = ucb[:, kb * bw:(kb + 1) * bw]
        r_parts.append(jnp.dot(blk, wa_ref[kb], preferred_element_type=F32))
        i_parts.append(jnp.dot(blk, wx_ref[kb], preferred_element_type=F32))
    r = _sigmoid(jnp.concatenate(r_parts, axis=1) + ba_ref[...])
    ig = _sigmoid(jnp.concatenate(i_parts, axis=1) + bx_ref[...])

    nl = -lam_ref[...]
    softplus = jnp.maximum(nl, 0.0) + jnp.log(1.0 + jnp.exp(-jnp.abs(nl)))
    a = jnp.exp2(r * ((-LRU_C * LOG2E) * softplus))
    gap = 1.0 - a * a
    mult = jnp.where(gap > 0.0, gap * lax.rsqrt(gap), 0.0)
    gated = uc * ig
    a_ref[...] = a
    b_ref[...] = gated * mult

    @pl.when(ti == 0)
    def _():
        b_ref[0:SUBLANES, :] = gated[0:SUBLANES, :]

    def step(t, h):
        r0 = pl.multiple_of(t * SUBLANES, SUBLANES)
        h = a_ref[pl.ds(r0, SUBLANES), :] * h + b_ref[pl.ds(r0, SUBLANES), :]
        b_ref[pl.ds(r0, SUBLANES), :] = h
        return h

    h_ref[...] = lax.fori_loop(0, tt, step, h_ref[...], unroll=8)

    g = g_ref[...].astype(F32)
    gelu = 0.5 * g * (1.0 + jnp.tanh(0.7978845608028654 * (g + 0.044715 * (g * g * g))))
    y_ref[...] = (b_ref[...] * gelu).astype(y_ref.dtype)


def _lru(z, conv_w, conv_b, wa, ba, wx, bx, lam, n, c, col_u, col_g):
    nblk_total, bw, _ = wa.shape
    ct = _pick(c, (512, 256, 128))
    tt = _pick(n // SUBLANES, (128, 64, 32, 16, 8))
    rows = tt * SUBLANES
    nblk = ct // bw
    vec = lambda: pl.BlockSpec((1, ct), lambda ci, ti: (0, ci))
    return pl.pallas_call(
        functools.partial(_lru_kernel, tt=tt, nblk=nblk, bw=bw),
        out_shape=jax.ShapeDtypeStruct((n, c), BF16),
        grid=(c // ct, n // rows),
        in_specs=[
            pl.BlockSpec((rows, ct), lambda ci, ti: (ti, col_u // ct + ci)),
            pl.BlockSpec((rows, ct), lambda ci, ti: (ti, col_g // ct + ci)),
            pl.BlockSpec((CONV_WIDTH, ct), lambda ci, ti: (0, ci)),
            vec(),
            pl.BlockSpec((nblk, bw, bw), lambda ci, ti: (ci, 0, 0)),
            vec(),
            pl.BlockSpec((nblk, bw, bw), lambda ci, ti: (ci, 0, 0)),
            vec(),
            vec(),
        ],
        out_specs=pl.BlockSpec((rows, ct), lambda ci, ti: (ti, ci)),
        scratch_shapes=[
            pltpu.VMEM((SUBLANES * (CONV_WIDTH - 1), ct), F32),
            pltpu.VMEM((SUBLANES, ct), F32),
            pltpu.VMEM((rows, ct), F32),
            pltpu.VMEM((rows, ct), F32),
        ],
        compiler_params=_params(("arbitrary", "arbitrary")),
        name="rg_lru",
    )(z, z, conv_w, conv_b, wa, ba, wx, bx, lam)


def _attn_kernel(q_ref, k_ref, v_ref, o_ref, st_ref, *, band, dil, slopes, qblocks):
    qi = lax.broadcasted_iota(jnp.int32, (band, 2 * band), 0)
    kj = lax.broadcasted_iota(jnp.int32, (band, 2 * band), 1)
    rel = qi - kj
    scale = HEAD_DIM ** -0.5
    nt = (((1,), (1,)), ((), ()))
    lane = lax.broadcasted_iota(jnp.int32, (band, LANES), 1)

    for qq in range(qblocks):
        n = pl.program_id(1) * qblocks + qq
        win0 = pl.multiple_of(jnp.maximum(n - 1, 0) * band, band)
        own = jnp.where(n > 0, band, 0)
        rows = slice(qq * band, (qq + 1) * band)
        q = q_ref[rows, :]
        kw = k_ref[pl.ds(win0, 2 * band), :]
        vw = v_ref[pl.ds(win0, 2 * band), :]
        dist_i = rel + own
        valid = (dist_i >= 0) & (dist_i <= band)
        dist = dist_i.astype(F32)

        outs = []
        stat = jnp.zeros((band, LANES), F32)
        for h in range(HEADS_PER_GROUP):
            sl = slice(h * HEAD_DIM, (h + 1) * HEAD_DIM)
            bias = slopes[h] * dil
            s = lax.dot_general(q[:, sl], kw[:, sl], nt, preferred_element_type=F32)
            s = jnp.where(valid, s * scale - bias * dist, -jnp.inf)
            m = jnp.max(s, axis=1, keepdims=True)
            p = jnp.exp(s - m)
            l = jnp.sum(p, axis=1, keepdims=True)
            acc = jnp.dot(p.astype(BF16), vw[:, sl], preferred_element_type=F32)
            outs.append(acc / l)
            stat = jnp.where(lane == h, m + jnp.log(l), stat)
        o_ref[rows, :] = jnp.concatenate(outs, axis=1).astype(o_ref.dtype)
        st_ref[rows, :] = stat


def _attn_out_layout(g, tm):
    per = tm // (SUBLANES * ATTN_GROUPS[g][1])
    if per > 0 and per % BF16_SUBLANES == 0:
        return True, BF16
    if per > 0 and per % SUBLANES == 0:
        return True, F32
    return False, BF16


def _attn_group(qkv, g, n, merge_tm):
    window, dil = ATTN_GROUPS[g]
    band = window // dil
    stride = SUBLANES * dil
    lp = n // stride
    gw = HEADS_PER_GROUP * HEAD_DIM
    n_heads = HEADS_PER_GROUP * len(ATTN_GROUPS)
    slopes = tuple(2.0 ** (-8.0 * (g * HEADS_PER_GROUP + h + 1) / n_heads) for h in range(HEADS_PER_GROUP))
    qblocks = _pick(lp // band, (4, 2, 1))
    tq = qblocks * band
    class_major, o_dtype = _attn_out_layout(g, merge_tm)
    o, st = pl.pallas_call(
        functools.partial(_attn_kernel, band=band, dil=float(dil), slopes=slopes, qblocks=qblocks),
        out_shape=(jax.ShapeDtypeStruct((stride, lp, gw) if class_major else (lp, stride * gw), o_dtype),
                   jax.ShapeDtypeStruct((stride, lp, LANES) if class_major else (lp, stride * LANES), F32)),
        grid=(stride, lp // tq),
        in_specs=[
            pl.BlockSpec((None, tq, gw), lambda c, i: (c, i, 0)),
            pl.BlockSpec((None, lp, gw), lambda c, i: (c, 0, 1)),
            pl.BlockSpec((None, lp, gw), lambda c, i: (c, 0, 2)),
        ],
        out_specs=((pl.BlockSpec((None, tq, gw), lambda c, i: (c, i, 0)),
                    pl.BlockSpec((None, tq, LANES), lambda c, i: (c, i, 0))) if class_major else
                   (pl.BlockSpec((tq, gw), lambda c, i: (i, c)),
                    pl.BlockSpec((tq, LANES), lambda c, i: (i, c)))),
        compiler_params=_params(("arbitrary", "arbitrary")),
        name=f"dilated_attn_g{g}",
    )(qkv, qkv, qkv)
    if class_major:
        return o, st
    return o.reshape(n, gw), st.reshape(n, LANES)


def _class_rows_to_time_major(ref, slab_ref, width):
    classes, per, _ = ref.shape
    for c in range(classes):
        for j in range(width // LANES):
            slab_ref[j, pl.ds(c, per, stride=classes), :] = ref[c, :, j * LANES:(j + 1) * LANES].astype(F32)
    return [slab_ref[j] for j in range(width // LANES)]


def _merge_kernel(x_ref, yr_ref, gr_ref, ga_ref, o0_ref, o1_ref, o2_ref, s0_ref, s1_ref, s2_ref,
                  wr_ref, wa_ref, wo_ref, g1_ref, b1_ref, x1_ref, x1p_ref, slab_ref, oslab_ref, sslab_ref,
                  *, tt, d):
    t1 = jnp.dot(yr_ref[...], wr_ref[...], preferred_element_type=F32)
    merged = _sigmoid(gr_ref[...].astype(F32)) * t1

    gw = HEADS_PER_GROUP * HEAD_DIM
    outs, stats = [], []
    for gi, (o_ref, s_ref) in enumerate(((o0_ref, s0_ref), (o1_ref, s1_ref), (o2_ref, s2_ref))):
        if len(o_ref.shape) == 3:
            outs.append(_class_rows_to_time_major(o_ref, oslab_ref.at[gi], gw))
            stats.append(_class_rows_to_time_major(s_ref, sslab_ref.at[gi], LANES)[0])
        else:
            outs.append([o_ref[:, h * HEAD_DIM:(h + 1) * HEAD_DIM].astype(F32) for h in range(HEADS_PER_GROUP)])
            stats.append(s_ref[...])

    heads = []
    for h in range(HEADS_PER_GROUP):
        lse = [s[:, h:h + 1] for s in stats]
        mx = jnp.maximum(jnp.maximum(lse[0], lse[1]), lse[2])
        e = [jnp.exp(v - mx) for v in lse]
        tot = e[0] + e[1] + e[2]
        acc = None
        for gi in range(len(ATTN_GROUPS)):
            term = (e[gi] / tot) * outs[gi][h]
            acc = term if acc is None else acc + term
        heads.append(acc)
    oa = jnp.concatenate(heads, axis=1).astype(BF16)
    t2 = jnp.dot(oa, wa_ref[...], preferred_element_type=F32)
    merged = merged + _sigmoid(ga_ref[...].astype(F32)) * t2
    mix = jnp.dot(merged.astype(BF16), wo_ref[...], preferred_element_type=F32)

    v = DN_ALPHA * _time_major_rows(x_ref, slab_ref, tt, d) + mix
    mu = jnp.mean(v, axis=-1, keepdims=True)
    cen = v - mu
    var = jnp.mean(cen * cen, axis=-1, keepdims=True)
    x1 = cen * lax.rsqrt(var + LN_EPS) * g1_ref[...] + b1_ref[...]
    x1_ref[...] = x1
    _store_packed(x1p_ref, 0, x1)


def _merge_tile(n):
    return _pick(n, (256, 128, 64))


def _merge(x, yr, z, os_, sts, wr, wa, wo, g1, b1, n, d, col_gr, col_ga):
    tm = _merge_tile(n)
    tt = tm // SUBLANES
    gw = HEADS_PER_GROUP * HEAD_DIM
    assert HEAD_DIM == LANES
    rpt = _rows_per_token(d)
    const = lambda shape: pl.BlockSpec(shape, lambda i: (0, 0), pipeline_mode=pl.Buffered(1))
    row = lambda w: pl.BlockSpec((tm, w), lambda i: (i, 0))

    def attn_spec(a, w):
        if a.ndim == 3:
            classes = a.shape[0]
            return pl.BlockSpec((classes, tm // classes, w), lambda i: (0, i, 0))
        return row(w)

    return pl.pallas_call(
        functools.partial(_merge_kernel, tt=tt, d=d),
        out_shape=(jax.ShapeDtypeStruct((n, d), F32), jax.ShapeDtypeStruct((n * rpt, LANES), U32)),
        grid=(n // tm,),
        in_specs=[
            pl.BlockSpec((SUBLANES, tt, d), lambda i: (0, i, 0)),
            row(yr.shape[1]),
            pl.BlockSpec((tm, d), lambda i: (i, col_gr // d)),
            pl.BlockSpec((tm, d), lambda i: (i, col_ga // d)),
            *[attn_spec(a, gw) for a in os_],
            *[attn_spec(a, LANES) for a in sts],
            const(wr.shape), const(wa.shape), const(wo.shape),
            const((1, d)), const((1, d)),
        ],
        out_specs=(row(d), pl.BlockSpec((tm * rpt, LANES), lambda i: (i, 0))),
        scratch_shapes=[pltpu.VMEM((d // LANES, tm, LANES), F32),
                        pltpu.VMEM((len(os_), gw // LANES, tm, LANES), F32),
                        pltpu.VMEM((len(sts), 1, tm, LANES), F32)],
        compiler_params=_params(("arbitrary",)),
        name="merge_ln1",
    )(x, yr, z, z, *os_, *sts, wr, wa, wo, g1, b1)


def _router_kernel(x_ref, wr_ref, bias_ref, idx_ref, w_ref, rank_ref, cnt_ref, base_ref, tri_ref, *, t, e):
    i = pl.program_id(0)

    @pl.when(i == 0)
    def _():
        base_ref[...] = jnp.zeros_like(base_ref)
        rr = lax.broadcasted_iota(jnp.int32, (t, t), 0)
        cc = lax.broadcasted_iota(jnp.int32, (t, t), 1)
        tri_ref[...] = jnp.where(rr < cc, 1.0, 0.0).astype(BF16)

    nt = (((1,), (1,)), ((), ()))
    logits = lax.dot_general(wr_ref[...], x_ref[...].astype(BF16), nt, preferred_element_type=F32)
    scores = _sigmoid(logits)
    biased = scores + bias_ref[...]

    per = e // N_GROUP
    sub = lax.broadcasted_iota(jnp.int32, (per, t), 0).astype(F32)
    blocks, gscore = [], []
    for g in range(N_GROUP):
        blk = biased[g * per:(g + 1) * per, :]
        m1 = jnp.max(blk, axis=0, keepdims=True)
        first = jnp.min(jnp.where(blk == m1, sub, float(per)), axis=0, keepdims=True)
        m2 = jnp.max(jnp.where(sub == first, -jnp.inf, blk), axis=0, keepdims=True)
        blocks.append(blk)
        gscore.append(m1 + m2)
    masked_blocks = []
    for g in range(N_GROUP):
        beaten = jnp.zeros((1, t), F32)
        for g2 in range(N_GROUP):
            if g2 == g:
                continue
            wins = (gscore[g2] > gscore[g]) | ((gscore[g2] == gscore[g]) & (g2 < g))
            beaten = beaten + jnp.where(wins, 1.0, 0.0)
        keep = beaten < float(TOPK_GROUP)
        masked_blocks.append(jnp.where(keep, blocks[g], -jnp.inf))
    masked = jnp.concatenate(masked_blocks, axis=0)

    eidx = lax.broadcasted_iota(jnp.int32, (e, t), 0).astype(F32)
    member = jnp.zeros((e, t), F32)
    sels, ws = [], []
    for _ in range(TOP_K):
        m = jnp.max(masked, axis=0, keepdims=True)
        sel = jnp.min(jnp.where(masked == m, eidx, float(e)), axis=0, keepdims=True)
        hit = eidx == sel
        ws.append(jnp.sum(jnp.where(hit, scores, 0.0), axis=0, keepdims=True))
        masked = jnp.where(hit, -jnp.inf, masked)
        member = jnp.where(hit, 1.0, member)
        sels.append(sel)
    wsum = ws[0]
    for k in range(1, TOP_K):
        wsum = wsum + ws[k]

    before = jnp.dot(member.astype(BF16), tri_ref[...], preferred_element_type=F32) + base_ref[...]
    for k in range(TOP_K):
        idx_ref[k:k + 1, :] = sels[k].astype(jnp.int32)
        w_ref[k:k + 1, :] = ws[k] / wsum * ROUTED_SCALE
        rk = jnp.sum(jnp.where(eidx == sels[k], before, 0.0), axis=0, keepdims=True)
        rank_ref[k:k + 1, :] = rk.astype(jnp.int32)
    base_ref[...] = base_ref[...] + jnp.sum(member, axis=1, keepdims=True)
    cnt_ref[...] = jnp.broadcast_to(base_ref[...], cnt_ref.shape)


def _router(x1, wrt, bias, n, d):
    e = wrt.shape[0]
    t = _pick(n, (512, 256, 128))
    tok = lambda dt: jax.ShapeDtypeStruct((TOP_K, n), dt)
    return pl.pallas_call(
        functools.partial(_router_kernel, t=t, e=e),
        out_shape=(tok(jnp.int32), tok(F32), tok(jnp.int32), jax.ShapeDtypeStruct((e, LANES), F32)),
        grid=(n // t,),
        in_specs=[
            pl.BlockSpec((t, d), lambda i: (i, 0)),
            pl.BlockSpec((e, d), lambda i: (0, 0)),
            pl.BlockSpec((e, 1), lambda i: (0, 0)),
        ],
        out_specs=(pl.BlockSpec((TOP_K, t), lambda i: (0, i)),
                   pl.BlockSpec((TOP_K, t), lambda i: (0, i)),
                   pl.BlockSpec((TOP_K, t), lambda i: (0, i)),
                   pl.BlockSpec((e, LANES), lambda i: (0, 0))),
        scratch_shapes=[pltpu.VMEM((e, 1), F32), pltpu.VMEM((t, t), BF16)],
        compiler_params=_params(("arbitrary",)),
        name="router_topk",
    )(x1, wrt, bias)


def _dest_kernel(idx_ref, rank_ref, start_ref, dest_ref, *, t, e):
    eidx = lax.broadcasted_iota(jnp.int32, (e, t), 0)
    for k in range(TOP_K):
        hit = eidx == idx_ref[k:k + 1, :]
        st = jnp.sum(jnp.where(hit, start_ref[...], 0.0), axis=0, keepdims=True)
        dest_ref[0, k:k + 1, :] = st.astype(jnp.int32) + rank_ref[k:k + 1, :]


def _dest(idx, rank, start_col, n, t):
    e = start_col.shape[0]
    return pl.pallas_call(
        functools.partial(_dest_kernel, t=t, e=e),
        out_shape=jax.ShapeDtypeStruct((n // t, TOP_K, t), jnp.int32),
        grid=(n // t,),
        in_specs=[
            pl.BlockSpec((TOP_K, t), lambda i: (0, i)),
            pl.BlockSpec((TOP_K, t), lambda i: (0, i)),
            pl.BlockSpec((e, 1), lambda i: (0, 0)),
        ],
        out_specs=pl.BlockSpec((1, TOP_K, t), lambda i: (i, 0, 0)),
        compiler_params=_params(("arbitrary",)),
        name="moe_dest",
    )(idx, rank, start_col)


def _token_copy(src_ref, src_tok, dst_ref, dst_tok, sem, rpt):
    s0 = pl.multiple_of(src_tok * rpt, rpt)
    d0 = pl.multiple_of(dst_tok * rpt, rpt)
    return pltpu.make_async_copy(src_ref.at[pl.ds(s0, rpt), :], dst_ref.at[pl.ds(d0, rpt), :], sem)


TOKENS_PER_SLOT_ROW = LANES // TOP_K


def _for_each_token_slots(table, t, fn):
    def body(r, c):
        for pair in range(0, TOKENS_PER_SLOT_ROW, 2):
            slots = [[table[r, (pair + p) * TOP_K + k] for k in range(TOP_K)] for p in range(2)]
            for p in range(2):
                fn(r * TOKENS_PER_SLOT_ROW + pair + p, slots[p])
        return c

    lax.fori_loop(0, t // TOKENS_PER_SLOT_ROW, body, 0)


def _slot_table_copy(dest_hbm, step, dsm, slot, sem_idx):
    return pltpu.make_async_copy(dest_hbm.at[step], dsm.at[slot], sem_idx.at[slot])


def _dispatch_kernel(dest_hbm, x_hbm, wsg_ref, wsu_ref, wsd_ref, xs_hbm, sh_ref, dsm, xring, zero_ref, sem_idx,
                     sem_in, sem, sem_pad, *, t, rpt, n_steps, pad_rows):
    i = pl.program_id(0)
    slot = i % 2
    rows = t * rpt

    def block_in(step, buf):
        r0 = pl.multiple_of(step * rows, rows)
        return pltpu.make_async_copy(x_hbm.at[pl.ds(r0, rows), :], xring.at[buf], sem_in.at[buf])

    @pl.when(i == 0)
    def _():
        _slot_table_copy(dest_hbm, 0, dsm, 0, sem_idx).start()
        block_in(0, 0).start()
        zero_ref[...] = jnp.zeros_like(zero_ref)
        pad = pltpu.make_async_copy(zero_ref, xs_hbm.at[pl.ds(n_steps * t * TOP_K * rpt, pad_rows), :], sem_pad)
        pad.start()
        pad.wait()

    _slot_table_copy(dest_hbm, i, dsm, slot, sem_idx).wait()
    block_in(i, i % 3).wait()

    @pl.when(i + 1 < n_steps)
    def _():
        _slot_table_copy(dest_hbm, i + 1, dsm, 1 - slot, sem_idx).start()
        block_in(i + 1, (i + 1) % 3).start()

    x_ref = xring.at[i % 3]

    def issue(tl, slots):
        for k in range(TOP_K):
            _token_copy(x_ref, tl, xs_hbm, slots[k], sem.at[slot], rpt).start(priority=k % 2)

    _for_each_token_slots(dsm.at[slot], t, issue)

    x = _unpack_words([x_ref[_token_rows(0, t, rpt, j), :] for j in range(rpt)]).astype(BF16)
    g = jnp.dot(x, wsg_ref[...], preferred_element_type=F32)
    u = jnp.dot(x, wsu_ref[...], preferred_element_type=F32)
    h = (g * _sigmoid(g)) * u
    sh_ref[...] = jnp.dot(h.astype(BF16), wsd_ref[...], preferred_element_type=F32).astype(sh_ref.dtype)

    def drain(parity):
        for _ in range(TOP_K):
            pltpu.make_async_copy(xring.at[0], xs_hbm.at[pl.ds(0, rows), :], sem.at[parity]).wait()

    @pl.when(i > 0)
    def _():
        drain(1 - slot)

    @pl.when(i == n_steps - 1)
    def _():
        drain(slot)


def _dispatch(dest3, x1p, wsg, wsu, wsd, n, d, t, pad_tokens):
    rpt = _rows_per_token(d)
    const = lambda shape: pl.BlockSpec(shape, lambda i: (0, 0), pipeline_mode=pl.Buffered(1))
    return pl.pallas_call(
        functools.partial(_dispatch_kernel, t=t, rpt=rpt, n_steps=n // t, pad_rows=pad_tokens * rpt),
        out_shape=(jax.ShapeDtypeStruct(((n * TOP_K + pad_tokens) * rpt, LANES), U32),
                   jax.ShapeDtypeStruct((n, d), BF16)),
        grid=(n // t,),
        in_specs=[pl.BlockSpec(memory_space=pl.ANY), pl.BlockSpec(memory_space=pl.ANY),
                  const(wsg.shape), const(wsu.shape), const(wsd.shape)],
        out_specs=(pl.BlockSpec(memory_space=pl.ANY), pl.BlockSpec((t, d), lambda i: (i, 0))),
        scratch_shapes=[pltpu.SMEM((2, TOP_K * t // LANES, LANES), jnp.int32),
                        pltpu.VMEM((3, t * rpt, LANES), U32),
                        pltpu.VMEM((pad_tokens * rpt, LANES), U32),
                        pltpu.SemaphoreType.DMA((2,)), pltpu.SemaphoreType.DMA((3,)),
                        pltpu.SemaphoreType.DMA((2,)), pltpu.SemaphoreType.DMA],
        compiler_params=_params(("arbitrary",)),
        name="moe_dispatch",
    )(dest3, x1p, wsg, wsu, wsd)


CHUNK_PARTS = 4


def _expert_kernel(r0_ref, we_ref, nw_ref, nx_ref, par_ref, parts_ref, xs_hbm, wg_hbm, wu_hbm, wd_hbm, o_hbm,
                   xin_ref, out_ref, wgf_ref, wuf_ref, wdf_ref, wgb_ref, wub_ref, wdb_ref,
                   wsem, isem, osem, *, tm, rpt):
    w = pl.program_id(0)
    nw = nw_ref[0]
    rows = tm * rpt

    def weight_copies(expert, buf):
        return [pltpu.make_async_copy(src.at[expert], dst.at[buf], wsem.at[buf])
                for src, dst in ((wg_hbm, wgf_ref), (wu_hbm, wuf_ref), (wd_hbm, wdf_ref))]

    def rows_in(item, buf):
        r0 = pl.multiple_of(r0_ref[item] * rpt, rpt)
        return pltpu.make_async_copy(xs_hbm.at[pl.ds(r0, rows), :], xin_ref.at[buf], isem.at[buf])

    def rows_out(item, buf):
        r0 = pl.multiple_of(r0_ref[item] * rpt, rpt)
        return pltpu.make_async_copy(out_ref.at[buf], o_hbm.at[pl.ds(r0, rows), :], osem.at[buf])

    @pl.when(w < nw)
    def _():
        e = we_ref[w]
        pw = jnp.maximum(w - 1, 0)
        buf_io = w % 2

        @pl.when(w == 0)
        def _():
            rows_in(0, 0).start()

        @pl.when(w + 1 < nw)
        def _():
            rows_in(w + 1, 1 - buf_io).start()

        @pl.when((w == 0) | (e != we_ref[pw]))
        def _():
            buf = par_ref[w]
            nxt = nx_ref[w]

            @pl.when(w == 0)
            def _():
                for c in weight_copies(e, buf):
                    c.start(priority=1)

            @pl.when(nxt >= 0)
            def _():
                for c in weight_copies(nxt, 1 - buf):
                    c.start(priority=1)

            for c in weight_copies(e, buf):
                c.wait()
            wgb_ref[...] = wgf_ref[buf].astype(BF16)
            wub_ref[...] = wuf_ref[buf].astype(BF16)
            wdb_ref[...] = wdf_ref[buf].astype(BF16)

        rows_in(w, buf_io).wait()
        xin = xin_ref.at[buf_io]
        out = out_ref.at[buf_io]

        def swiglu_rows(m):
            x = _unpack_words([xin[_token_rows(0, m, rpt, j), :] for j in range(rpt)]).astype(BF16)
            g = jnp.dot(x, wgb_ref[...], preferred_element_type=F32)
            u = jnp.dot(x, wub_ref[...], preferred_element_type=F32)
            h = (g * _sigmoid(g)) * u
            _store_packed(out, 0, jnp.dot(h.astype(BF16), wdb_ref[...], preferred_element_type=F32))

        quarter = tm // CHUNK_PARTS
        for parts in range(1, CHUNK_PARTS + 1):
            @pl.when(parts_ref[w] == parts)
            def _(parts=parts):
                swiglu_rows(parts * quarter)
                if parts < CHUNK_PARTS:
                    rest = (CHUNK_PARTS - parts) * quarter * rpt
                    out[pl.ds(parts * quarter * rpt, rest), :] = jnp.zeros((rest, LANES), U32)

        @pl.when(w > 0)
        def _():
            rows_out(pw, 1 - buf_io).wait()

        rows_out(w, buf_io).start()

        @pl.when(w == nw - 1)
        def _():
            rows_out(w, buf_io).wait()


def _experts(work, xs, w_gate, w_up, w_down, tm):
    _, d, de = w_gate.shape
    rpt = _rows_per_token(d)
    n_work = work[0].shape[0]
    hbm = pl.BlockSpec(memory_space=pl.ANY)
    return pl.pallas_call(
        functools.partial(_expert_kernel, tm=tm, rpt=rpt),
        out_shape=jax.ShapeDtypeStruct(xs.shape, U32),
        grid_spec=pltpu.PrefetchScalarGridSpec(
            num_scalar_prefetch=len(work),
            grid=(n_work,),
            in_specs=[hbm, hbm, hbm, hbm],
            out_specs=hbm,
            scratch_shapes=[pltpu.VMEM((2, tm * rpt, LANES), U32), pltpu.VMEM((2, tm * rpt, LANES), U32),
                            pltpu.VMEM((2, d, de), F32), pltpu.VMEM((2, d, de), F32), pltpu.VMEM((2, de, d), F32),
                            pltpu.VMEM((d, de), BF16), pltpu.VMEM((d, de), BF16), pltpu.VMEM((de, d), BF16),
                            pltpu.SemaphoreType.DMA((2,)), pltpu.SemaphoreType.DMA((2,)),
                            pltpu.SemaphoreType.DMA((2,))],
        ),
        compiler_params=_params(("arbitrary",)),
        name="moe_experts",
    )(*work, xs, w_gate, w_up, w_down)


def _combine_kernel(dest_hbm, x1_ref, wt_ref, sh_ref, o_hbm, g2_ref, b2_ref, y_ref,
                    dsm, gbuf, slab_ref, sem_idx, sem, *, t, tt, d, rpt, n_steps):
    i = pl.program_id(0)
    slot = i % 2

    def issue(step_slot):
        def gather(tl, slots):
            for k in range(TOP_K):
                _token_copy(o_hbm, slots[k], gbuf.at[step_slot], k * t + tl, sem.at[step_slot],
                            rpt).start(priority=k % 2)

        _for_each_token_slots(dsm.at[step_slot], t, gather)

    @pl.when(i == 0)
    def _():
        first = _slot_table_copy(dest_hbm, 0, dsm, 0, sem_idx)
        first.start()
        first.wait()
        issue(0)
        if n_steps > 1:
            _slot_table_copy(dest_hbm, 1, dsm, 1, sem_idx).start()

    @pl.when(i + 1 < n_steps)
    def _():
        _slot_table_copy(dest_hbm, i + 1, dsm, 1 - slot, sem_idx).wait()
        issue(1 - slot)

        @pl.when(i + 2 < n_steps)
        def _():
            _slot_table_copy(dest_hbm, i + 2, dsm, slot, sem_idx).start()

    gcur = gbuf.at[slot]
    pltpu.make_async_copy(o_hbm.at[pl.ds(0, TOP_K * t * rpt), :], gcur, sem.at[slot]).wait()

    acc_lo = [None] * rpt
    acc_hi = [None] * rpt
    for k in range(TOP_K):
        wk = wt_ref[:, k:k + 1]
        for j in range(rpt):
            words = gcur[_token_rows(k * t * rpt, t, rpt, j), :]
            lo = wk * pltpu.bitcast(words << 16, F32)
            hi = wk * pltpu.bitcast(words & U32(HI_MASK), F32)
            acc_lo[j] = lo if k == 0 else acc_lo[j] + lo
            acc_hi[j] = hi if k == 0 else acc_hi[j] + hi
    routed = jnp.concatenate(acc_lo + acc_hi, axis=1)

    v = DN_ALPHA * x1_ref[...] + (routed + sh_ref[...].astype(F32))
    mu = jnp.mean(v, axis=-1, keepdims=True)
    cen = v - mu
    var = jnp.mean(cen * cen, axis=-1, keepdims=True)
    y = cen * lax.rsqrt(var + LN_EPS) * g2_ref[...] + b2_ref[...]
    _store_batch_major(y_ref, slab_ref, y, tt, d)


def _combine(dest3, x1, wt, sh, o, g2, b2, n, d, t):
    tt = t // SUBLANES
    rpt = _rows_per_token(d)
    const = lambda shape: pl.BlockSpec(shape, lambda i: (0, 0), pipeline_mode=pl.Buffered(1))
    return pl.pallas_call(
        functools.partial(_combine_kernel, t=t, tt=tt, d=d, rpt=rpt, n_steps=n // t),
        out_shape=jax.ShapeDtypeStruct((SUBLANES, n // SUBLANES, d), F32),
        grid=(n // t,),
        in_specs=[
            pl.BlockSpec(memory_space=pl.ANY),
            pl.BlockSpec((t, d), lambda i: (i, 0)),
            pl.BlockSpec((t, TOP_K), lambda i: (i, 0)),
            pl.BlockSpec((t, d), lambda i: (i, 0)),
            pl.BlockSpec(memory_space=pl.ANY),
            const((1, d)), const((1, d)),
        ],
        out_specs=pl.BlockSpec((SUBLANES, tt, d), lambda i: (0, i, 0)),
        scratch_shapes=[pltpu.SMEM((2, TOP_K * t // LANES, LANES), jnp.int32), pltpu.VMEM((2, TOP_K * t * rpt, LANES), U32),
                        pltpu.VMEM((d // LANES, t, LANES), F32),
                        pltpu.SemaphoreType.DMA((2,)), pltpu.SemaphoreType.DMA((2,))],
        compiler_params=_params(("arbitrary",)),
        name="moe_combine_ln2",
    )(dest3, x1, wt, sh, o, g2, b2)


def _work_items(counts, m, tm):
    e = counts.shape[0]
    end = jnp.cumsum(counts)
    start = end - counts
    nchunk = (counts + tm - 1) // tm
    cum = jnp.cumsum(nchunk)
    off = cum - nchunk
    n_work = m // tm + e
    w = jnp.arange(n_work, dtype=jnp.int32)
    we = jnp.minimum(jnp.sum(cum[None, :] <= w[:, None], axis=1), e - 1).astype(jnp.int32)
    chunk = w - off[we]
    row0 = (start[we] + chunk * tm).astype(jnp.int32)
    nw = cum[-1].astype(jnp.int32)
    row0 = jnp.where(w < nw, row0, 0)
    rem = counts - (nchunk - 1) * tm
    quarter = tm // CHUNK_PARTS
    parts = jnp.where(chunk == nchunk[we] - 1, (rem[we] + quarter - 1) // quarter, CHUNK_PARTS).astype(jnp.int32)
    parts = jnp.clip(parts, 1, CHUNK_PARTS)
    ids = jnp.arange(e, dtype=jnp.int32)
    later = lax.cummin(jnp.where(counts > 0, ids, e)[::-1])[::-1]
    nxt_of = jnp.concatenate([later[1:], jnp.full((1,), e, jnp.int32)])
    nxt_of = jnp.where(nxt_of >= e, -1, nxt_of).astype(jnp.int32)
    ordinal = jnp.cumsum((counts > 0).astype(jnp.int32)) - 1
    work = (row0, we, nw.reshape(1), nxt_of[we], (ordinal[we] % 2).astype(jnp.int32), parts)
    return work, start.astype(jnp.int32)


def kernel(x, w_in, b_in, conv_w, conv_b, lru_wa, lru_ba, lru_wx, lru_bx, lru_lambda, w_rnn_br, w_attn_br, w_out, ln1_g, ln1_b, w_router, router_bias, w_gate, w_up, w_down, ws_gate, ws_up, ws_down, ln2_g, ln2_b):
    bsz, s, d = x.shape
    assert bsz == SUBLANES, "time-major rows need batch == 8"
    assert DEPTH == 1 and w_in.shape[0] == 1
    n = bsz * s
    c = lru_lambda.shape[1]
    a = HEADS_PER_GROUP * len(ATTN_GROUPS) * HEAD_DIM
    assert c == d and w_in.shape[2] == 2 * c + 3 * a + 2 * d
    layer = 0

    gw = HEADS_PER_GROUP * HEAD_DIM
    src = (0, c, 2 * c, 2 * c + a, 2 * c + 2 * a, 2 * c + 3 * a, 2 * c + 3 * a + d, 2 * c + 3 * a + 2 * d)
    part = lambda v, p, lo=0, hi=None: v[..., src[p]:src[p + 1]][..., lo:hi]
    cat = lambda v, pieces: jnp.concatenate([part(v, *p) for p in pieces], axis=-1)
    main = ((0,), (1,), (5,), (6,))
    w_main = cat(w_in[layer], main).astype(BF16)
    b_main = cat(b_in[layer], main).reshape(1, -1)
    col_u, col_g, col_gr, col_ga = 0, c, 2 * c, 2 * c + d
    row = lambda v: v.reshape(1, -1)

    z, xb = _in_proj(x, w_main, b_main, n, d)

    yr = _lru(z, conv_w[layer], row(conv_b[layer]), lru_wa[layer].astype(BF16), row(lru_ba[layer]),
              lru_wx[layer].astype(BF16), row(lru_bx[layer]), row(lru_lambda[layer]), n, c, col_u, col_g)

    os_, sts = [], []
    for g in range(len(ATTN_GROUPS)):
        cols = tuple((p, g * gw, (g + 1) * gw) for p in (2, 3, 4))
        qkv = _qkv_proj(xb, cat(w_in[layer], cols).astype(BF16), cat(b_in[layer], cols).reshape(1, -1), g, n, d)
        o, st = _attn_group(qkv, g, n, _merge_tile(n))
        os_.append(o)
        sts.append(st)

    x1, x1p = _merge(x, yr, z, os_, sts, w_rnn_br[layer].astype(BF16), w_attn_br[layer].astype(BF16),
                     w_out[layer].astype(BF16), row(ln1_g[layer]), row(ln1_b[layer]), n, d, col_gr, col_ga)

    n_exp = w_router.shape[2]
    idx, wts, rank, cnt = _router(x1, w_router[layer].T.astype(BF16), router_bias[layer].reshape(n_exp, 1), n, d)

    tm = 512
    t_tok = _pick(n, (256, 128))
    m = n * TOP_K
    counts = cnt[:, 0].astype(jnp.int32)
    work, start = _work_items(counts, m, tm)
    dest3 = _dest(idx, rank, start.astype(F32).reshape(n_exp, 1), n, t_tok)
    dest3 = dest3.transpose(0, 2, 1).reshape(n // t_tok, t_tok * TOP_K // LANES, LANES)

    xs, shared = _dispatch(dest3, x1p, ws_gate[layer].astype(BF16), ws_up[layer].astype(BF16),
                           ws_down[layer].astype(BF16), n, d, t_tok, pad_tokens=tm)
    o = _experts(work, xs, w_gate[layer], w_up[layer], w_down[layer], tm)
    y = _combine(dest3, x1, wts.T, shared, o, row(ln2_g[layer]), row(ln2_b[layer]), n, d, t_tok)
    return y
```

```python
import functools

import jax
import jax.numpy as jnp
from jax import lax
from jax.experimental import pallas as pl
from jax.experimental.pallas import tpu as pltpu

HEAD_DIM = 128
ATTN_GROUPS = ((128, 1), (512, 4), (2048, 16))
HEADS_PER_GROUP = 4
CONV_WIDTH = 4
LRU_C = 8.0
N_GROUP = 8
TOPK_GROUP = 4
TOP_K = 8
ROUTED_SCALE = 2.5
DEPTH = 1
DN_ALPHA = (2 * DEPTH) ** 0.25
LN_EPS = 1e-5

SUBLANES = 8
LANES = 128
V7X_VMEM_LIMIT_BYTES = 56 * 1024 * 1024

F32 = jnp.float32
BF16 = jnp.bfloat16


def _pick(n, cands):
    for c in cands:
        if n % c == 0:
            return c
    raise ValueError(f"no tile in {cands} divides {n}")


LOG2E = 1.4426950408889634


def _sigmoid(x):
    return 1.0 / (1.0 + jnp.exp2(x * (-LOG2E)))


def _params(sem, vmem=V7X_VMEM_LIMIT_BYTES):
    return pltpu.CompilerParams(dimension_semantics=sem, vmem_limit_bytes=vmem)


def _time_major_rows(x_ref, slab_ref, tt, d):
    for b in range(SUBLANES):
        for j in range(d // LANES):
            slab_ref[j, pl.ds(b, tt, stride=SUBLANES), :] = x_ref[b, :, j * LANES:(j + 1) * LANES]
    return jnp.concatenate([slab_ref[j] for j in range(d // LANES)], axis=1)


def _store_batch_major(y_ref, slab_ref, y, tt, d):
    for j in range(d // LANES):
        slab_ref[j] = y[:, j * LANES:(j + 1) * LANES]
    for b in range(SUBLANES):
        for j in range(d // LANES):
            y_ref[b, :, j * LANES:(j + 1) * LANES] = slab_ref[j, pl.ds(b, tt, stride=SUBLANES), :]


U32 = jnp.uint32
HI_MASK = 0xFFFF0000


def _rows_per_token(d):
    assert d % (2 * LANES) == 0
    return d // (2 * LANES)


def _token_rows(base, m, rpt, j):
    return pl.ds(base + j, m, stride=rpt)


def _pack_words(x):
    half = x.shape[1] // 2
    bits = pltpu.bitcast(x.astype(BF16).astype(F32), U32)
    words = (bits[:, :half] >> 16) | (bits[:, half:] & U32(HI_MASK))
    return [words[:, j * LANES:(j + 1) * LANES] for j in range(half // LANES)]


def _store_packed(ref, base, x):
    chunks = _pack_words(x)
    for j, chunk in enumerate(chunks):
        ref[_token_rows(base, x.shape[0], len(chunks), j), :] = chunk


def _unpack_words(chunks):
    lo = [pltpu.bitcast(c << 16, F32) for c in chunks]
    hi = [pltpu.bitcast(c & U32(HI_MASK), F32) for c in chunks]
    return jnp.concatenate(lo + hi, axis=1)


def _in_proj_kernel(x_ref, w_ref, b_ref, z_ref, xb_ref, slab_ref, *, tt, d):
    @pl.when(pl.program_id(1) == 0)
    def _():
        for b in range(SUBLANES):
            for j in range(d // LANES):
                slab_ref[j, pl.ds(b, tt, stride=SUBLANES), :] = x_ref[b, :, j * LANES:(j + 1) * LANES]
        for j in range(d // LANES):
            xb_ref[:, j * LANES:(j + 1) * LANES] = slab_ref[j].astype(BF16)

    z = jnp.dot(xb_ref[...], w_ref[...], preferred_element_type=F32) + b_ref[...]
    z_ref[...] = z.astype(z_ref.dtype)


def _in_proj(x, w, b, n, d):
    d_out = w.shape[1]
    tm = _pick(n, (1024, 512, 256, 128, 64, 32, 16))
    tn = _pick(d_out, (1024, 512, 256, 128))
    tt = tm // SUBLANES
    return pl.pallas_call(
        functools.partial(_in_proj_kernel, tt=tt, d=d),
        out_shape=(jax.ShapeDtypeStruct((n, d_out), BF16), jax.ShapeDtypeStruct((n, d), BF16)),
        grid=(n // tm, d_out // tn),
        in_specs=[
            pl.BlockSpec((SUBLANES, tt, d), lambda i, j: (0, i, 0)),
            pl.BlockSpec((d, tn), lambda i, j: (0, j)),
            pl.BlockSpec((1, tn), lambda i, j: (0, j)),
        ],
        out_specs=(pl.BlockSpec((tm, tn), lambda i, j: (i, j)),
                   pl.BlockSpec((tm, d), lambda i, j: (i, 0))),
        scratch_shapes=[pltpu.VMEM((d // LANES, tm, LANES), F32)],
        compiler_params=_params(("arbitrary", "arbitrary")),
        name="in_proj",
    )(x, w, b)


BF16_SUBLANES = 16


def _qkv_kernel(xb_ref, w_ref, b_ref, p_ref, o_ref, *, stride):
    r = (jnp.dot(xb_ref[...], w_ref[...], preferred_element_type=F32) + b_ref[...]).astype(BF16)
    sb = p_ref.shape[0]
    for s in range(r.shape[0] // (2 * sb)):
        halves = [jnp.dot(p_ref[...], r[(2 * s + i) * sb:(2 * s + i + 1) * sb, :], preferred_element_type=F32)
                  for i in range(2)]
        for c in range(stride):
            tile = jnp.concatenate([hv[c * SUBLANES:(c + 1) * SUBLANES, :] for hv in halves], axis=0)
            o_ref[c, s * BF16_SUBLANES:(s + 1) * BF16_SUBLANES, :] = tile.astype(BF16)


def _qkv_proj(xb, w, b, g, n, d):
    _, dil = ATTN_GROUPS[g]
    stride = SUBLANES * dil
    gw = HEADS_PER_GROUP * HEAD_DIM
    sb = SUBLANES * stride
    tm = max(_pick(n, (1024, 512, 256)), 2 * sb)
    per = tm // stride
    new = jnp.arange(sb, dtype=jnp.int32)
    old = (new % SUBLANES) * stride + new // SUBLANES
    perm = (old[:, None] == jnp.arange(sb, dtype=jnp.int32)[None, :]).astype(BF16)
    return pl.pallas_call(
        functools.partial(_qkv_kernel, stride=stride),
        out_shape=jax.ShapeDtypeStruct((stride, n // stride, 3 * gw), BF16),
        grid=(n // tm, 3),
        in_specs=[
            pl.BlockSpec((tm, d), lambda i, j: (i, 0)),
            pl.BlockSpec((d, gw), lambda i, j: (0, j)),
            pl.BlockSpec((1, gw), lambda i, j: (0, j)),
            pl.BlockSpec((sb, sb), lambda i, j: (0, 0), pipeline_mode=pl.Buffered(1)),
        ],
        out_specs=pl.BlockSpec((stride, per, gw), lambda i, j: (0, i, j)),
        compiler_params=_params(("arbitrary", "arbitrary")),
        name=f"qkv_proj_g{g}",
    )(xb, w, b, perm)


def _lru_kernel(u_ref, g_ref, cw_ref, cb_ref, wa_ref, ba_ref, wx_ref, bx_ref, lam_ref, y_ref,
                carry_ref, h_ref, a_ref, b_ref, *, tt, nblk, bw):
    ti = pl.program_id(1)
    rows = tt * SUBLANES
    halo = SUBLANES * (CONV_WIDTH - 1)

    @pl.when(ti == 0)
    def _():
        carry_ref[...] = jnp.zeros_like(carry_ref)
        h_ref[...] = jnp.zeros_like(h_ref)

    u = u_ref[...].astype(F32)
    ext = jnp.concatenate([carry_ref[...], u], axis=0)
    uc = cb_ref[...] + cw_ref[0:1, :] * ext[0:rows, :]
    for j in range(1, CONV_WIDTH):
        uc = uc + cw_ref[j:j + 1, :] * ext[SUBLANES * j:SUBLANES * j + rows, :]
    carry_ref[...] = u[rows - halo:, :]

    ucb = uc.astype(BF16)
    r_parts, i_parts = [], []
    for kb in range(nblk):
        blk = ucb[:, kb * bw:(kb + 1) * bw]
        r_parts.append(jnp.dot(blk, wa_ref[kb], preferred_element_type=F32))
        i_parts.append(jnp.dot(blk, wx_ref[kb], preferred_element_type=F32))
    r = _sigmoid(jnp.concatenate(r_parts, axis=1) + ba_ref[...])
    ig = _sigmoid(jnp.concatenate(i_parts, axis=1) + bx_ref[...])

    nl = -lam_ref[...]
    softplus = jnp.maximum(nl, 0.0) + jnp.log(1.0 + jnp.exp(-jnp.abs(nl)))
    a = jnp.exp2(r * ((-LRU_C * LOG2E) * softplus))
    gap = 1.0 - a * a
    mult = jnp.where(gap > 0.0, gap * lax.rsqrt(gap), 0.0)
    gated = uc * ig
    a_ref[...] = a
    b_ref[...] = gated * mult

    @pl.when(ti == 0)
    def _():
        b_ref[0:SUBLANES, :] = gated[0:SUBLANES, :]

    def step(t, h):
        r0 = pl.multiple_of(t * SUBLANES, SUBLANES)
        h = a_ref[pl.ds(r0, SUBLANES), :] * h + b_ref[pl.ds(r0, SUBLANES), :]
        b_ref[pl.ds(r0, SUBLANES), :] = h
        return h

    h_ref[...] = lax.fori_loop(0, tt, step, h_ref[...], unroll=8)

    g = g_ref[...].astype(F32)
    gelu = 0.5 * g * (1.0 + jnp.tanh(0.7978845608028654 * (g + 0.044715 * (g * g * g))))
    y_ref[...] = (b_ref[...] * gelu).astype(y_ref.dtype)


def _lru(z, conv_w, conv_b, wa, ba, wx, bx, lam, n, c, col_u, col_g):
    nblk_total, bw, _ = wa.shape
    ct = _pick(c, (512, 256, 128))
    tt = _pick(n // SUBLANES, (128, 64, 32, 16, 8))
    rows = tt * SUBLANES
    nblk = ct // bw
    vec = lambda: pl.BlockSpec((1, ct), lambda ci, ti: (0, ci))
    return pl.pallas_call(
        functools.partial(_lru_kernel, tt=tt, nblk=nblk, bw=bw),
        out_shape=jax.ShapeDtypeStruct((n, c), BF16),
        grid=(c // ct, n // rows),
        in_specs=[
            pl.BlockSpec((rows, ct), lambda ci, ti: (ti, col_u // ct + ci)),
            pl.BlockSpec((rows, ct), lambda ci, ti: (ti, col_g // ct + ci)),
            pl.BlockSpec((CONV_WIDTH, ct), lambda ci, ti: (0, ci)),
            vec(),
            pl.BlockSpec((nblk, bw, bw), lambda ci, ti: (ci, 0, 0)),
            vec(),
            pl.BlockSpec((nblk, bw, bw), lambda ci, ti: (ci, 0, 0)),
            vec(),
            vec(),
        ],
        out_specs=pl.BlockSpec((rows, ct), lambda ci, ti: (ti, ci)),
        scratch_shapes=[
            pltpu.VMEM((SUBLANES * (CONV_WIDTH - 1), ct), F32),
            pltpu.VMEM((SUBLANES, ct), F32),
            pltpu.VMEM((rows, ct), F32),
            pltpu.VMEM((rows, ct), F32),
        ],
        compiler_params=_params(("arbitrary", "arbitrary")),
        name="rg_lru",
    )(z, z, conv_w, conv_b, wa, ba, wx, bx, lam)


def _attn_kernel(q_ref, k_ref, v_ref, o_ref, st_ref, *, band, dil, slopes, qblocks):
    qi = lax.broadcasted_iota(jnp.int32, (band, 2 * band), 0)
    kj = lax.broadcasted_iota(jnp.int32, (band, 2 * band), 1)
    rel = qi - kj
    scale = HEAD_DIM ** -0.5
    nt = (((1,), (1,)), ((), ()))
    lane = lax.broadcasted_iota(jnp.int32, (band, LANES), 1)

    for qq in range(qblocks):
        n = pl.program_id(1) * qblocks + qq
        win0 = pl.multiple_of(jnp.maximum(n - 1, 0) * band, band)
        own = jnp.where(n > 0, band, 0)
        rows = slice(qq * band, (qq + 1) * band)
        q = q_ref[rows, :]
        kw = k_ref[pl.ds(win0, 2 * band), :]
        vw = v_ref[pl.ds(win0, 2 * band), :]
        dist_i = rel + own
        valid = (dist_i >= 0) & (dist_i <= band)
        dist = dist_i.astype(F32)

        outs = []
        stat = jnp.zeros((band, LANES), F32)
        for h in range(HEADS_PER_GROUP):
            sl = slice(h * HEAD_DIM, (h + 1) * HEAD_DIM)
            bias = slopes[h] * dil
            s = lax.dot_general(q[:, sl], kw[:, sl], nt, preferred_element_type=F32)
            s = jnp.where(valid, s * scale - bias * dist, -jnp.inf)
            m = jnp.max(s, axis=1, keepdims=True)
            p = jnp.exp(s - m)
            l = jnp.sum(p, axis=1, keepdims=True)
            acc = jnp.dot(p.astype(BF16), vw[:, sl], preferred_element_type=F32)
            outs.append(acc / l)
            stat = jnp.where(lane == h, m + jnp.log(l), stat)
        o_ref[rows, :] = jnp.concatenate(outs, axis=1).astype(o_ref.dtype)
        st_ref[rows, :] = stat


def _attn_out_layout(g, tm):
    per = tm // (SUBLANES * ATTN_GROUPS[g][1])
    if per > 0 and per % BF16_SUBLANES == 0:
        return True, BF16
    if per > 0 and per % SUBLANES == 0:
        return True, F32
    return False, BF16


def _attn_group(qkv, g, n, merge_tm):
    window, dil = ATTN_GROUPS[g]
    band = window // dil
    stride = SUBLANES * dil
    lp = n // stride
    gw = HEADS_PER_GROUP * HEAD_DIM
    n_heads = HEADS_PER_GROUP * len(ATTN_GROUPS)
    slopes = tuple(2.0 ** (-8.0 * (g * HEADS_PER_GROUP + h + 1) / n_heads) for h in range(HEADS_PER_GROUP))
    qblocks = _pick(lp // band, (4, 2, 1))
    tq = qblocks * band
    class_major, o_dtype = _attn_out_layout(g, merge_tm)
    o, st = pl.pallas_call(
        functools.partial(_attn_kernel, band=band, dil=float(dil), slopes=slopes, qblocks=qblocks),
        out_shape=(jax.ShapeDtypeStruct((stride, lp, gw) if class_major else (lp, stride * gw), o_dtype),
                   jax.ShapeDtypeStruct((stride, lp, LANES) if class_major else (lp, stride * LANES), F32)),
        grid=(stride, lp // tq),
        in_specs=[
            pl.BlockSpec((None, tq, gw), lambda c, i: (c, i, 0)),
            pl.BlockSpec((None, lp, gw), lambda c, i: (c, 0, 1)),
            pl.BlockSpec((None, lp, gw), lambda c, i: (c, 0, 2)),
        ],
        out_specs=((pl.BlockSpec((None, tq, gw), lambda c, i: (c, i, 0)),
                    pl.BlockSpec((None, tq, LANES), lambda c, i: (c, i, 0))) if class_major else
                   (pl.BlockSpec((tq, gw), lambda c, i: (i, c)),
                    pl.BlockSpec((tq, LANES), lambda c, i: (i, c)))),
        compiler_params=_params(("arbitrary", "arbitrary")),
        name=f"dilated_attn_g{g}",
    )(qkv, qkv, qkv)
    if class_major:
        return o, st
    return o.reshape(n, gw), st.reshape(n, LANES)


def _class_rows_to_time_major(ref, slab_ref, width):
    classes, per, _ = ref.shape
    for c in range(classes):
        for j in range(width // LANES):
            slab_ref[j, pl.ds(c, per, stride=classes), :] = ref[c, :, j * LANES:(j + 1) * LANES].astype(F32)
    return [slab_ref[j] for j in range(width // LANES)]


def _merge_kernel(x_ref, yr_ref, gr_ref, ga_ref, o0_ref, o1_ref, o2_ref, s0_ref, s1_ref, s2_ref,
                  wr_ref, wa_ref, wo_ref, g1_ref, b1_ref, x1_ref, x1p_ref, slab_ref, oslab_ref, sslab_ref,
                  *, tt, d):
    t1 = jnp.dot(yr_ref[...], wr_ref[...], preferred_element_type=F32)
    merged = _sigmoid(gr_ref[...].astype(F32)) * t1

    gw = HEADS_PER_GROUP * HEAD_DIM
    outs, stats = [], []
    for gi, (o_ref, s_ref) in enumerate(((o0_ref, s0_ref), (o1_ref, s1_ref), (o2_ref, s2_ref))):
        if len(o_ref.shape) == 3:
            outs.append(_class_rows_to_time_major(o_ref, oslab_ref.at[gi], gw))
            stats.append(_class_rows_to_time_major(s_ref, sslab_ref.at[gi], LANES)[0])
        else:
            outs.append([o_ref[:, h * HEAD_DIM:(h + 1) * HEAD_DIM].astype(F32) for h in range(HEADS_PER_GROUP)])
            stats.append(s_ref[...])

    heads = []
    for h in range(HEADS_PER_GROUP):
        lse = [s[:, h:h + 1] for s in stats]
        mx = jnp.maximum(jnp.maximum(lse[0], lse[1]), lse[2])
        e = [jnp.exp(v - mx) for v in lse]
        tot = e[0] + e[1] + e[2]
        acc = None
        for gi in range(len(ATTN_GROUPS)):
            term = (e[gi] / tot) * outs[gi][h]
            acc = term if acc is None else acc + term
        heads.append(acc)
    oa = jnp.concatenate(heads, axis=1).astype(BF16)
    t2 = jnp.dot(oa, wa_ref[...], preferred_element_type=F32)
    merged = merged + _sigmoid(ga_ref[...].astype(F32)) * t2
    mix = jnp.dot(merged.astype(BF16), wo_ref[...], preferred_element_type=F32)

    v = DN_ALPHA * _time_major_rows(x_ref, slab_ref, tt, d) + mix
    mu = jnp.mean(v, axis=-1, keepdims=True)
    cen = v - mu
    var = jnp.mean(cen * cen, axis=-1, keepdims=True)
    x1 = cen * lax.rsqrt(var + LN_EPS) * g1_ref[...] + b1_ref[...]
    x1_ref[...] = x1
    _store_packed(x1p_ref, 0, x1)


def _merge_tile(n):
    return _pick(n, (256, 128, 64))


def _merge(x, yr, z, os_, sts, wr, wa, wo, g1, b1, n, d, col_gr, col_ga):
    tm = _merge_tile(n)
    tt = tm // SUBLANES
    gw = HEADS_PER_GROUP * HEAD_DIM
    assert HEAD_DIM == LANES
    rpt = _rows_per_token(d)
    const = lambda shape: pl.BlockSpec(shape, lambda i: (0, 0), pipeline_mode=pl.Buffered(1))
    row = lambda w: pl.BlockSpec((tm, w), lambda i: (i, 0))

    def attn_spec(a, w):
        if a.ndim == 3:
            classes = a.shape[0]
            return pl.BlockSpec((classes, tm // classes, w), lambda i: (0, i, 0))
        return row(w)

    return pl.pallas_call(
        functools.partial(_merge_kernel, tt=tt, d=d),
        out_shape=(jax.ShapeDtypeStruct((n, d), F32), jax.ShapeDtypeStruct((n * rpt, LANES), U32)),
        grid=(n // tm,),
        in_specs=[
            pl.BlockSpec((SUBLANES, tt, d), lambda i: (0, i, 0)),
            row(yr.shape[1]),
            pl.BlockSpec((tm, d), lambda i: (i, col_gr // d)),
            pl.BlockSpec((tm, d), lambda i: (i, col_ga // d)),
            *[attn_spec(a, gw) for a in os_],
            *[attn_spec(a, LANES) for a in sts],
            const(wr.shape), const(wa.shape), const(wo.shape),
            const((1, d)), const((1, d)),
        ],
        out_specs=(row(d), pl.BlockSpec((tm * rpt, LANES), lambda i: (i, 0))),
        scratch_shapes=[pltpu.VMEM((d // LANES, tm, LANES), F32),
                        pltpu.VMEM((len(os_), gw // LANES, tm, LANES), F32),
                        pltpu.VMEM((len(sts), 1, tm, LANES), F32)],
        compiler_params=_params(("arbitrary",)),
        name="merge_ln1",
    )(x, yr, z, z, *os_, *sts, wr, wa, wo, g1, b1)


def _router_kernel(x_ref, wr_ref, bias_ref, idx_ref, w_ref, rank_ref, cnt_ref, base_ref, tri_ref, *, t, e):
    i = pl.program_id(0)

    @pl.when(i == 0)
    def _():
        base_ref[...] = jnp.zeros_like(base_ref)
        rr = lax.broadcasted_iota(jnp.int32, (t, t), 0)
        cc = lax.broadcasted_iota(jnp.int32, (t, t), 1)
        tri_ref[...] = jnp.where(rr < cc, 1.0, 0.0).astype(BF16)

    nt = (((1,), (1,)), ((), ()))
    logits = lax.dot_general(wr_ref[...], x_ref[...].astype(BF16), nt, preferred_element_type=F32)
    scores = _sigmoid(logits)
    biased = scores + bias_ref[...]

    per = e // N_GROUP
    sub = lax.broadcasted_iota(jnp.int32, (per, t), 0).astype(F32)
    blocks, gscore = [], []
    for g in range(N_GROUP):
        blk = biased[g * per:(g + 1) * per, :]
        m1 = jnp.max(blk, axis=0, keepdims=True)
        first = jnp.min(jnp.where(blk == m1, sub, float(per)), axis=0, keepdims=True)
        m2 = jnp.max(jnp.where(sub == first, -jnp.inf, blk), axis=0, keepdims=True)
        blocks.append(blk)
        gscore.append(m1 + m2)
    masked_blocks = []
    for g in range(N_GROUP):
        beaten = jnp.zeros((1, t), F32)
        for g2 in range(N_GROUP):
            if g2 == g:
                continue
            wins = (gscore[g2] > gscore[g]) | ((gscore[g2] == gscore[g]) & (g2 < g))
            beaten = beaten + jnp.where(wins, 1.0, 0.0)
        keep = beaten < float(TOPK_GROUP)
        masked_blocks.append(jnp.where(keep, blocks[g], -jnp.inf))
    masked = jnp.concatenate(masked_blocks, axis=0)

    eidx = lax.broadcasted_iota(jnp.int32, (e, t), 0).astype(F32)
    member = jnp.zeros((e, t), F32)
    sels, ws = [], []
    for _ in range(TOP_K):
        m = jnp.max(masked, axis=0, keepdims=True)
        sel = jnp.min(jnp.where(masked == m, eidx, float(e)), axis=0, keepdims=True)
        hit = eidx == sel
        ws.append(jnp.sum(jnp.where(hit, scores, 0.0), axis=0, keepdims=True))
        masked = jnp.where(hit, -jnp.inf, masked)
        member = jnp.where(hit, 1.0, member)
        sels.append(sel)
    wsum = ws[0]
    for k in range(1, TOP_K):
        wsum = wsum + ws[k]

    before = jnp.dot(member.astype(BF16), tri_ref[...], preferred_element_type=F32) + base_ref[...]
    for k in range(TOP_K):
        idx_ref[k:k + 1, :] = sels[k].astype(jnp.int32)
        w_ref[k:k + 1, :] = ws[k] / wsum * ROUTED_SCALE
        rk = jnp.sum(jnp.where(eidx == sels[k], before, 0.0), axis=0, keepdims=True)
        rank_ref[k:k + 1, :] = rk.astype(jnp.int32)
    base_ref[...] = base_ref[...] + jnp.sum(member, axis=1, keepdims=True)
    cnt_ref[...] = jnp.broadcast_to(base_ref[...], cnt_ref.shape)


def _router(x1, wrt, bias, n, d):
    e = wrt.shape[0]
    t = _pick(n, (512, 256, 128))
    tok = lambda dt: jax.ShapeDtypeStruct((TOP_K, n), dt)
    return pl.pallas_call(
        functools.partial(_router_kernel, t=t, e=e),
        out_shape=(tok(jnp.int32), tok(F32), tok(jnp.int32), jax.ShapeDtypeStruct((e, LANES), F32)),
        grid=(n // t,),
        in_specs=[
            pl.BlockSpec((t, d), lambda i: (i, 0)),
            pl.BlockSpec((e, d), lambda i: (0, 0)),
            pl.BlockSpec((e, 1), lambda i: (0, 0)),
        ],
        out_specs=(pl.BlockSpec((TOP_K, t), lambda i: (0, i)),
                   pl.BlockSpec((TOP_K, t), lambda i: (0, i)),
                   pl.BlockSpec((TOP_K, t), lambda i: (0, i)),
                   pl.BlockSpec((e, LANES), lambda i: (0, 0))),
        scratch_shapes=[pltpu.VMEM((e, 1), F32), pltpu.VMEM((t, t), BF16)],
        compiler_params=_params(("arbitrary",)),
        name="router_topk",
    )(x1, wrt, bias)


def _dest_kernel(idx_ref, rank_ref, start_ref, dest_ref, *, t, e):
    eidx = lax.broadcasted_iota(jnp.int32, (e, t), 0)
    for k in range(TOP_K):
        hit = eidx == idx_ref[k:k + 1, :]
        st = jnp.sum(jnp.where(hit, start_ref[...], 0.0), axis=0, keepdims=True)
        dest_ref[0, k:k + 1, :] = st.astype(jnp.int32) + rank_ref[k:k + 1, :]


def _dest(idx, rank, start_col, n, t):
    e = start_col.shape[0]
    return pl.pallas_call(
        functools.partial(_dest_kernel, t=t, e=e),
        out_shape=jax.ShapeDtypeStruct((n // t, TOP_K, t), jnp.int32),
        grid=(n // t,),
        in_specs=[
            pl.BlockSpec((TOP_K, t), lambda i: (0, i)),
            pl.BlockSpec((TOP_K, t), lambda i: (0, i)),
            pl.BlockSpec((e, 1), lambda i: (0, 0)),
        ],
        out_specs=pl.BlockSpec((1, TOP_K, t), lambda i: (i, 0, 0)),
        compiler_params=_params(("arbitrary",)),
        name="moe_dest",
    )(idx, rank, start_col)


def _token_copy(src_ref, src_tok, dst_ref, dst_tok, sem, rpt):
    s0 = pl.multiple_of(src_tok * rpt, rpt)
    d0 = pl.multiple_of(dst_tok * rpt, rpt)
    return pltpu.make_async_copy(src_ref.at[pl.ds(s0, rpt), :], dst_ref.at[pl.ds(d0, rpt), :], sem)


TOKENS_PER_SLOT_ROW = LANES // TOP_K


def _for_each_token_slots(table, t, fn):
    def body(r, c):
        for pair in range(0, TOKENS_PER_SLOT_ROW, 2):
            slots = [[table[r, (pair + p) * TOP_K + k] for k in range(TOP_K)] for p in range(2)]
            for p in range(2):
                fn(r * TOKENS_PER_SLOT_ROW + pair + p, slots[p])
        return c

    lax.fori_loop(0, t // TOKENS_PER_SLOT_ROW, body, 0)


def _slot_table_copy(dest_hbm, step, dsm, slot, sem_idx):
    return pltpu.make_async_copy(dest_hbm.at[step], dsm.at[slot], sem_idx.at[slot])


def _dispatch_kernel(dest_hbm, x_hbm, wsg_ref, wsu_ref, wsd_ref, xs_hbm, sh_ref, dsm, xring, zero_ref, sem_idx,
                     sem_in, sem, sem_pad, *, t, rpt, n_steps, pad_rows):
    i = pl.program_id(0)
    slot = i % 2
    rows = t * rpt

    def block_in(step, buf):
        r0 = pl.multiple_of(step * rows, rows)
        return pltpu.make_async_copy(x_hbm.at[pl.ds(r0, rows), :], xring.at[buf], sem_in.at[buf])

    @pl.when(i == 0)
    def _():
        _slot_table_copy(dest_hbm, 0, dsm, 0, sem_idx).start()
        block_in(0, 0).start()
        zero_ref[...] = jnp.zeros_like(zero_ref)
        pad = pltpu.make_async_copy(zero_ref, xs_hbm.at[pl.ds(n_steps * t * TOP_K * rpt, pad_rows), :], sem_pad)
        pad.start()
        pad.wait()

    _slot_table_copy(dest_hbm, i, dsm, slot, sem_idx).wait()
    block_in(i, i % 3).wait()

    @pl.when(i + 1 < n_steps)
    def _():
        _slot_table_copy(dest_hbm, i + 1, dsm, 1 - slot, sem_idx).start()
        block_in(i + 1, (i + 1) % 3).start()

    x_ref = xring.at[i % 3]

    def issue(tl, slots):
        for k in range(TOP_K):
            _token_copy(x_ref, tl, xs_hbm, slots[k], sem.at[slot], rpt).start(priority=k % 2)

    _for_each_token_slots(dsm.at[slot], t, issue)

    x = _unpack_words([x_ref[_token_rows(0, t, rpt, j), :] for j in range(rpt)]).astype(BF16)
    g = jnp.dot(x, wsg_ref[...], preferred_element_type=F32)
    u = jnp.dot(x, wsu_ref[...], preferred_element_type=F32)
    h = (g * _sigmoid(g)) * u
    sh_ref[...] = jnp.dot(h.astype(BF16), wsd_ref[...], preferred_element_type=F32).astype(sh_ref.dtype)

    def drain(parity):
        for _ in range(TOP_K):
            pltpu.make_async_copy(xring.at[0], xs_hbm.at[pl.ds(0, rows), :], sem.at[parity]).wait()

    @pl.when(i > 0)
    def _():
        drain(1 - slot)

    @pl.when(i == n_steps - 1)
    def _():
        drain(slot)


def _dispatch(dest3, x1p, wsg, wsu, wsd, n, d, t, pad_tokens):
    rpt = _rows_per_token(d)
    const = lambda shape: pl.BlockSpec(shape, lambda i: (0, 0), pipeline_mode=pl.Buffered(1))
    return pl.pallas_call(
        functools.partial(_dispatch_kernel, t=t, rpt=rpt, n_steps=n // t, pad_rows=pad_tokens * rpt),
        out_shape=(jax.ShapeDtypeStruct(((n * TOP_K + pad_tokens) * rpt, LANES), U32),
                   jax.ShapeDtypeStruct((n, d), BF16)),
        grid=(n // t,),
        in_specs=[pl.BlockSpec(memory_space=pl.ANY), pl.BlockSpec(memory_space=pl.ANY),
                  const(wsg.shape), const(wsu.shape), const(wsd.shape)],
        out_specs=(pl.BlockSpec(memory_space=pl.ANY), pl.BlockSpec((t, d), lambda i: (i, 0))),
        scratch_shapes=[pltpu.SMEM((2, TOP_K * t // LANES, LANES), jnp.int32),
                        pltpu.VMEM((3, t * rpt, LANES), U32),
                        pltpu.VMEM((pad_tokens * rpt, LANES), U32),
                        pltpu.SemaphoreType.DMA((2,)), pltpu.SemaphoreType.DMA((3,)),
                        pltpu.SemaphoreType.DMA((2,)), pltpu.SemaphoreType.DMA],
        compiler_params=_params(("arbitrary",)),
        name="moe_dispatch",
    )(dest3, x1p, wsg, wsu, wsd)


CHUNK_PARTS = 4


def _expert_kernel(r0_ref, we_ref, nw_ref, nx_ref, par_ref, parts_ref, xs_hbm, wg_hbm, wu_hbm, wd_hbm, o_hbm,
                   xin_ref, out_ref, wgf_ref, wuf_ref, wdf_ref, wgb_ref, wub_ref, wdb_ref,
                   wsem, isem, osem, *, tm, rpt):
    w = pl.program_id(0)
    nw = nw_ref[0]
    rows = tm * rpt

    def weight_copies(expert, buf):
        return [pltpu.make_async_copy(src.at[expert], dst.at[buf], wsem.at[buf])
                for src, dst in ((wg_hbm, wgf_ref), (wu_hbm, wuf_ref), (wd_hbm, wdf_ref))]

    def rows_in(item, buf):
        r0 = pl.multiple_of(r0_ref[item] * rpt, rpt)
        return pltpu.make_async_copy(xs_hbm.at[pl.ds(r0, rows), :], xin_ref.at[buf], isem.at[buf])

    def rows_out(item, buf):
        r0 = pl.multiple_of(r0_ref[item] * rpt, rpt)
        return pltpu.make_async_copy(out_ref.at[buf], o_hbm.at[pl.ds(r0, rows), :], osem.at[buf])

    @pl.when(w < nw)
    def _():
        e = we_ref[w]
        pw = jnp.maximum(w - 1, 0)
        buf_io = w % 2

        @pl.when(w == 0)
        def _():
            rows_in(0, 0).start()
            out_ref[1] = jnp.zeros(out_ref.shape[1:], U32)
            pad = pltpu.make_async_copy(out_ref.at[1], o_hbm.at[pl.ds(o_hbm.shape[0] - rows, rows), :], osem.at[1])
            pad.start()
            pad.wait()

        @pl.when(w + 1 < nw)
        def _():
            rows_in(w + 1, 1 - buf_io).start()

        @pl.when((w == 0) | (e != we_ref[pw]))
        def _():
            buf = par_ref[w]
            nxt = nx_ref[w]

            @pl.when(w == 0)
            def _():
                for c in weight_copies(e, buf):
                    c.start(priority=1)

            @pl.when(nxt >= 0)
            def _():
                for c in weight_copies(nxt, 1 - buf):
                    c.start(priority=1)

            for c in weight_copies(e, buf):
                c.wait()
            wgb_ref[...] = wgf_ref[buf].astype(BF16)
            wub_ref[...] = wuf_ref[buf].astype(BF16)
            wdb_ref[...] = wdf_ref[buf].astype(BF16)

        rows_in(w, buf_io).wait()
        xin = xin_ref.at[buf_io]
        out = out_ref.at[buf_io]

        def swiglu_rows(m):
            x = _unpack_words([xin[_token_rows(0, m, rpt, j), :] for j in range(rpt)]).astype(BF16)
            g = jnp.dot(x, wgb_ref[...], preferred_element_type=F32)
            u = jnp.dot(x, wub_ref[...], preferred_element_type=F32)
            h = (g * _sigmoid(g)) * u
            _store_packed(out, 0, jnp.dot(h.astype(BF16), wdb_ref[...], preferred_element_type=F32))

        quarter = tm // CHUNK_PARTS
        for parts in range(1, CHUNK_PARTS + 1):
            @pl.when(parts_ref[w] == parts)
            def _(parts=parts):
                swiglu_rows(parts * quarter)
                if parts < CHUNK_PARTS:
                    rest = (CHUNK_PARTS - parts) * quarter * rpt
                    out[pl.ds(parts * quarter * rpt, rest), :] = jnp.zeros((rest, LANES), U32)

        @pl.when(w > 0)
        def _():
            rows_out(pw, 1 - buf_io).wait()

        rows_out(w, buf_io).start()

        @pl.when(w == nw - 1)
        def _():
            rows_out(w, buf_io).wait()


def _experts(work, xs, w_gate, w_up, w_down, tm):
    _, d, de = w_gate.shape
    rpt = _rows_per_token(d)
    n_work = work[0].shape[0]
    hbm = pl.BlockSpec(memory_space=pl.ANY)
    return pl.pallas_call(
        functools.partial(_expert_kernel, tm=tm, rpt=rpt),
        out_shape=jax.ShapeDtypeStruct(xs.shape, U32),
        grid_spec=pltpu.PrefetchScalarGridSpec(
            num_scalar_prefetch=len(work),
            grid=(n_work,),
            in_specs=[hbm, hbm, hbm, hbm],
            out_specs=hbm,
            scratch_shapes=[pltpu.VMEM((2, tm * rpt, LANES), U32), pltpu.VMEM((2, tm * rpt, LANES), U32),
                            pltpu.VMEM((2, d, de), F32), pltpu.VMEM((2, d, de), F32), pltpu.VMEM((2, de, d), F32),
                            pltpu.VMEM((d, de), BF16), pltpu.VMEM((d, de), BF16), pltpu.VMEM((de, d), BF16),
                            pltpu.SemaphoreType.DMA((2,)), pltpu.SemaphoreType.DMA((2,)),
                            pltpu.SemaphoreType.DMA((2,))],
        ),
        compiler_params=_params(("arbitrary",)),
        name="moe_experts",
    )(*work, xs, w_gate, w_up, w_down)


def _combine_kernel(dest_hbm, x1_ref, wt_ref, sh_ref, o_hbm, g2_ref, b2_ref, y_ref,
                    dsm, gbuf, slab_ref, sem_idx, sem, *, t, tt, d, rpt, n_steps):
    i = pl.program_id(0)
    slot = i % 2

    def issue(step_slot):
        def gather(tl, slots):
            for k in range(TOP_K):
                _token_copy(o_hbm, slots[k], gbuf.at[step_slot], k * t + tl, sem.at[step_slot],
                            rpt).start(priority=k % 2)

        _for_each_token_slots(dsm.at[step_slot], t, gather)

    @pl.when(i == 0)
    def _():
        first = _slot_table_copy(dest_hbm, 0, dsm, 0, sem_idx)
        first.start()
        first.wait()
        issue(0)
        if n_steps > 1:
            _slot_table_copy(dest_hbm, 1, dsm, 1, sem_idx).start()

    @pl.when(i + 1 < n_steps)
    def _():
        _slot_table_copy(dest_hbm, i + 1, dsm, 1 - slot, sem_idx).wait()
        issue(1 - slot)

        @pl.when(i + 2 < n_steps)
        def _():
            _slot_table_copy(dest_hbm, i + 2, dsm, slot, sem_idx).start()

    gcur = gbuf.at[slot]
    pltpu.make_async_copy(o_hbm.at[pl.ds(0, TOP_K * t * rpt), :], gcur, sem.at[slot]).wait()

    acc_lo = [None] * rpt
    acc_hi = [None] * rpt
    for k in range(TOP_K):
        wk = wt_ref[:, k:k + 1]
        for j in range(rpt):
            words = gcur[_token_rows(k * t * rpt, t, rpt, j), :]
            lo = wk * pltpu.bitcast(words << 16, F32)
            hi = wk * pltpu.bitcast(words & U32(HI_MASK), F32)
            acc_lo[j] = lo if k == 0 else acc_lo[j] + lo
            acc_hi[j] = hi if k == 0 else acc_hi[j] + hi
    routed = jnp.concatenate(acc_lo + acc_hi, axis=1)

    v = DN_ALPHA * x1_ref[...] + (routed + sh_ref[...].astype(F32))
    mu = jnp.mean(v, axis=-1, keepdims=True)
    cen = v - mu
    var = jnp.mean(cen * cen, axis=-1, keepdims=True)
    y = cen * lax.rsqrt(var + LN_EPS) * g2_ref[...] + b2_ref[...]
    _store_batch_major(y_ref, slab_ref, y, tt, d)


def _combine(dest3, x1, wt, sh, o, g2, b2, n, d, t):
    tt = t // SUBLANES
    rpt = _rows_per_token(d)
    const = lambda shape: pl.BlockSpec(shape, lambda i: (0, 0), pipeline_mode=pl.Buffered(1))
    return pl.pallas_call(
        functools.partial(_combine_kernel, t=t, tt=tt, d=d, rpt=rpt, n_steps=n // t),
        out_shape=jax.ShapeDtypeStruct((SUBLANES, n // SUBLANES, d), F32),
        grid=(n // t,),
        in_specs=[
            pl.BlockSpec(memory_space=pl.ANY),
            pl.BlockSpec((t, d), lambda i: (i, 0)),
            pl.BlockSpec((t, TOP_K), lambda i: (i, 0)),
            pl.BlockSpec((t, d), lambda i: (i, 0)),
            pl.BlockSpec(memory_space=pl.ANY),
            const((1, d)), const((1, d)),
        ],
        out_specs=pl.BlockSpec((SUBLANES, tt, d), lambda i: (0, i, 0)),
        scratch_shapes=[pltpu.SMEM((2, TOP_K * t // LANES, LANES), jnp.int32), pltpu.VMEM((2, TOP_K * t * rpt, LANES), U32),
                        pltpu.VMEM((d // LANES, t, LANES), F32),
                        pltpu.SemaphoreType.DMA((2,)), pltpu.SemaphoreType.DMA((2,))],
        compiler_params=_params(("arbitrary",)),
        name="moe_combine_ln2",
    )(dest3, x1, wt, sh, o, g2, b2)


def _work_items(counts, m, tm):
    e = counts.shape[0]
    end = jnp.cumsum(counts)
    start = end - counts
    nchunk = (counts + tm - 1) // tm
    cum = jnp.cumsum(nchunk)
    off = cum - nchunk
    n_work = m // tm + e
    w = jnp.arange(n_work, dtype=jnp.int32)
    we = jnp.minimum(jnp.sum(cum[None, :] <= w[:, None], axis=1), e - 1).astype(jnp.int32)
    chunk = w - off[we]
    row0 = (start[we] + chunk * tm).astype(jnp.int32)
    nw = cum[-1].astype(jnp.int32)
    row0 = jnp.where(w < nw, row0, 0)
    rem = counts - (nchunk - 1) * tm
    quarter = tm // CHUNK_PARTS
    parts = jnp.where(chunk == nchunk[we] - 1, (rem[we] + quarter - 1) // quarter, CHUNK_PARTS).astype(jnp.int32)
    parts = jnp.clip(parts, 1, CHUNK_PARTS)
    ids = jnp.arange(e, dtype=jnp.int32)
    later = lax.cummin(jnp.where(counts > 0, ids, e)[::-1])[::-1]
    nxt_of = jnp.concatenate([later[1:], jnp.full((1,), e, jnp.int32)])
    nxt_of = jnp.where(nxt_of >= e, -1, nxt_of).astype(jnp.int32)
    ordinal = jnp.cumsum((counts > 0).astype(jnp.int32)) - 1
    work = (row0, we, nw.reshape(1), nxt_of[we], (ordinal[we] % 2).astype(jnp.int32), parts)
    return work, start.astype(jnp.int32)


def kernel(x, w_in, b_in, conv_w, conv_b, lru_wa, lru_ba, lru_wx, lru_bx, lru_lambda, w_rnn_br, w_attn_br, w_out, ln1_g, ln1_b, w_router, router_bias, w_gate, w_up, w_down, ws_gate, ws_up, ws_down, ln2_g, ln2_b):
    bsz, s, d = x.shape
    assert bsz == SUBLANES, "time-major rows need batch == 8"
    assert DEPTH == 1 and w_in.shape[0] == 1
    n = bsz * s
    c = lru_lambda.shape[1]
    a = HEADS_PER_GROUP * len(ATTN_GROUPS) * HEAD_DIM
    assert c == d and w_in.shape[2] == 2 * c + 3 * a + 2 * d
    layer = 0

    gw = HEADS_PER_GROUP * HEAD_DIM
    src = (0, c, 2 * c, 2 * c + a, 2 * c + 2 * a, 2 * c + 3 * a, 2 * c + 3 * a + d, 2 * c + 3 * a + 2 * d)
    part = lambda v, p, lo=0, hi=None: v[..., src[p]:src[p + 1]][..., lo:hi]
    cat = lambda v, pieces: jnp.concatenate([part(v, *p) for p in pieces], axis=-1)
    main = ((0,), (1,), (5,), (6,))
    w_main = cat(w_in[layer], main).astype(BF16)
    b_main = cat(b_in[layer], main).reshape(1, -1)
    col_u, col_g, col_gr, col_ga = 0, c, 2 * c, 2 * c + d
    row = lambda v: v.reshape(1, -1)

    z, xb = _in_proj(x, w_main, b_main, n, d)

    yr = _lru(z, conv_w[layer], row(conv_b[layer]), lru_wa[layer].astype(BF16), row(lru_ba[layer]),
              lru_wx[layer].astype(BF16), row(lru_bx[layer]), row(lru_lambda[layer]), n, c, col_u, col_g)

    os_, sts = [], []
    for g in range(len(ATTN_GROUPS)):
        cols = tuple((p, g * gw, (g + 1) * gw) for p in (2, 3, 4))
        qkv = _qkv_proj(xb, cat(w_in[layer], cols).astype(BF16), cat(b_in[layer], cols).reshape(1, -1), g, n, d)
        o, st = _attn_group(qkv, g, n, _merge_tile(n))
        os_.append(o)
        sts.append(st)

    x1, x1p = _merge(x, yr, z, os_, sts, w_rnn_br[layer].astype(BF16), w_attn_br[layer].astype(BF16),
                     w_out[layer].astype(BF16), row(ln1_g[layer]), row(ln1_b[layer]), n, d, col_gr, col_ga)

    n_exp = w_router.shape[2]
    idx, wts, rank, cnt = _router(x1, w_router[layer].T.astype(BF16), router_bias[layer].reshape(n_exp, 1), n, d)

    tm = 512
    t_tok = _pick(n, (256, 128))
    m = n * TOP_K
    counts = cnt[:, 0].astype(jnp.int32)
    work, start = _work_items(counts, m, tm)
    dest3 = _dest(idx, rank, start.astype(F32).reshape(n_exp, 1), n, t_tok)
    dest3 = dest3.transpose(0, 2, 1).reshape(n // t_tok, t_tok * TOP_K // LANES, LANES)

    xs, shared = _dispatch(dest3, x1p, ws_gate[layer].astype(BF16), ws_up[layer].astype(BF16),
                           ws_down[layer].astype(BF16), n, d, t_tok, pad_tokens=tm)
    o = _experts(work, xs, w_gate[layer], w_up[layer], w_down[layer], tm)
    y = _combine(dest3, x1, wts.T, shared, o, row(ln2_g[layer]), row(ln2_b[layer]), n, d, t_tok)
    return y
```
